```python
import jax, jax.numpy as jnp
from jax import lax
import numpy as np

D_MODEL = 2048
BATCH = 4
SEQ = 2048
DEPTH = 2

HEAD_DIM = 64
N_Q_HEADS = 16
N_KV_HEADS = 4
Q_PER_KV = N_Q_HEADS // N_KV_HEADS
ATTN_WIDTH = N_Q_HEADS * HEAD_DIM
KV_WIDTH = N_KV_HEADS * HEAD_DIM
WINDOW = 128
BLOCK = 128
ROPE_THETA = 10000.0
SGU_GROUPS = 8
SGU_GROUP_DIM = 128
SGU_WIDTH = SGU_GROUPS * SGU_GROUP_DIM
CHUNK = 128
N_BRANCHES = 2
D_FF = ((-(-8 * D_MODEL // 3) + 255) // 256) * 256
IN_WIDTH = ATTN_WIDTH + 2 * KV_WIDTH + 2 * SGU_WIDTH + N_BRANCHES * D_MODEL
EPS = 1e-6

kernel_name = "hybrid_gated_swa_sgu_block"


def rms_norm(x, g):
    xf = x.astype(jnp.float32)
    y = xf * lax.rsqrt(jnp.mean(xf * xf, axis=-1, keepdims=True) + EPS)
    return (y * g.astype(jnp.float32)).astype(x.dtype)


def rope_tables(seq):
    pos = jnp.arange(seq, dtype=jnp.float32)
    inv_freq = jnp.power(ROPE_THETA, -jnp.arange(0, HEAD_DIM, 2, dtype=jnp.float32) / HEAD_DIM)
    ang = pos[:, None] * inv_freq[None, :]
    return jnp.cos(ang), jnp.sin(ang)


def apply_rope(x, cos, sin):
    xf = x.astype(jnp.float32)
    half = HEAD_DIM // 2
    x1, x2 = xf[..., :half], xf[..., half:]
    c, s = cos[None, :, None, :], sin[None, :, None, :]
    return jnp.concatenate([x1 * c - x2 * s, x2 * c + x1 * s], axis=-1).astype(x.dtype)


def sliding_window_attention(q, k, v, sinks):
    B, S = q.shape[0], q.shape[1]
    nb = S // BLOCK
    qb = q.reshape(B, nb, BLOCK, N_KV_HEADS, Q_PER_KV, HEAD_DIM)
    kb = k.reshape(B, nb, BLOCK, N_KV_HEADS, HEAD_DIM)
    vb = v.reshape(B, nb, BLOCK, N_KV_HEADS, HEAD_DIM)

    def with_prev(t):
        prev = jnp.pad(t[:, :-1], ((0, 0), (1, 0), (0, 0), (0, 0), (0, 0)))
        return jnp.concatenate([prev, t], axis=2)

    kw, vw = with_prev(kb), with_prev(vb)
    scale = HEAD_DIM ** -0.5
    scores = jnp.einsum('bnqhgd,bnkhd->bnhgqk', qb, kw).astype(jnp.float32) * scale
    q_pos = jnp.arange(BLOCK)[:, None] + BLOCK
    k_pos = jnp.arange(2 * BLOCK)[None, :]
    diff = q_pos - k_pos
    band = (diff >= 0) & (diff < WINDOW)
    valid = (jnp.arange(nb)[:, None, None] > 0) | (k_pos >= BLOCK)[None]
    mask = (band[None] & valid)[None, :, None, None]
    scores = jnp.where(mask, scores, -1e30)
    sink = jnp.broadcast_to(
        sinks.astype(jnp.float32).reshape(N_KV_HEADS, Q_PER_KV)[None, None, :, :, None, None],
        scores.shape[:-1] + (1,))
    probs = jax.nn.softmax(jnp.concatenate([scores, sink], axis=-1), axis=-1)[..., :-1]
    out = jnp.einsum('bnhgqk,bnkhd->bnqhgd', probs.astype(v.dtype), vw)
    return out.reshape(B, S, ATTN_WIDTH)


def chunked_sgu(uv, w_s, b_s, ln_g, ln_b):
    B, S = uv.shape[0], uv.shape[1]
    nc = S // CHUNK
    u, v = uv[..., :SGU_WIDTH], uv[..., SGU_WIDTH:]
    vf = v.astype(jnp.float32).reshape(B, S, SGU_GROUPS, SGU_GROUP_DIM)
    mu = jnp.mean(vf, axis=-1, keepdims=True)
    var = jnp.mean(jnp.square(vf - mu), axis=-1, keepdims=True)
    vn = ((vf - mu) * lax.rsqrt(var + EPS) * ln_g.reshape(SGU_GROUPS, SGU_GROUP_DIM)
          + ln_b.reshape(SGU_GROUPS, SGU_GROUP_DIM)).astype(v.dtype)
    vc = vn.reshape(B, nc, CHUNK, SGU_GROUPS, SGU_GROUP_DIM)
    tri = jnp.tril(jnp.ones((CHUNK, CHUNK), dtype=bool))
    w = jnp.where(tri[None], w_s, jnp.zeros_like(w_s))
    s = jnp.einsum('gij,bnjgd->bnigd', w, vc) + jnp.transpose(b_s)[None, None, :, :, None]
    return u * s.reshape(B, S, SGU_WIDTH)


def setup_inputs(seed: int = 0) -> dict:
    key = jax.random.key(seed)
    ks = jax.random.split(key, 17)
    D = D_MODEL

    def nrm(k, shape, scale):
        return jax.random.normal(k, shape, jnp.float32) * scale

    return {
        "x": nrm(ks[0], (BATCH, SEQ, D), 1.0),
        "mix_norm": 1.0 + nrm(ks[1], (DEPTH, D), 0.02),
        "w_in": nrm(ks[2], (DEPTH, D, IN_WIDTH), D ** -0.5),
        "q_norm": 1.0 + nrm(ks[3], (DEPTH, HEAD_DIM), 0.02),
        "k_norm": 1.0 + nrm(ks[4], (DEPTH, HEAD_DIM), 0.02),
        "sinks": nrm(ks[5], (DEPTH, N_Q_HEADS), 0.5),
        "sgu_ln_g": 1.0 + nrm(ks[6], (DEPTH, SGU_WIDTH), 0.02),
        "sgu_ln_b": nrm(ks[7], (DEPTH, SGU_WIDTH), 0.02),
        "w_spatial": nrm(ks[8], (DEPTH, SGU_GROUPS, CHUNK, CHUNK), 0.5 * CHUNK ** -0.5),
        "b_spatial": 1.0 + nrm(ks[9], (DEPTH, SGU_GROUPS, CHUNK), 0.02),
        "w_attn_branch": nrm(ks[10], (DEPTH, ATTN_WIDTH, D), ATTN_WIDTH ** -0.5),
        "w_sgu_branch": nrm(ks[11], (DEPTH, SGU_WIDTH, D), SGU_WIDTH ** -0.5),
        "w_out": nrm(ks[12], (DEPTH, D, D), D ** -0.5),
        "ffn_norm": 1.0 + nrm(ks[13], (DEPTH, D), 0.02),
        "w_gate": nrm(ks[14], (DEPTH, D, D_FF), D ** -0.5),
        "w_up": nrm(ks[15], (DEPTH, D, D_FF), D ** -0.5),
        "w_down": nrm(ks[16], (DEPTH, D_FF, D), D_FF ** -0.5),
    }


def reference(x, mix_norm, w_in, q_norm, k_norm, sinks, sgu_ln_g, sgu_ln_b, w_spatial,
              b_spatial, w_attn_branch, w_sgu_branch, w_out, ffn_norm, w_gate, w_up, w_down):
    B, S = x.shape[0], x.shape[1]
    cos, sin = rope_tables(S)
    cuts = [ATTN_WIDTH, ATTN_WIDTH + KV_WIDTH, ATTN_WIDTH + 2 * KV_WIDTH,
            ATTN_WIDTH + 2 * KV_WIDTH + 2 * SGU_WIDTH]
    for l in range(DEPTH):
        h = rms_norm(x, mix_norm[l])
        proj = h @ w_in[l]
        q, k, v, uv, gate_logits = jnp.split(proj, cuts, axis=-1)
        q = apply_rope(rms_norm(q.reshape(B, S, N_Q_HEADS, HEAD_DIM), q_norm[l]), cos, sin)
        k = apply_rope(rms_norm(k.reshape(B, S, N_KV_HEADS, HEAD_DIM), k_norm[l]), cos, sin)
        v = v.reshape(B, S, N_KV_HEADS, HEAD_DIM)
        branch_a = sliding_window_attention(q, k, v, sinks[l]) @ w_attn_branch[l]
        branch_b = chunked_sgu(jax.nn.gelu(uv), w_spatial[l], b_spatial[l],
                               sgu_ln_g[l], sgu_ln_b[l]) @ w_sgu_branch[l]
        gates = jax.nn.sigmoid(gate_logits)
        merged = gates[..., :D_MODEL] * branch_a + gates[..., D_MODEL:] * branch_b
        x = x + merged @ w_out[l]
        h2 = rms_norm(x, ffn_norm[l])
        x = x + (jax.nn.silu(h2 @ w_gate[l]) * (h2 @ w_up[l])) @ w_down[l]
    return x
```

```python
import functools

import jax
import jax.numpy as jnp
from jax import lax
from jax.experimental import pallas as pl
from jax.experimental.pallas import tpu as pltpu

F32 = jnp.float32
BF16 = jnp.bfloat16

HEAD_DIM = 64
N_Q_HEADS = 16
N_KV_HEADS = 4
Q_PER_KV = N_Q_HEADS // N_KV_HEADS
ATTN_WIDTH = N_Q_HEADS * HEAD_DIM
KV_WIDTH = N_KV_HEADS * HEAD_DIM
WINDOW = 128
ROPE_THETA = 10000.0
SGU_GROUPS = 8
SGU_GROUP_DIM = 128
SGU_WIDTH = SGU_GROUPS * SGU_GROUP_DIM
CHUNK = 128
EPS = 1e-6
MASK_VALUE = -1e30

V7X_LANES = 128
V7X_VMEM_BYTES = 64 * 1024 * 1024
VMEM_LIMIT = V7X_VMEM_BYTES - 8 * 1024 * 1024

_ARB2 = ("arbitrary", "arbitrary")


def _params(sem=_ARB2):
    return pltpu.CompilerParams(dimension_semantics=sem, vmem_limit_bytes=VMEM_LIMIT)


def _rmsnorm_kernel(x_ref, g_ref, o_ref):
    x = x_ref[...]
    y = x * lax.rsqrt(jnp.mean(x * x, axis=-1, keepdims=True) + EPS)
    o_ref[...] = (y * g_ref[...]).astype(o_ref.dtype)


def _rmsnorm(x, gains, layer, tm=512):
    m, d = x.shape
    return pl.pallas_call(
        _rmsnorm_kernel,
        grid=(m // tm,),
        in_specs=[pl.BlockSpec((tm, d), lambda i: (i, 0)),
                  pl.BlockSpec((None, 1, d), lambda i: (layer, 0, 0))],
        out_specs=pl.BlockSpec((tm, d), lambda i: (i, 0)),
        out_shape=jax.ShapeDtypeStruct((m, d), BF16),
        compiler_params=_params(("arbitrary",)),
    )(x, gains)


def _cast_weight(w_ref, wbf_ref):
    @pl.when(pl.program_id(1) == 0)
    def _():
        wbf_ref[...] = w_ref[...].astype(BF16)


def _proj_kernel(h_ref, w_ref, o_ref, wbf_ref, *, act):
    _cast_weight(w_ref, wbf_ref)
    y = jnp.dot(h_ref[...], wbf_ref[...], preferred_element_type=F32)
    if act == "gelu":
        y = jax.nn.gelu(y, approximate=True)
    elif act == "sigmoid":
        y = jax.nn.sigmoid(y)
    o_ref[...] = y.astype(o_ref.dtype)


def _proj(h, w, layer, col0, ncols, act, out_dtype, tm=1024, tn=512):
    m, k = h.shape
    off = col0 // tn
    return pl.pallas_call(
        functools.partial(_proj_kernel, act=act),
        grid=(ncols // tn, m // tm),
        in_specs=[pl.BlockSpec((tm, k), lambda j, i: (i, 0)),
                  pl.BlockSpec((None, k, tn), lambda j, i: (layer, 0, j + off))],
        out_specs=pl.BlockSpec((tm, tn), lambda j, i: (i, j)),
        out_shape=jax.ShapeDtypeStruct((m, ncols), out_dtype),
        scratch_shapes=[pltpu.VMEM((k, tn), BF16)],
        compiler_params=_params(),
    )(h, w)


def _merge_kernel(a_ref, b_ref, wa_ref, wb_ref, ga_ref, gb_ref, o_ref, wabf_ref, wbbf_ref):
    _cast_weight(wa_ref, wabf_ref)
    _cast_weight(wb_ref, wbbf_ref)
    ya = jnp.dot(a_ref[...], wabf_ref[...], preferred_element_type=F32)
    yb = jnp.dot(b_ref[...], wbbf_ref[...], preferred_element_type=F32)
    o_ref[...] = (ga_ref[...] * ya + gb_ref[...] * yb).astype(o_ref.dtype)


def _merge(a, b, wa, wb, gates, layer, tm=1024, tn=512):
    m, k = a.shape
    d = wa.shape[-1]
    goff = d // tn
    return pl.pallas_call(
        _merge_kernel,
        grid=(d // tn, m // tm),
        in_specs=[pl.BlockSpec((tm, k), lambda j, i: (i, 0)),
                  pl.BlockSpec((tm, k), lambda j, i: (i, 0)),
                  pl.BlockSpec((None, k, tn), lambda j, i: (layer, 0, j)),
                  pl.BlockSpec((None, k, tn), lambda j, i: (layer, 0, j)),
                  pl.BlockSpec((tm, tn), lambda j, i: (i, j)),
                  pl.BlockSpec((tm, tn), lambda j, i: (i, j + goff))],
        out_specs=pl.BlockSpec((tm, tn), lambda j, i: (i, j)),
        out_shape=jax.ShapeDtypeStruct((m, d), BF16),
        scratch_shapes=[pltpu.VMEM((k, tn), BF16), pltpu.VMEM((k, tn), BF16)],
        compiler_params=_params(),
    )(a, b, wa, wb, gates, gates)


def _residual_kernel(h_ref, w_ref, x_ref, o_ref, wbf_ref):
    _cast_weight(w_ref, wbf_ref)
    y = jnp.dot(h_ref[...], wbf_ref[...], preferred_element_type=F32)
    o_ref[...] = x_ref[...] + y


def _residual_matmul(h, w, x, layer, tm, tn):
    m, k = h.shape
    d = w.shape[-1]
    return pl.pallas_call(
        _residual_kernel,
        grid=(d // tn, m // tm),
        in_specs=[pl.BlockSpec((tm, k), lambda j, i: (i, 0)),
                  pl.BlockSpec((None, k, tn), lambda j, i: (layer, 0, j)),
                  pl.BlockSpec((tm, tn), lambda j, i: (i, j))],
        out_specs=pl.BlockSpec((tm, tn), lambda j, i: (i, j)),
        out_shape=jax.ShapeDtypeStruct((m, d), F32),
        scratch_shapes=[pltpu.VMEM((k, tn), BF16)],
        compiler_params=_params(),
    )(h, w, x)


def _swiglu_kernel(h_ref, wg_ref, wu_ref, o_ref, wgbf_ref, wubf_ref):
    _cast_weight(wg_ref, wgbf_ref)
    _cast_weight(wu_ref, wubf_ref)
    h = h_ref[...]
    g = jnp.dot(h, wgbf_ref[...], preferred_element_type=F32)
    u = jnp.dot(h, wubf_ref[...], preferred_element_type=F32)
    o_ref[...] = (g * jax.nn.sigmoid(g) * u).astype(o_ref.dtype)


def _swiglu(h, wg, wu, layer, tm=1024, tn=512):
    m, k = h.shape
    f = wg.shape[-1]
    return pl.pallas_call(
        _swiglu_kernel,
        grid=(f // tn, m // tm),
        in_specs=[pl.BlockSpec((tm, k), lambda j, i: (i, 0)),
                  pl.BlockSpec((None, k, tn), lambda j, i: (layer, 0, j)),
                  pl.BlockSpec((None, k, tn), lambda j, i: (layer, 0, j))],
        out_specs=pl.BlockSpec((tm, tn), lambda j, i: (i, j)),
        out_shape=jax.ShapeDtypeStruct((m, f), BF16),
        scratch_shapes=[pltpu.VMEM((k, tn), BF16), pltpu.VMEM((k, tn), BF16)],
        compiler_params=_params(),
    )(h, wg, wu)


def _attn_kernel(sinks_ref, q_ref, kvc_ref, kvp_ref, cosc_ref, sinc_ref, cosp_ref, sinp_ref,
                 gq_ref, gk_ref, o_ref):
    n = pl.program_id(1)
    lane = lax.broadcasted_iota(jnp.int32, (1, V7X_LANES), 1)
    lo_half = lane < HEAD_DIM
    even_quarter = ((lane // (HEAD_DIM // 2)) % 2) == 0
    r = lax.broadcasted_iota(jnp.int32, (V7X_LANES, V7X_LANES), 0)
    c = lax.broadcasted_iota(jnp.int32, (V7X_LANES, V7X_LANES), 1)
    head_mean = jnp.where((r // HEAD_DIM) == (c // HEAD_DIM), 1.0 / HEAD_DIM, 0.0).astype(BF16)

    def norm_rope(x, g, cos, sin):
        ms = jnp.dot((x * x).astype(BF16), head_mean, preferred_element_type=F32)
        y = x * lax.rsqrt(ms + EPS) * g
        partner = jnp.where(even_quarter,
                            pltpu.roll(y, V7X_LANES - HEAD_DIM // 2, 1),
                            pltpu.roll(y, HEAD_DIM // 2, 1))
        return y * cos + partner * sin

    def dup_half(x, use_hi):
        swapped = pltpu.roll(x, HEAD_DIM, 1)
        return jnp.where(lo_half, swapped, x) if use_hi else jnp.where(lo_half, x, swapped)

    cosc, sinc, cosp, sinp = cosc_ref[...], sinc_ref[...], cosp_ref[...], sinp_ref[...]
    gq, gk = gq_ref[...], gk_ref[...]

    k_dup, v_dup = [], []
    for col in range(KV_WIDTH // V7X_LANES):
        sl = slice(col * V7X_LANES, (col + 1) * V7X_LANES)
        vsl = slice(KV_WIDTH + col * V7X_LANES, KV_WIDTH + (col + 1) * V7X_LANES)
        k_win = jnp.concatenate([norm_rope(kvp_ref[:, sl], gk, cosp, sinp),
                                 norm_rope(kvc_ref[:, sl], gk, cosc, sinc)], axis=0)
        v_win = jnp.concatenate([kvp_ref[:, vsl], kvc_ref[:, vsl]], axis=0)
        for hi in (False, True):
            k_dup.append(dup_half(k_win, hi).astype(BF16))
            v_dup.append(dup_half(v_win, hi).astype(BF16))

    qi = lax.broadcasted_iota(jnp.int32, (WINDOW, 2 * WINDOW), 0)
    kj = lax.broadcasted_iota(jnp.int32, (WINDOW, 2 * WINDOW), 1)
    diff = qi + WINDOW - kj
    allowed = (diff >= 0) & (diff < WINDOW) & ((kj >= WINDOW) | (n > 0))

    scale = HEAD_DIM ** -0.5
    for pair in range(N_Q_HEADS // 2):
        sl = slice(pair * V7X_LANES, (pair + 1) * V7X_LANES)
        qp = norm_rope(q_ref[:, sl], gq, cosc, sinc) * scale
        kv_head = (2 * pair) // Q_PER_KV
        outs = []
        for half in range(2):
            sink = sinks_ref[2 * pair + half]
            qm = jnp.where(lo_half if half == 0 else jnp.logical_not(lo_half), qp, 0.0).astype(BF16)
            s = lax.dot_general(qm, k_dup[kv_head], (((1,), (1,)), ((), ())),
                                preferred_element_type=F32)
            s = jnp.where(allowed, s, MASK_VALUE)
            m = jnp.maximum(jnp.max(s, axis=-1, keepdims=True), sink)
            p = jnp.exp(s - m)
            denom = jnp.sum(p, axis=-1, keepdims=True) + jnp.exp(sink - m)
            o = jnp.dot(p.astype(BF16), v_dup[kv_head], preferred_element_type=F32)
            outs.append(o / denom)
        o_ref[:, sl] = jnp.where(lo_half, outs[0], outs[1]).astype(o_ref.dtype)


def _attention(qkv, sinks, cos, sin_signed, gq, gk, batch, seq):
    m = qkv.shape[0]
    nb = seq // WINDOW
    kv_blk = ATTN_WIDTH // (2 * KV_WIDTH)

    def cur(b, n):
        return b * nb + n

    def prev(b, n):
        return b * nb + jnp.maximum(n - 1, 0)

    table = pl.BlockSpec((WINDOW, V7X_LANES), lambda b, n: (n, 0))
    table_prev = pl.BlockSpec((WINDOW, V7X_LANES), lambda b, n: (jnp.maximum(n - 1, 0), 0))
    gain = pl.BlockSpec((1, V7X_LANES), lambda b, n: (0, 0))
    return pl.pallas_call(
        _attn_kernel,
        grid=(batch, nb),
        in_specs=[pl.BlockSpec(memory_space=pltpu.SMEM),
                  pl.BlockSpec((WINDOW, ATTN_WIDTH), lambda b, n: (cur(b, n), 0)),
                  pl.BlockSpec((WINDOW, 2 * KV_WIDTH), lambda b, n: (cur(b, n), kv_blk)),
                  pl.BlockSpec((WINDOW, 2 * KV_WIDTH), lambda b, n: (prev(b, n), kv_blk)),
                  table, table, table_prev, table_prev, gain, gain],
        out_specs=pl.BlockSpec((WINDOW, ATTN_WIDTH), lambda b, n: (cur(b, n), 0)),
        out_shape=jax.ShapeDtypeStruct((m, ATTN_WIDTH), BF16),
        compiler_params=_params(),
    )(sinks, qkv, qkv, qkv, cos, sin_signed, cos, sin_signed, gq, gk)


def _sgu_kernel(u_ref, v_ref, w_ref, bt_ref, g_ref, b_ref, o_ref):
    r = lax.broadcasted_iota(jnp.int32, (CHUNK, CHUNK), 0)
    c = lax.broadcasted_iota(jnp.int32, (CHUNK, CHUNK), 1)
    causal = r >= c
    for grp in range(SGU_GROUPS):
        sl = slice(grp * SGU_GROUP_DIM, (grp + 1) * SGU_GROUP_DIM)
        v = v_ref[:, sl]
        mu = jnp.mean(v, axis=-1, keepdims=True)
        dv = v - mu
        var = jnp.mean(dv * dv, axis=-1, keepdims=True)
        vn = dv * lax.rsqrt(var + EPS) * g_ref[:, sl] + b_ref[:, sl]
        w = jnp.where(causal, w_ref[grp], 0.0).astype(BF16)
        s = jnp.dot(w, vn.astype(BF16), preferred_element_type=F32) + bt_ref[:, grp:grp + 1]
        o_ref[:, sl] = (u_ref[:, sl] * s).astype(o_ref.dtype)


def _sgu(uv, w_s, b_t, ln_g, ln_b, layer):
    m = uv.shape[0]
    return pl.pallas_call(
        _sgu_kernel,
        grid=(m // CHUNK,),
        in_specs=[pl.BlockSpec((CHUNK, SGU_WIDTH), lambda i: (i, 0)),
                  pl.BlockSpec((CHUNK, SGU_WIDTH), lambda i: (i, 1)),
                  pl.BlockSpec((None, SGU_GROUPS, CHUNK, CHUNK), lambda i: (layer, 0, 0, 0)),
                  pl.BlockSpec((None, CHUNK, SGU_GROUPS), lambda i: (layer, 0, 0)),
                  pl.BlockSpec((None, 1, SGU_WIDTH), lambda i: (layer, 0, 0)),
                  pl.BlockSpec((None, 1, SGU_WIDTH), lambda i: (layer, 0, 0))],
        out_specs=pl.BlockSpec((CHUNK, SGU_WIDTH), lambda i: (i, 0)),
        out_shape=jax.ShapeDtypeStruct((m, SGU_WIDTH), BF16),
        compiler_params=_params(("arbitrary",)),
    )(uv, uv, w_s, b_t, ln_g, ln_b)


def _rope_tables(seq):
    pos = jnp.arange(seq, dtype=F32)
    inv_freq = jnp.power(ROPE_THETA, -jnp.arange(0, HEAD_DIM, 2, dtype=F32) / HEAD_DIM)
    ang = pos[:, None] * inv_freq[None, :]
    cos, sin = jnp.cos(ang), jnp.sin(ang)
    reps = V7X_LANES // HEAD_DIM
    return jnp.tile(cos, (1, 2 * reps)), jnp.tile(jnp.concatenate([-sin, sin], axis=-1), (1, reps))


def kernel(x, mix_norm, w_in, q_norm, k_norm, sinks, sgu_ln_g, sgu_ln_b, w_spatial, b_spatial,
           w_attn_branch, w_sgu_branch, w_out, ffn_norm, w_gate, w_up, w_down):
    batch, seq, d = x.shape
    depth = w_in.shape[0]
    m = batch * seq
    cos, sin_signed = _rope_tables(seq)
    reps = V7X_LANES // HEAD_DIM
    gq = jnp.tile(q_norm, (1, reps))
    gk = jnp.tile(k_norm, (1, reps))
    b_t = jnp.swapaxes(b_spatial, 1, 2)
    mix_norm, ffn_norm, sgu_ln_g, sgu_ln_b = (
        p.reshape(depth, 1, -1) for p in (mix_norm, ffn_norm, sgu_ln_g, sgu_ln_b))
    qkv_w = ATTN_WIDTH + 2 * KV_WIDTH
    uv_w = 2 * SGU_WIDTH

    xf = x.reshape(m, d)
    for l in range(depth):
        h = _rmsnorm(xf, mix_norm, l)
        qkv = _proj(h, w_in, l, 0, qkv_w, None, F32)
        uv = _proj(h, w_in, l, qkv_w, uv_w, "gelu", F32)
        gates = _proj(h, w_in, l, qkv_w + uv_w, 2 * d, "sigmoid", F32)
        a = _attention(qkv, sinks[l], cos, sin_signed, gq[l:l + 1], gk[l:l + 1], batch, seq)
        b = _sgu(uv, w_spatial, b_t, sgu_ln_g, sgu_ln_b, l)
        merged = _merge(a, b, w_attn_branch, w_sgu_branch, gates, l)
        xf = _residual_matmul(merged, w_out, xf, l, tm=1024, tn=512)
        h2 = _rmsnorm(xf, ffn_norm, l)
        act = _swiglu(h2, w_gate, w_up, l)
        xf = _residual_matmul(act, w_down, xf, l, tm=512, tn=512)
    return xf.reshape(batch, seq, d)
```

```python
import functools

import jax
import jax.numpy as jnp
from jax import lax
from jax.experimental import pallas as pl
from jax.experimental.pallas import tpu as pltpu

F32 = jnp.float32
BF16 = jnp.bfloat16

HEAD_DIM = 64
N_Q_HEADS = 16
N_KV_HEADS = 4
Q_PER_KV = N_Q_HEADS // N_KV_HEADS
ATTN_WIDTH = N_Q_HEADS * HEAD_DIM
KV_WIDTH = N_KV_HEADS * HEAD_DIM
WINDOW = 128
ROPE_THETA = 10000.0
SGU_GROUPS = 8
SGU_GROUP_DIM = 128
SGU_WIDTH = SGU_GROUPS * SGU_GROUP_DIM
CHUNK = 128
EPS = 1e-6
MASK_VALUE = -1e30

V7X_LANES = 128
V7X_VMEM_BYTES = 64 * 1024 * 1024
VMEM_LIMIT = V7X_VMEM_BYTES - 8 * 1024 * 1024

TN = 512
QKV_TILES = (ATTN_WIDTH + 2 * KV_WIDTH) // TN
UV_TILES = 2 * SGU_WIDTH // TN


def _params(n_axes):
    return pltpu.CompilerParams(dimension_semantics=("arbitrary",) * n_axes,
                                vmem_limit_bytes=VMEM_LIMIT)


def _rmsnorm_kernel(x_ref, g_ref, o_ref):
    x = x_ref[...]
    y = x * lax.rsqrt(jnp.mean(x * x, axis=-1, keepdims=True) + EPS)
    o_ref[...] = (y * g_ref[...]).astype(o_ref.dtype)


def _rmsnorm(x, gains, layer, tm=512):
    m, d = x.shape
    return pl.pallas_call(
        _rmsnorm_kernel,
        grid=(m // tm,),
        in_specs=[pl.BlockSpec((tm, d), lambda i: (i, 0)),
                  pl.BlockSpec((None, 1, d), lambda i: (layer, 0, 0))],
        out_specs=pl.BlockSpec((tm, d), lambda i: (i, 0)),
        out_shape=jax.ShapeDtypeStruct((m, d), BF16),
        compiler_params=_params(1),
        name="rmsnorm",
    )(x, gains)


def _w_spec(k, tn, layer, nt, tile0=0):
    def index(p, i, j):
        return (layer, 0, tile0 + p * nt + jnp.where(i == 0, j, nt - 1))
    return pl.BlockSpec((None, k, tn), index)


def _stage_weights(w_refs, wbf_refs):
    j = pl.program_id(2)

    @pl.when(pl.program_id(1) == 0)
    def _():
        for w_ref, wbf_ref in zip(w_refs, wbf_refs):
            wbf_ref[j] = w_ref[...].astype(BF16)


def _in_proj_kernel(h_ref, w_ref, o_ref, wbf_ref):
    _stage_weights([w_ref], [wbf_ref])
    j = pl.program_id(2)

    def emit(act):
        y = jnp.dot(h_ref[...], wbf_ref[j], preferred_element_type=F32)
        o_ref[...] = (y if act is None else act(y)).astype(o_ref.dtype)

    pl.when(j < QKV_TILES)(lambda: emit(None))
    pl.when((j >= QKV_TILES) & (j < QKV_TILES + UV_TILES))(
        lambda: emit(functools.partial(jax.nn.gelu, approximate=True)))
    pl.when(j >= QKV_TILES + UV_TILES)(lambda: emit(jax.nn.sigmoid))


def _in_proj(h, w, layer, tm=1024):
    m, k = h.shape
    n = w.shape[-1]
    nt = n // TN
    return pl.pallas_call(
        _in_proj_kernel,
        grid=(1, m // tm, nt),
        in_specs=[pl.BlockSpec((tm, k), lambda p, i, j: (i, 0)),
                  _w_spec(k, TN, layer, nt)],
        out_specs=pl.BlockSpec((tm, TN), lambda p, i, j: (i, j)),
        out_shape=jax.ShapeDtypeStruct((m, n), F32),
        scratch_shapes=[pltpu.VMEM((nt, k, TN), BF16)],
        compiler_params=_params(3),
        name="in_proj",
    )(h, w)


def _merge_kernel(a_ref, b_ref, wa_ref, wb_ref, ga_ref, gb_ref, o_ref, wabf_ref, wbbf_ref):
    _stage_weights([wa_ref, wb_ref], [wabf_ref, wbbf_ref])
    j = pl.program_id(2)
    ya = jnp.dot(a_ref[...], wabf_ref[j], preferred_element_type=F32)
    yb = jnp.dot(b_ref[...], wbbf_ref[j], preferred_element_type=F32)
    o_ref[...] = (ga_ref[...] * ya + gb_ref[...] * yb).astype(o_ref.dtype)


def _merge(a, b, wa, wb, proj, layer, tm=1024):
    m, k = a.shape
    d = wa.shape[-1]
    nt = d // TN
    gate0 = QKV_TILES + UV_TILES
    return pl.pallas_call(
        _merge_kernel,
        grid=(1, m // tm, nt),
        in_specs=[pl.BlockSpec((tm, k), lambda p, i, j: (i, 0)),
                  pl.BlockSpec((tm, k), lambda p, i, j: (i, 0)),
                  _w_spec(k, TN, layer, nt),
                  _w_spec(k, TN, layer, nt),
                  pl.BlockSpec((tm, TN), lambda p, i, j: (i, gate0 + j)),
                  pl.BlockSpec((tm, TN), lambda p, i, j: (i, gate0 + nt + j))],
        out_specs=pl.BlockSpec((tm, TN), lambda p, i, j: (i, j)),
        out_shape=jax.ShapeDtypeStruct((m, d), BF16),
        scratch_shapes=[pltpu.VMEM((nt, k, TN), BF16), pltpu.VMEM((nt, k, TN), BF16)],
        compiler_params=_params(3),
        name="merge",
    )(a, b, wa, wb, proj, proj)


def _residual_kernel(h_ref, w_ref, x_ref, o_ref, wbf_ref):
    _stage_weights([w_ref], [wbf_ref])
    y = jnp.dot(h_ref[...], wbf_ref[pl.program_id(2)], preferred_element_type=F32)
    o_ref[...] = x_ref[...] + y


def _residual_matmul(h, w, x, layer, tm, tn, passes, name):
    m, k = h.shape
    d = w.shape[-1]
    nt = d // tn // passes
    return pl.pallas_call(
        _residual_kernel,
        grid=(passes, m // tm, nt),
        in_specs=[pl.BlockSpec((tm, k), lambda p, i, j: (i, 0)),
                  _w_spec(k, tn, layer, nt),
                  pl.BlockSpec((tm, tn), lambda p, i, j: (i, p * nt + j))],
        out_specs=pl.BlockSpec((tm, tn), lambda p, i, j: (i, p * nt + j)),
        out_shape=jax.ShapeDtypeStruct((m, d), F32),
        scratch_shapes=[pltpu.VMEM((nt, k, tn), BF16)],
        compiler_params=_params(3),
        name=name,
    )(h, w, x)


def _swiglu_kernel(h_ref, wg_ref, wu_ref, o_ref, wgbf_ref, wubf_ref):
    _stage_weights([wg_ref, wu_ref], [wgbf_ref, wubf_ref])
    j = pl.program_id(2)
    h = h_ref[...]
    g = jnp.dot(h, wgbf_ref[j], preferred_element_type=F32)
    u = jnp.dot(h, wubf_ref[j], preferred_element_type=F32)
    o_ref[...] = (g * jax.nn.sigmoid(g) * u).astype(o_ref.dtype)


def _swiglu(h, wg, wu, layer, tm=1024, tn=256, passes=2):
    m, k = h.shape
    f = wg.shape[-1]
    nt = f // tn // passes
    return pl.pallas_call(
        _swiglu_kernel,
        grid=(passes, m // tm, nt),
        in_specs=[pl.BlockSpec((tm, k), lambda p, i, j: (i, 0)),
                  _w_spec(k, tn, layer, nt),
                  _w_spec(k, tn, layer, nt)],
        out_specs=pl.BlockSpec((tm, tn), lambda p, i, j: (i, p * nt + j)),
        out_shape=jax.ShapeDtypeStruct((m, f), BF16),
        scratch_shapes=[pltpu.VMEM((nt, k, tn), BF16), pltpu.VMEM((nt, k, tn), BF16)],
        compiler_params=_params(3),
        name="swiglu",
    )(h, wg, wu)


def _attn_kernel(sinks_ref, q_ref, kvc_ref, kvp_ref, cosc_ref, sinc_ref, cosp_ref, sinp_ref,
                 gq_ref, gk_ref, o_ref):
    n = pl.program_id(1)
    lane = lax.broadcasted_iota(jnp.int32, (1, V7X_LANES), 1)
    lo_half = lane < HEAD_DIM
    even_quarter = ((lane // (HEAD_DIM // 2)) % 2) == 0
    r = lax.broadcasted_iota(jnp.int32, (V7X_LANES, V7X_LANES), 0)
    c = lax.broadcasted_iota(jnp.int32, (V7X_LANES, V7X_LANES), 1)
    head_mean = jnp.where((r // HEAD_DIM) == (c // HEAD_DIM), 1.0 / HEAD_DIM, 0.0).astype(BF16)

    def norm_rope(x, g, cos, sin):
        ms = jnp.dot((x * x).astype(BF16), head_mean, preferred_element_type=F32)
        y = x * lax.rsqrt(ms + EPS) * g
        partner = jnp.where(even_quarter,
                            pltpu.roll(y, V7X_LANES - HEAD_DIM // 2, 1),
                            pltpu.roll(y, HEAD_DIM // 2, 1))
        return y * cos + partner * sin

    def dup_half(x, use_hi):
        swapped = pltpu.roll(x, HEAD_DIM, 1)
        return jnp.where(lo_half, swapped, x) if use_hi else jnp.where(lo_half, x, swapped)

    cosc, sinc, cosp, sinp = cosc_ref[...], sinc_ref[...], cosp_ref[...], sinp_ref[...]
    gq, gk = gq_ref[...], gk_ref[...]

    k_dup, v_dup = [], []
    for col in range(KV_WIDTH // V7X_LANES):
        sl = slice(col * V7X_LANES, (col + 1) * V7X_LANES)
        vsl = slice(KV_WIDTH + col * V7X_LANES, KV_WIDTH + (col + 1) * V7X_LANES)
        k_win = jnp.concatenate([norm_rope(kvp_ref[:, sl], gk, cosp, sinp),
                                 norm_rope(kvc_ref[:, sl], gk, cosc, sinc)], axis=0)
        v_win = jnp.concatenate([kvp_ref[:, vsl], kvc_ref[:, vsl]], axis=0)
        for hi in (False, True):
            k_dup.append(dup_half(k_win, hi).astype(BF16))
            v_dup.append(dup_half(v_win, hi).astype(BF16))

    qi = lax.broadcasted_iota(jnp.int32, (WINDOW, 2 * WINDOW), 0)
    kj = lax.broadcasted_iota(jnp.int32, (WINDOW, 2 * WINDOW), 1)
    diff = qi + WINDOW - kj
    allowed = (diff >= 0) & (diff < WINDOW) & ((kj >= WINDOW) | (n > 0))

    scale = HEAD_DIM ** -0.5
    for pair in range(N_Q_HEADS // 2):
        sl = slice(pair * V7X_LANES, (pair + 1) * V7X_LANES)
        qp = norm_rope(q_ref[:, sl], gq, cosc, sinc) * scale
        kv_head = (2 * pair) // Q_PER_KV
        outs = []
        for half in range(2):
            sink = sinks_ref[2 * pair + half]
            qm = jnp.where(lo_half if half == 0 else jnp.logical_not(lo_half), qp, 0.0).astype(BF16)
            s = lax.dot_general(qm, k_dup[kv_head], (((1,), (1,)), ((), ())),
                                preferred_element_type=F32)
            s = jnp.where(allowed, s, MASK_VALUE)
            m = jnp.maximum(jnp.max(s, axis=-1, keepdims=True), sink)
            p = jnp.exp(s - m)
            denom = jnp.sum(p, axis=-1, keepdims=True) + jnp.exp(sink - m)
            o = jnp.dot(p.astype(BF16), v_dup[kv_head], preferred_element_type=F32)
            outs.append(o / denom)
        o_ref[:, sl] = jnp.where(lo_half, outs[0], outs[1]).astype(o_ref.dtype)


def _attention(proj, sinks, cos, sin_signed, gq, gk, batch, seq):
    m = proj.shape[0]
    nb = seq // WINDOW
    kv_blk = ATTN_WIDTH // (2 * KV_WIDTH)

    def cur(b, n):
        return b * nb + n

    def prev(b, n):
        return b * nb + jnp.maximum(n - 1, 0)

    table = pl.BlockSpec((WINDOW, V7X_LANES), lambda b, n: (n, 0))
    table_prev = pl.BlockSpec((WINDOW, V7X_LANES), lambda b, n: (jnp.maximum(n - 1, 0), 0))
    gain = pl.BlockSpec((1, V7X_LANES), lambda b, n: (0, 0))
    return pl.pallas_call(
        _attn_kernel,
        grid=(batch, nb),
        in_specs=[pl.BlockSpec(memory_space=pltpu.SMEM),
                  pl.BlockSpec((WINDOW, ATTN_WIDTH), lambda b, n: (cur(b, n), 0)),
                  pl.BlockSpec((WINDOW, 2 * KV_WIDTH), lambda b, n: (cur(b, n), kv_blk)),
                  pl.BlockSpec((WINDOW, 2 * KV_WIDTH), lambda b, n: (prev(b, n), kv_blk)),
                  table, table, table_prev, table_prev, gain, gain],
        out_specs=pl.BlockSpec((WINDOW, ATTN_WIDTH), lambda b, n: (cur(b, n), 0)),
        out_shape=jax.ShapeDtypeStruct((m, ATTN_WIDTH), BF16),
        compiler_params=_params(2),
        name="swa_attention",
    )(sinks, proj, proj, proj, cos, sin_signed, cos, sin_signed, gq, gk)


def _sgu_kernel(u0_ref, u1_ref, v0_ref, v1_ref, w_ref, bt_ref, g_ref, b_ref, o_ref):
    r = lax.broadcasted_iota(jnp.int32, (CHUNK, CHUNK), 0)
    c = lax.broadcasted_iota(jnp.int32, (CHUNK, CHUNK), 1)
    causal = r >= c
    groups_per_tile = TN // SGU_GROUP_DIM
    for grp in range(SGU_GROUPS):
        sl = slice(grp * SGU_GROUP_DIM, (grp + 1) * SGU_GROUP_DIM)
        tsl = slice((grp % groups_per_tile) * SGU_GROUP_DIM, (grp % groups_per_tile + 1) * SGU_GROUP_DIM)
        u_ref = (u0_ref, u1_ref)[grp // groups_per_tile]
        v_ref = (v0_ref, v1_ref)[grp // groups_per_tile]
        v = v_ref[:, tsl]
        mu = jnp.mean(v, axis=-1, keepdims=True)
        dv = v - mu
        var = jnp.mean(dv * dv, axis=-1, keepdims=True)
        vn = dv * lax.rsqrt(var + EPS) * g_ref[:, sl] + b_ref[:, sl]
        w = jnp.where(causal, w_ref[grp], 0.0).astype(BF16)
        s = jnp.dot(w, vn.astype(BF16), preferred_element_type=F32) + bt_ref[:, grp:grp + 1]
        o_ref[:, sl] = (u_ref[:, tsl] * s).astype(o_ref.dtype)


def _sgu(proj, w_s, b_t, ln_g, ln_b, layer):
    m = proj.shape[0]
    u0 = QKV_TILES

    def tile(t):
        return pl.BlockSpec((CHUNK, TN), lambda i: (i, t))

    return pl.pallas_call(
        _sgu_kernel,
        grid=(m // CHUNK,),
        in_specs=[tile(u0), tile(u0 + 1), tile(u0 + 2), tile(u0 + 3),
                  pl.BlockSpec((None, SGU_GROUPS, CHUNK, CHUNK), lambda i: (layer, 0, 0, 0)),
                  pl.BlockSpec((None, CHUNK, SGU_GROUPS), lambda i: (layer, 0, 0)),
                  pl.BlockSpec((None, 1, SGU_WIDTH), lambda i: (layer, 0, 0)),
                  pl.BlockSpec((None, 1, SGU_WIDTH), lambda i: (layer, 0, 0))],
        out_specs=pl.BlockSpec((CHUNK, SGU_WIDTH), lambda i: (i, 0)),
        out_shape=jax.ShapeDtypeStruct((m, SGU_WIDTH), BF16),
        compiler_params=_params(1),
        name="sgu",
    )(proj, proj, proj, proj, w_s, b_t, ln_g, ln_b)


def _rope_tables(seq):
    pos = jnp.arange(seq, dtype=F32)
    inv_freq = jnp.power(ROPE_THETA, -jnp.arange(0, HEAD_DIM, 2, dtype=F32) / HEAD_DIM)
    ang = pos[:, None] * inv_freq[None, :]
    cos, sin = jnp.cos(ang), jnp.sin(ang)
    reps = V7X_LANES // HEAD_DIM
    return jnp.tile(cos, (1, 2 * reps)), jnp.tile(jnp.concatenate([-sin, sin], axis=-1), (1, reps))


def kernel(x, mix_norm, w_in, q_norm, k_norm, sinks, sgu_ln_g, sgu_ln_b, w_spatial, b_spatial,
           w_attn_branch, w_sgu_branch, w_out, ffn_norm, w_gate, w_up, w_down):
    batch, seq, d = x.shape
    depth = w_in.shape[0]
    m = batch * seq
    assert w_in.shape[-1] == (QKV_TILES + UV_TILES) * TN + 2 * d
    cos, sin_signed = _rope_tables(seq)
    reps = V7X_LANES // HEAD_DIM
    gq = jnp.tile(q_norm, (1, reps))
    gk = jnp.tile(k_norm, (1, reps))
    b_t = jnp.swapaxes(b_spatial, 1, 2)
    mix_norm, ffn_norm, sgu_ln_g, sgu_ln_b = (
        p.reshape(depth, 1, -1) for p in (mix_norm, ffn_norm, sgu_ln_g, sgu_ln_b))

    xf = x.reshape(m, d)
    for l in range(depth):
        h = _rmsnorm(xf, mix_norm, l)
        proj = _in_proj(h, w_in, l)
        a = _attention(proj, sinks[l], cos, sin_signed, gq[l:l + 1], gk[l:l + 1], batch, seq)
        b = _sgu(proj, w_spatial, b_t, sgu_ln_g, sgu_ln_b, l)
        merged = _merge(a, b, w_attn_branch, w_sgu_branch, proj, l)
        xf = _residual_matmul(merged, w_out, xf, l, tm=1024, tn=TN, passes=1, name="out_proj")
        h2 = _rmsnorm(xf, ffn_norm, l)
        act = _swiglu(h2, w_gate, w_up, l)
        xf = _residual_matmul(act, w_down, xf, l, tm=512, tn=TN, passes=2, name="down_proj")
    return xf.reshape(batch, seq, d)
```

```python
import functools

import jax
import jax.numpy as jnp
from jax import lax
from jax.experimental import pallas as pl
from jax.experimental.pallas import tpu as pltpu

F32 = jnp.float32
BF16 = jnp.bfloat16

HEAD_DIM = 64
N_Q_HEADS = 16
N_KV_HEADS = 4
Q_PER_KV = N_Q_HEADS // N_KV_HEADS
ATTN_WIDTH = N_Q_HEADS * HEAD_DIM
KV_WIDTH = N_KV_HEADS * HEAD_DIM
WINDOW = 128
ROPE_THETA = 10000.0
SGU_GROUPS = 8
SGU_GROUP_DIM = 128
SGU_WIDTH = SGU_GROUPS * SGU_GROUP_DIM
CHUNK = 128
EPS = 1e-6
MASK_VALUE = -1e30

V7X_LANES = 128
V7X_VMEM_BYTES = 64 * 1024 * 1024
VMEM_LIMIT = V7X_VMEM_BYTES - 8 * 1024 * 1024

TN = 512
QKV_TILES = (ATTN_WIDTH + 2 * KV_WIDTH) // TN
UV_TILES = 2 * SGU_WIDTH // TN


def _params(n_axes):
    return pltpu.CompilerParams(dimension_semantics=("arbitrary",) * n_axes,
                                vmem_limit_bytes=VMEM_LIMIT)


def _rmsnorm_kernel(x_ref, g_ref, o_ref):
    x = x_ref[...]
    y = x * lax.rsqrt(jnp.mean(x * x, axis=-1, keepdims=True) + EPS)
    o_ref[...] = (y * g_ref[...]).astype(o_ref.dtype)


def _rmsnorm(x, gains, layer, tm=512):
    m, d = x.shape
    return pl.pallas_call(
        _rmsnorm_kernel,
        grid=(m // tm,),
        in_specs=[pl.BlockSpec((tm, d), lambda i: (i, 0)),
                  pl.BlockSpec((None, 1, d), lambda i: (layer, 0, 0))],
        out_specs=pl.BlockSpec((tm, d), lambda i: (i, 0)),
        out_shape=jax.ShapeDtypeStruct((m, d), BF16),
        compiler_params=_params(1),
        name="rmsnorm",
    )(x, gains)


def _w_spec(k, tn, layer, nt, tile0=0):
    def index(p, i, j):
        return (layer, 0, tile0 + p * nt + jnp.where(i == 0, j, nt - 1))
    return pl.BlockSpec((None, k, tn), index)


def _stage_weights(w_refs, wbf_refs):
    j = pl.program_id(2)

    @pl.when(pl.program_id(1) == 0)
    def _():
        for w_ref, wbf_ref in zip(w_refs, wbf_refs):
            wbf_ref[j] = w_ref[...].astype(BF16)


SUB_ROWS = 512


def _row_blocks(o_ref):
    tm = o_ref.shape[0]
    return [pl.ds(r, SUB_ROWS) for r in range(0, tm, SUB_ROWS)]


def _in_proj_kernel(h_ref, w_ref, o_ref, wbf_ref, *, nt):
    _stage_weights([w_ref], [wbf_ref])
    j = pl.program_id(2)
    g = pl.program_id(0) * nt + j

    def emit(act):
        for rows in _row_blocks(o_ref):
            y = jnp.dot(h_ref[rows, :], wbf_ref[j], preferred_element_type=F32)
            o_ref[rows, :] = (y if act is None else act(y)).astype(o_ref.dtype)

    pl.when(g < QKV_TILES)(lambda: emit(None))
    pl.when((g >= QKV_TILES) & (g < QKV_TILES + UV_TILES))(
        lambda: emit(functools.partial(jax.nn.gelu, approximate=True)))
    pl.when(g >= QKV_TILES + UV_TILES)(lambda: emit(jax.nn.sigmoid))


def _in_proj(h, w, layer, tm=2048, passes=3):
    m, k = h.shape
    n = w.shape[-1]
    nt = n // TN // passes
    return pl.pallas_call(
        functools.partial(_in_proj_kernel, nt=nt),
        grid=(passes, m // tm, nt),
        in_specs=[pl.BlockSpec((tm, k), lambda p, i, j: (i, 0)),
                  _w_spec(k, TN, layer, nt)],
        out_specs=pl.BlockSpec((tm, TN), lambda p, i, j: (i, p * nt + j)),
        out_shape=jax.ShapeDtypeStruct((m, n), F32),
        scratch_shapes=[pltpu.VMEM((nt, k, TN), BF16)],
        compiler_params=_params(3),
        name="in_proj",
    )(h, w)


def _merge_kernel(a_ref, b_ref, wa_ref, wb_ref, ga_ref, gb_ref, o_ref, wabf_ref, wbbf_ref):
    _stage_weights([wa_ref, wb_ref], [wabf_ref, wbbf_ref])
    j = pl.program_id(2)
    for rows in _row_blocks(o_ref):
        ya = jnp.dot(a_ref[rows, :], wabf_ref[j], preferred_element_type=F32)
        yb = jnp.dot(b_ref[rows, :], wbbf_ref[j], preferred_element_type=F32)
        o_ref[rows, :] = (ga_ref[rows, :] * ya + gb_ref[rows, :] * yb).astype(o_ref.dtype)


def _merge(a, b, wa, wb, proj, layer, tm=1024):
    m, k = a.shape
    d = wa.shape[-1]
    nt = d // TN
    gate0 = QKV_TILES + UV_TILES
    return pl.pallas_call(
        _merge_kernel,
        grid=(1, m // tm, nt),
        in_specs=[pl.BlockSpec((tm, k), lambda p, i, j: (i, 0)),
                  pl.BlockSpec((tm, k), lambda p, i, j: (i, 0)),
                  _w_spec(k, TN, layer, nt),
                  _w_spec(k, TN, layer, nt),
                  pl.BlockSpec((tm, TN), lambda p, i, j: (i, gate0 + j)),
                  pl.BlockSpec((tm, TN), lambda p, i, j: (i, gate0 + nt + j))],
        out_specs=pl.BlockSpec((tm, TN), lambda p, i, j: (i, j)),
        out_shape=jax.ShapeDtypeStruct((m, d), BF16),
        scratch_shapes=[pltpu.VMEM((nt, k, TN), BF16), pltpu.VMEM((nt, k, TN), BF16)],
        compiler_params=_params(3),
        name="merge",
    )(a, b, wa, wb, proj, proj)


def _residual_kernel(h_ref, w_ref, x_ref, o_ref, wbf_ref):
    _stage_weights([w_ref], [wbf_ref])
    j = pl.program_id(2)
    for rows in _row_blocks(o_ref):
        y = jnp.dot(h_ref[rows, :], wbf_ref[j], preferred_element_type=F32)
        o_ref[rows, :] = x_ref[rows, :] + y


def _residual_matmul(h, w, x, layer, tm, tn, passes, name):
    m, k = h.shape
    d = w.shape[-1]
    nt = d // tn // passes
    return pl.pallas_call(
        _residual_kernel,
        grid=(passes, m // tm, nt),
        in_specs=[pl.BlockSpec((tm, k), lambda p, i, j: (i, 0)),
                  _w_spec(k, tn, layer, nt),
                  pl.BlockSpec((tm, tn), lambda p, i, j: (i, p * nt + j))],
        out_specs=pl.BlockSpec((tm, tn), lambda p, i, j: (i, p * nt + j)),
        out_shape=jax.ShapeDtypeStruct((m, d), F32),
        scratch_shapes=[pltpu.VMEM((nt, k, tn), BF16)],
        compiler_params=_params(3),
        name=name,
    )(h, w, x)


def _swiglu_kernel(h_ref, wg_ref, wu_ref, o_ref, wgbf_ref, wubf_ref):
    _stage_weights([wg_ref, wu_ref], [wgbf_ref, wubf_ref])
    j = pl.program_id(2)
    for rows in _row_blocks(o_ref):
        h = h_ref[rows, :]
        g = jnp.dot(h, wgbf_ref[j], preferred_element_type=F32)
        u = jnp.dot(h, wubf_ref[j], preferred_element_type=F32)
        o_ref[rows, :] = (g * jax.nn.sigmoid(g) * u).astype(o_ref.dtype)


def _swiglu(h, wg, wu, layer, tm=2048, tn=256, passes=2):
    m, k = h.shape
    f = wg.shape[-1]
    nt = f // tn // passes
    return pl.pallas_call(
        _swiglu_kernel,
        grid=(passes, m // tm, nt),
        in_specs=[pl.BlockSpec((tm, k), lambda p, i, j: (i, 0)),
                  _w_spec(k, tn, layer, nt),
                  _w_spec(k, tn, layer, nt)],
        out_specs=pl.BlockSpec((tm, tn), lambda p, i, j: (i, p * nt + j)),
        out_shape=jax.ShapeDtypeStruct((m, f), BF16),
        scratch_shapes=[pltpu.VMEM((nt, k, tn), BF16), pltpu.VMEM((nt, k, tn), BF16)],
        compiler_params=_params(3),
        name="swiglu",
    )(h, wg, wu)


def _attn_kernel(sinks_ref, q_ref, kvc_ref, kvp_ref, cosc_ref, sinc_ref, cosp_ref, sinp_ref,
                 gq_ref, gk_ref, o_ref):
    n = pl.program_id(1)
    lane = lax.broadcasted_iota(jnp.int32, (1, V7X_LANES), 1)
    lo_half = lane < HEAD_DIM
    even_quarter = ((lane // (HEAD_DIM // 2)) % 2) == 0
    r = lax.broadcasted_iota(jnp.int32, (V7X_LANES, V7X_LANES), 0)
    c = lax.broadcasted_iota(jnp.int32, (V7X_LANES, V7X_LANES), 1)
    head_mean = jnp.where((r // HEAD_DIM) == (c // HEAD_DIM), 1.0 / HEAD_DIM, 0.0).astype(BF16)

    def norm_rope(x, g, cos, sin):
        ms = jnp.dot((x * x).astype(BF16), head_mean, preferred_element_type=F32)
        y = x * lax.rsqrt(ms + EPS) * g
        partner = jnp.where(even_quarter,
                            pltpu.roll(y, V7X_LANES - HEAD_DIM // 2, 1),
                            pltpu.roll(y, HEAD_DIM // 2, 1))
        return y * cos + partner * sin

    def dup_half(x, use_hi):
        swapped = pltpu.roll(x, HEAD_DIM, 1)
        return jnp.where(lo_half, swapped, x) if use_hi else jnp.where(lo_half, x, swapped)

    cosc, sinc, cosp, sinp = cosc_ref[...], sinc_ref[...], cosp_ref[...], sinp_ref[...]
    gq, gk = gq_ref[...], gk_ref[...]

    k_dup, v_dup = [], []
    for col in range(KV_WIDTH // V7X_LANES):
        sl = slice(col * V7X_LANES, (col + 1) * V7X_LANES)
        vsl = slice(KV_WIDTH + col * V7X_LANES, KV_WIDTH + (col + 1) * V7X_LANES)
        k_win = jnp.concatenate([norm_rope(kvp_ref[:, sl], gk, cosp, sinp),
                                 norm_rope(kvc_ref[:, sl], gk, cosc, sinc)], axis=0)
        v_win = jnp.concatenate([kvp_ref[:, vsl], kvc_ref[:, vsl]], axis=0)
        for hi in (False, True):
            k_dup.append(dup_half(k_win, hi).astype(BF16))
            v_dup.append(dup_half(v_win, hi).astype(BF16))

    qi = lax.broadcasted_iota(jnp.int32, (WINDOW, 2 * WINDOW), 0)
    kj = lax.broadcasted_iota(jnp.int32, (WINDOW, 2 * WINDOW), 1)
    diff = qi + WINDOW - kj
    allowed = (diff >= 0) & (diff < WINDOW) & ((kj >= WINDOW) | (n > 0))

    scale = HEAD_DIM ** -0.5
    for pair in range(N_Q_HEADS // 2):
        sl = slice(pair * V7X_LANES, (pair + 1) * V7X_LANES)
        qp = norm_rope(q_ref[:, sl], gq, cosc, sinc) * scale
        kv_head = (2 * pair) // Q_PER_KV
        outs = []
        for half in range(2):
            sink = sinks_ref[2 * pair + half]
            qm = jnp.where(lo_half if half == 0 else jnp.logical_not(lo_half), qp, 0.0).astype(BF16)
            s = lax.dot_general(qm, k_dup[kv_head], (((1,), (1,)), ((), ())),
                                preferred_element_type=F32)
            s = jnp.where(allowed, s, MASK_VALUE)
            m = jnp.maximum(jnp.max(s, axis=-1, keepdims=True), sink)
            p = jnp.exp(s - m)
            denom = jnp.sum(p, axis=-1, keepdims=True) + jnp.exp(sink - m)
            o = jnp.dot(p.astype(BF16), v_dup[kv_head], preferred_element_type=F32)
            outs.append(o / denom)
        o_ref[:, sl] = jnp.where(lo_half, outs[0], outs[1]).astype(o_ref.dtype)


def _attention(proj, sinks, cos, sin_signed, gq, gk, batch, seq):
    m = proj.shape[0]
    nb = seq // WINDOW
    kv_blk = ATTN_WIDTH // (2 * KV_WIDTH)

    def cur(b, n):
        return b * nb + n

    def prev(b, n):
        return b * nb + jnp.maximum(n - 1, 0)

    table = pl.BlockSpec((WINDOW, V7X_LANES), lambda b, n: (n, 0))
    table_prev = pl.BlockSpec((WINDOW, V7X_LANES), lambda b, n: (jnp.maximum(n - 1, 0), 0))
    gain = pl.BlockSpec((1, V7X_LANES), lambda b, n: (0, 0))
    return pl.pallas_call(
        _attn_kernel,
        grid=(batch, nb),
        in_specs=[pl.BlockSpec(memory_space=pltpu.SMEM),
                  pl.BlockSpec((WINDOW, ATTN_WIDTH), lambda b, n: (cur(b, n), 0)),
                  pl.BlockSpec((WINDOW, 2 * KV_WIDTH), lambda b, n: (cur(b, n), kv_blk)),
                  pl.BlockSpec((WINDOW, 2 * KV_WIDTH), lambda b, n: (prev(b, n), kv_blk)),
                  table, table, table_prev, table_prev, gain, gain],
        out_specs=pl.BlockSpec((WINDOW, ATTN_WIDTH), lambda b, n: (cur(b, n), 0)),
        out_shape=jax.ShapeDtypeStruct((m, ATTN_WIDTH), BF16),
        compiler_params=_params(2),
        name="swa_attention",
    )(sinks, proj, proj, proj, cos, sin_signed, cos, sin_signed, gq, gk)


def _sgu_kernel(u0_ref, u1_ref, v0_ref, v1_ref, w_ref, bt_ref, g_ref, b_ref, o_ref):
    r = lax.broadcasted_iota(jnp.int32, (CHUNK, CHUNK), 0)
    c = lax.broadcasted_iota(jnp.int32, (CHUNK, CHUNK), 1)
    causal = r >= c
    groups_per_tile = TN // SGU_GROUP_DIM
    for grp in range(SGU_GROUPS):
        sl = slice(grp * SGU_GROUP_DIM, (grp + 1) * SGU_GROUP_DIM)
        tsl = slice((grp % groups_per_tile) * SGU_GROUP_DIM, (grp % groups_per_tile + 1) * SGU_GROUP_DIM)
        u_ref = (u0_ref, u1_ref)[grp // groups_per_tile]
        v_ref = (v0_ref, v1_ref)[grp // groups_per_tile]
        v = v_ref[:, tsl]
        mu = jnp.mean(v, axis=-1, keepdims=True)
        dv = v - mu
        var = jnp.mean(dv * dv, axis=-1, keepdims=True)
        vn = dv * lax.rsqrt(var + EPS) * g_ref[:, sl] + b_ref[:, sl]
        w = jnp.where(causal, w_ref[grp], 0.0).astype(BF16)
        s = jnp.dot(w, vn.astype(BF16), preferred_element_type=F32) + bt_ref[:, grp:grp + 1]
        o_ref[:, sl] = (u_ref[:, tsl] * s).astype(o_ref.dtype)


def _sgu(proj, w_s, b_t, ln_g, ln_b, layer):
    m = proj.shape[0]
    u0 = QKV_TILES

    def tile(t):
        return pl.BlockSpec((CHUNK, TN), lambda i: (i, t))

    return pl.pallas_call(
        _sgu_kernel,
        grid=(m // CHUNK,),
        in_specs=[tile(u0), tile(u0 + 1), tile(u0 + 2), tile(u0 + 3),
                  pl.BlockSpec((None, SGU_GROUPS, CHUNK, CHUNK), lambda i: (layer, 0, 0, 0)),
                  pl.BlockSpec((None, CHUNK, SGU_GROUPS), lambda i: (layer, 0, 0)),
                  pl.BlockSpec((None, 1, SGU_WIDTH), lambda i: (layer, 0, 0)),
                  pl.BlockSpec((None, 1, SGU_WIDTH), lambda i: (layer, 0, 0))],
        out_specs=pl.BlockSpec((CHUNK, SGU_WIDTH), lambda i: (i, 0)),
        out_shape=jax.ShapeDtypeStruct((m, SGU_WIDTH), BF16),
        compiler_params=_params(1),
        name="sgu",
    )(proj, proj, proj, proj, w_s, b_t, ln_g, ln_b)


def _rope_tables(seq):
    pos = jnp.arange(seq, dtype=F32)
    inv_freq = jnp.power(ROPE_THETA, -jnp.arange(0, HEAD_DIM, 2, dtype=F32) / HEAD_DIM)
    ang = pos[:, None] * inv_freq[None, :]
    cos, sin = jnp.cos(ang), jnp.sin(ang)
    reps = V7X_LANES // HEAD_DIM
    return jnp.tile(cos, (1, 2 * reps)), jnp.tile(jnp.concatenate([-sin, sin], axis=-1), (1, reps))


def kernel(x, mix_norm, w_in, q_norm, k_norm, sinks, sgu_ln_g, sgu_ln_b, w_spatial, b_spatial,
           w_attn_branch, w_sgu_branch, w_out, ffn_norm, w_gate, w_up, w_down):
    batch, seq, d = x.shape
    depth = w_in.shape[0]
    m = batch * seq
    assert w_in.shape[-1] == (QKV_TILES + UV_TILES) * TN + 2 * d
    cos, sin_signed = _rope_tables(seq)
    reps = V7X_LANES // HEAD_DIM
    gq = jnp.tile(q_norm, (1, reps))
    gk = jnp.tile(k_norm, (1, reps))
    b_t = jnp.swapaxes(b_spatial, 1, 2)
    mix_norm, ffn_norm, sgu_ln_g, sgu_ln_b = (
        p.reshape(depth, 1, -1) for p in (mix_norm, ffn_norm, sgu_ln_g, sgu_ln_b))

    xf = x.reshape(m, d)
    for l in range(depth):
        h = _rmsnorm(xf, mix_norm, l)
        proj = _in_proj(h, w_in, l)
        a = _attention(proj, sinks[l], cos, sin_signed, gq[l:l + 1], gk[l:l + 1], batch, seq)
        b = _sgu(proj, w_spatial, b_t, sgu_ln_g, sgu_ln_b, l)
        merged = _merge(a, b, w_attn_branch, w_sgu_branch, proj, l)
        xf = _residual_matmul(merged, w_out, xf, l, tm=2048, tn=TN, passes=1, name="out_proj")
        h2 = _rmsnorm(xf, ffn_norm, l)
        act = _swiglu(h2, w_gate, w_up, l)
        xf = _residual_matmul(act, w_down, xf, l, tm=1024, tn=256, passes=2, name="down_proj")
    return xf.reshape(batch, seq, d)
```

```python
import functools

import jax
import jax.numpy as jnp
from jax import lax
from jax.experimental import pallas as pl
from jax.experimental.pallas import tpu as pltpu

F32 = jnp.float32
BF16 = jnp.bfloat16

HEAD_DIM = 64
N_Q_HEADS = 16
N_KV_HEADS = 4
Q_PER_KV = N_Q_HEADS // N_KV_HEADS
ATTN_WIDTH = N_Q_HEADS * HEAD_DIM
KV_WIDTH = N_KV_HEADS * HEAD_DIM
WINDOW = 128
ROPE_THETA = 10000.0
SGU_GROUPS = 8
SGU_GROUP_DIM = 128
SGU_WIDTH = SGU_GROUPS * SGU_GROUP_DIM
CHUNK = 128
EPS = 1e-6
MASK_VALUE = -1e30

V7X_LANES = 128
V7X_VMEM_BYTES = 64 * 1024 * 1024
VMEM_LIMIT = V7X_VMEM_BYTES - 8 * 1024 * 1024

TN = 512
QKV_TILES = (ATTN_WIDTH + 2 * KV_WIDTH) // TN
UV_TILES = 2 * SGU_WIDTH // TN


def _params(n_axes):
    return pltpu.CompilerParams(dimension_semantics=("arbitrary",) * n_axes,
                                vmem_limit_bytes=VMEM_LIMIT)


def _rmsnorm_kernel(x_ref, g_ref, o_ref):
    x = x_ref[...]
    y = x * lax.rsqrt(jnp.mean(x * x, axis=-1, keepdims=True) + EPS)
    o_ref[...] = (y * g_ref[...]).astype(o_ref.dtype)


def _rmsnorm(x, gains, layer, tm=512):
    m, d = x.shape
    return pl.pallas_call(
        _rmsnorm_kernel,
        grid=(m // tm,),
        in_specs=[pl.BlockSpec((tm, d), lambda i: (i, 0)),
                  pl.BlockSpec((None, 1, d), lambda i: (layer, 0, 0))],
        out_specs=pl.BlockSpec((tm, d), lambda i: (i, 0)),
        out_shape=jax.ShapeDtypeStruct((m, d), BF16),
        compiler_params=_params(1),
        name="rmsnorm",
    )(x, gains)


def _w_spec(k, tn, layer, nt, tile0=0):
    def index(p, i, j):
        return (layer, 0, tile0 + p * nt + jnp.where(i == 0, j, nt - 1))
    return pl.BlockSpec((None, k, tn), index)


def _stage_weights(w_refs, wbf_refs):
    j = pl.program_id(2)

    @pl.when(pl.program_id(1) == 0)
    def _():
        for w_ref, wbf_ref in zip(w_refs, wbf_refs):
            wbf_ref[j] = w_ref[...].astype(BF16)


SUB_ROWS = 512


def _row_blocks(o_ref):
    tm = o_ref.shape[0]
    return [pl.ds(r, SUB_ROWS) for r in range(0, tm, SUB_ROWS)]


def _in_proj_kernel(h_ref, w_ref, o_ref, wbf_ref, *, nt):
    _stage_weights([w_ref], [wbf_ref])
    j = pl.program_id(2)
    g = pl.program_id(0) * nt + j

    def emit(act):
        for rows in _row_blocks(o_ref):
            y = jnp.dot(h_ref[rows, :], wbf_ref[j], preferred_element_type=F32)
            o_ref[rows, :] = (y if act is None else act(y)).astype(o_ref.dtype)

    pl.when(g < QKV_TILES)(lambda: emit(None))
    pl.when((g >= QKV_TILES) & (g < QKV_TILES + UV_TILES))(
        lambda: emit(functools.partial(jax.nn.gelu, approximate=True)))
    pl.when(g >= QKV_TILES + UV_TILES)(lambda: emit(jax.nn.sigmoid))


def _in_proj(h, w, layer, tm=2048, passes=3):
    m, k = h.shape
    n = w.shape[-1]
    nt = n // TN // passes
    return pl.pallas_call(
        functools.partial(_in_proj_kernel, nt=nt),
        grid=(passes, m // tm, nt),
        in_specs=[pl.BlockSpec((tm, k), lambda p, i, j: (i, 0)),
                  _w_spec(k, TN, layer, nt)],
        out_specs=pl.BlockSpec((tm, TN), lambda p, i, j: (i, p * nt + j)),
        out_shape=jax.ShapeDtypeStruct((m, n), BF16),
        scratch_shapes=[pltpu.VMEM((nt, k, TN), BF16)],
        compiler_params=_params(3),
        name="in_proj",
    )(h, w)


def _merge_kernel(a_ref, b_ref, wa_ref, wb_ref, ga_ref, gb_ref, o_ref, wabf_ref, wbbf_ref):
    _stage_weights([wa_ref, wb_ref], [wabf_ref, wbbf_ref])
    j = pl.program_id(2)
    for rows in _row_blocks(o_ref):
        ya = jnp.dot(a_ref[rows, :], wabf_ref[j], preferred_element_type=F32)
        yb = jnp.dot(b_ref[rows, :], wbbf_ref[j], preferred_element_type=F32)
        o_ref[rows, :] = (ga_ref[rows, :] * ya + gb_ref[rows, :] * yb).astype(o_ref.dtype)


def _merge(a, b, wa, wb, proj, layer, tm=2048):
    m, k = a.shape
    d = wa.shape[-1]
    nt = d // TN
    gate0 = QKV_TILES + UV_TILES
    return pl.pallas_call(
        _merge_kernel,
        grid=(1, m // tm, nt),
        in_specs=[pl.BlockSpec((tm, k), lambda p, i, j: (i, 0)),
                  pl.BlockSpec((tm, k), lambda p, i, j: (i, 0)),
                  _w_spec(k, TN, layer, nt),
                  _w_spec(k, TN, layer, nt),
                  pl.BlockSpec((tm, TN), lambda p, i, j: (i, gate0 + j)),
                  pl.BlockSpec((tm, TN), lambda p, i, j: (i, gate0 + nt + j))],
        out_specs=pl.BlockSpec((tm, TN), lambda p, i, j: (i, j)),
        out_shape=jax.ShapeDtypeStruct((m, d), BF16),
        scratch_shapes=[pltpu.VMEM((nt, k, TN), BF16), pltpu.VMEM((nt, k, TN), BF16)],
        compiler_params=_params(3),
        name="merge",
    )(a, b, wa, wb, proj, proj)


def _residual_kernel(h_ref, w_ref, x_ref, o_ref, wbf_ref):
    _stage_weights([w_ref], [wbf_ref])
    j = pl.program_id(2)
    for rows in _row_blocks(o_ref):
        y = jnp.dot(h_ref[rows, :], wbf_ref[j], preferred_element_type=F32)
        o_ref[rows, :] = x_ref[rows, :] + y


def _residual_matmul(h, w, x, layer, tm, tn, passes, name):
    m, k = h.shape
    d = w.shape[-1]
    nt = d // tn // passes
    return pl.pallas_call(
        _residual_kernel,
        grid=(passes, m // tm, nt),
        in_specs=[pl.BlockSpec((tm, k), lambda p, i, j: (i, 0)),
                  _w_spec(k, tn, layer, nt),
                  pl.BlockSpec((tm, tn), lambda p, i, j: (i, p * nt + j))],
        out_specs=pl.BlockSpec((tm, tn), lambda p, i, j: (i, p * nt + j)),
        out_shape=jax.ShapeDtypeStruct((m, d), F32),
        scratch_shapes=[pltpu.VMEM((nt, k, tn), BF16)],
        compiler_params=_params(3),
        name=name,
    )(h, w, x)


def _swiglu_kernel(h_ref, wg_ref, wu_ref, o_ref, wgbf_ref, wubf_ref):
    _stage_weights([wg_ref, wu_ref], [wgbf_ref, wubf_ref])
    j = pl.program_id(2)
    for rows in _row_blocks(o_ref):
        h = h_ref[rows, :]
        g = jnp.dot(h, wgbf_ref[j], preferred_element_type=F32)
        u = jnp.dot(h, wubf_ref[j], preferred_element_type=F32)
        o_ref[rows, :] = (g * jax.nn.sigmoid(g) * u).astype(o_ref.dtype)


def _swiglu(h, wg, wu, layer, tm=2048, tn=256, passes=2):
    m, k = h.shape
    f = wg.shape[-1]
    nt = f // tn // passes
    return pl.pallas_call(
        _swiglu_kernel,
        grid=(passes, m // tm, nt),
        in_specs=[pl.BlockSpec((tm, k), lambda p, i, j: (i, 0)),
                  _w_spec(k, tn, layer, nt),
                  _w_spec(k, tn, layer, nt)],
        out_specs=pl.BlockSpec((tm, tn), lambda p, i, j: (i, p * nt + j)),
        out_shape=jax.ShapeDtypeStruct((m, f), BF16),
        scratch_shapes=[pltpu.VMEM((nt, k, tn), BF16), pltpu.VMEM((nt, k, tn), BF16)],
        compiler_params=_params(3),
        name="swiglu",
    )(h, wg, wu)


def _attn_kernel(sinks_ref, q_ref, kvc_ref, kvp_ref, cosc_ref, sinc_ref, cosp_ref, sinp_ref,
                 gq_ref, gk_ref, o_ref):
    n = pl.program_id(1)
    lane = lax.broadcasted_iota(jnp.int32, (1, V7X_LANES), 1)
    lo_half = lane < HEAD_DIM
    even_quarter = ((lane // (HEAD_DIM // 2)) % 2) == 0
    r = lax.broadcasted_iota(jnp.int32, (V7X_LANES, V7X_LANES), 0)
    c = lax.broadcasted_iota(jnp.int32, (V7X_LANES, V7X_LANES), 1)
    head_mean = jnp.where((r // HEAD_DIM) == (c // HEAD_DIM), 1.0 / HEAD_DIM, 0.0).astype(BF16)

    def norm_rope(x, g, cos, sin):
        ms = jnp.dot((x * x).astype(BF16), head_mean, preferred_element_type=F32)
        y = x * lax.rsqrt(ms + EPS) * g
        partner = jnp.where(even_quarter,
                            pltpu.roll(y, V7X_LANES - HEAD_DIM // 2, 1),
                            pltpu.roll(y, HEAD_DIM // 2, 1))
        return y * cos + partner * sin

    def dup_half(x, use_hi):
        swapped = pltpu.roll(x, HEAD_DIM, 1)
        return jnp.where(lo_half, swapped, x) if use_hi else jnp.where(lo_half, x, swapped)

    cosc, sinc, cosp, sinp = cosc_ref[...], sinc_ref[...], cosp_ref[...], sinp_ref[...]
    gq, gk = gq_ref[...], gk_ref[...]

    k_dup, v_dup = [], []
    for col in range(KV_WIDTH // V7X_LANES):
        sl = slice(col * V7X_LANES, (col + 1) * V7X_LANES)
        vsl = slice(KV_WIDTH + col * V7X_LANES, KV_WIDTH + (col + 1) * V7X_LANES)
        k_win = jnp.concatenate([norm_rope(kvp_ref[:, sl].astype(F32), gk, cosp, sinp),
                                 norm_rope(kvc_ref[:, sl].astype(F32), gk, cosc, sinc)], axis=0)
        v_win = jnp.concatenate([kvp_ref[:, vsl], kvc_ref[:, vsl]], axis=0).astype(F32)
        for hi in (False, True):
            k_dup.append(dup_half(k_win, hi).astype(BF16))
            v_dup.append(dup_half(v_win, hi).astype(BF16))

    qi = lax.broadcasted_iota(jnp.int32, (WINDOW, 2 * WINDOW), 0)
    kj = lax.broadcasted_iota(jnp.int32, (WINDOW, 2 * WINDOW), 1)
    diff = qi + WINDOW - kj
    allowed = (diff >= 0) & (diff < WINDOW) & ((kj >= WINDOW) | (n > 0))

    scale = HEAD_DIM ** -0.5
    for pair in range(N_Q_HEADS // 2):
        sl = slice(pair * V7X_LANES, (pair + 1) * V7X_LANES)
        qp = norm_rope(q_ref[:, sl].astype(F32), gq, cosc, sinc) * scale
        kv_head = (2 * pair) // Q_PER_KV
        outs = []
        for half in range(2):
            sink = sinks_ref[2 * pair + half]
            qm = jnp.where(lo_half if half == 0 else jnp.logical_not(lo_half), qp, 0.0).astype(BF16)
            s = lax.dot_general(qm, k_dup[kv_head], (((1,), (1,)), ((), ())),
                                preferred_element_type=F32)
            s = jnp.where(allowed, s, MASK_VALUE)
            m = jnp.maximum(jnp.max(s, axis=-1, keepdims=True), sink)
            p = jnp.exp(s - m)
            denom = jnp.sum(p, axis=-1, keepdims=True) + jnp.exp(sink - m)
            o = jnp.dot(p.astype(BF16), v_dup[kv_head], preferred_element_type=F32)
            outs.append(o / denom)
        o_ref[:, sl] = jnp.where(lo_half, outs[0], outs[1]).astype(o_ref.dtype)


def _attention(proj, sinks, cos, sin_signed, gq, gk, batch, seq):
    m = proj.shape[0]
    nb = seq // WINDOW
    kv_blk = ATTN_WIDTH // (2 * KV_WIDTH)

    def cur(b, n):
        return b * nb + n

    def prev(b, n):
        return b * nb + jnp.maximum(n - 1, 0)

    table = pl.BlockSpec((WINDOW, V7X_LANES), lambda b, n: (n, 0))
    table_prev = pl.BlockSpec((WINDOW, V7X_LANES), lambda b, n: (jnp.maximum(n - 1, 0), 0))
    gain = pl.BlockSpec((1, V7X_LANES), lambda b, n: (0, 0))
    return pl.pallas_call(
        _attn_kernel,
        grid=(batch, nb),
        in_specs=[pl.BlockSpec(memory_space=pltpu.SMEM),
                  pl.BlockSpec((WINDOW, ATTN_WIDTH), lambda b, n: (cur(b, n), 0)),
                  pl.BlockSpec((WINDOW, 2 * KV_WIDTH), lambda b, n: (cur(b, n), kv_blk)),
                  pl.BlockSpec((WINDOW, 2 * KV_WIDTH), lambda b, n: (prev(b, n), kv_blk)),
                  table, table, table_prev, table_prev, gain, gain],
        out_specs=pl.BlockSpec((WINDOW, ATTN_WIDTH), lambda b, n: (cur(b, n), 0)),
        out_shape=jax.ShapeDtypeStruct((m, ATTN_WIDTH), BF16),
        compiler_params=_params(2),
        name="swa_attention",
    )(sinks, proj, proj, proj, cos, sin_signed, cos, sin_signed, gq, gk)


def _sgu_kernel(u0_ref, u1_ref, v0_ref, v1_ref, w_ref, bt_ref, g_ref, b_ref, o_ref):
    r = lax.broadcasted_iota(jnp.int32, (CHUNK, CHUNK), 0)
    c = lax.broadcasted_iota(jnp.int32, (CHUNK, CHUNK), 1)
    causal = r >= c
    lane_mean = jnp.full((SGU_GROUP_DIM, SGU_GROUP_DIM), 1.0 / SGU_GROUP_DIM, BF16)
    groups_per_tile = TN // SGU_GROUP_DIM
    n_chunks = o_ref.shape[0] // CHUNK
    for grp in range(SGU_GROUPS):
        sl = slice(grp * SGU_GROUP_DIM, (grp + 1) * SGU_GROUP_DIM)
        tsl = slice((grp % groups_per_tile) * SGU_GROUP_DIM, (grp % groups_per_tile + 1) * SGU_GROUP_DIM)
        u_ref = (u0_ref, u1_ref)[grp // groups_per_tile]
        v_ref = (v0_ref, v1_ref)[grp // groups_per_tile]
        v = v_ref[:, tsl].astype(F32)
        mu = jnp.dot(v.astype(BF16), lane_mean, preferred_element_type=F32)
        dv = v - mu
        var = jnp.dot((dv * dv).astype(BF16), lane_mean, preferred_element_type=F32)
        vn = (dv * lax.rsqrt(var + EPS) * g_ref[:, sl] + b_ref[:, sl]).astype(BF16)
        w = jnp.where(causal, w_ref[grp], 0.0).astype(BF16)
        vn_chunks = [vn[ch * CHUNK:(ch + 1) * CHUNK, :] for ch in range(n_chunks)]
        s = jnp.dot(w, jnp.concatenate(vn_chunks, axis=-1), preferred_element_type=F32)
        s = s + bt_ref[:, grp:grp + 1]
        for ch in range(n_chunks):
            u = u_ref[ch * CHUNK:(ch + 1) * CHUNK, tsl].astype(F32)
            o_ref[ch * CHUNK:(ch + 1) * CHUNK, sl] = (
                u * s[:, ch * CHUNK:(ch + 1) * CHUNK]).astype(o_ref.dtype)


def _sgu(proj, w_s, b_t, ln_g, ln_b, layer, rows=4 * CHUNK):
    m = proj.shape[0]
    u0 = QKV_TILES

    def tile(t):
        return pl.BlockSpec((rows, TN), lambda i: (i, t))

    return pl.pallas_call(
        _sgu_kernel,
        grid=(m // rows,),
        in_specs=[tile(u0), tile(u0 + 1), tile(u0 + 2), tile(u0 + 3),
                  pl.BlockSpec((None, SGU_GROUPS, CHUNK, CHUNK), lambda i: (layer, 0, 0, 0)),
                  pl.BlockSpec((None, CHUNK, SGU_GROUPS), lambda i: (layer, 0, 0)),
                  pl.BlockSpec((None, 1, SGU_WIDTH), lambda i: (layer, 0, 0)),
                  pl.BlockSpec((None, 1, SGU_WIDTH), lambda i: (layer, 0, 0))],
        out_specs=pl.BlockSpec((rows, SGU_WIDTH), lambda i: (i, 0)),
        out_shape=jax.ShapeDtypeStruct((m, SGU_WIDTH), BF16),
        compiler_params=_params(1),
        name="sgu",
    )(proj, proj, proj, proj, w_s, b_t, ln_g, ln_b)


def _rope_tables(seq):
    pos = jnp.arange(seq, dtype=F32)
    inv_freq = jnp.power(ROPE_THETA, -jnp.arange(0, HEAD_DIM, 2, dtype=F32) / HEAD_DIM)
    ang = pos[:, None] * inv_freq[None, :]
    cos, sin = jnp.cos(ang), jnp.sin(ang)
    reps = V7X_LANES // HEAD_DIM
    return jnp.tile(cos, (1, 2 * reps)), jnp.tile(jnp.concatenate([-sin, sin], axis=-1), (1, reps))


def kernel(x, mix_norm, w_in, q_norm, k_norm, sinks, sgu_ln_g, sgu_ln_b, w_spatial, b_spatial,
           w_attn_branch, w_sgu_branch, w_out, ffn_norm, w_gate, w_up, w_down):
    batch, seq, d = x.shape
    depth = w_in.shape[0]
    m = batch * seq
    assert w_in.shape[-1] == (QKV_TILES + UV_TILES) * TN + 2 * d
    cos, sin_signed = _rope_tables(seq)
    reps = V7X_LANES // HEAD_DIM
    gq = jnp.tile(q_norm, (1, reps))
    gk = jnp.tile(k_norm, (1, reps))
    b_t = jnp.swapaxes(b_spatial, 1, 2)
    mix_norm, ffn_norm, sgu_ln_g, sgu_ln_b = (
        p.reshape(depth, 1, -1) for p in (mix_norm, ffn_norm, sgu_ln_g, sgu_ln_b))

    xf = x.reshape(m, d)
    for l in range(depth):
        h = _rmsnorm(xf, mix_norm, l)
        proj = _in_proj(h, w_in, l)
        a = _attention(proj, sinks[l], cos, sin_signed, gq[l:l + 1], gk[l:l + 1], batch, seq)
        b = _sgu(proj, w_spatial, b_t, sgu_ln_g, sgu_ln_b, l)
        merged = _merge(a, b, w_attn_branch, w_sgu_branch, proj, l)
        xf = _residual_matmul(merged, w_out, xf, l, tm=2048, tn=TN, passes=1, name="out_proj")
        h2 = _rmsnorm(xf, ffn_norm, l)
        act = _swiglu(h2, w_gate, w_up, l)
        xf = _residual_matmul(act, w_down, xf, l, tm=512, tn=TN, passes=2, name="down_proj")
    return xf.reshape(batch, seq, d)
```

```python
import functools

import jax
import jax.numpy as jnp
from jax import lax
from jax.experimental import pallas as pl
from jax.experimental.pallas import tpu as pltpu

F32 = jnp.float32
BF16 = jnp.bfloat16

HEAD_DIM = 64
N_Q_HEADS = 16
N_KV_HEADS = 4
Q_PER_KV = N_Q_HEADS // N_KV_HEADS
ATTN_WIDTH = N_Q_HEADS * HEAD_DIM
KV_WIDTH = N_KV_HEADS * HEAD_DIM
WINDOW = 128
ROPE_THETA = 10000.0
SGU_GROUPS = 8
SGU_GROUP_DIM = 128
SGU_WIDTH = SGU_GROUPS * SGU_GROUP_DIM
CHUNK = 128
EPS = 1e-6
MASK_VALUE = -1e30
LOG2_E = 1.4426950408889634

V7X_LANES = 128
V7X_VMEM_BYTES = 64 * 1024 * 1024
VMEM_LIMIT = V7X_VMEM_BYTES - 8 * 1024 * 1024

TN = 512
QKV_TILES = (ATTN_WIDTH + 2 * KV_WIDTH) // TN
UV_TILES = 2 * SGU_WIDTH // TN


def _params(n_axes):
    return pltpu.CompilerParams(dimension_semantics=("arbitrary",) * n_axes,
                                vmem_limit_bytes=VMEM_LIMIT)


def _rmsnorm_kernel(x_ref, g_ref, o_ref):
    x = x_ref[...]
    y = x * lax.rsqrt(jnp.mean(x * x, axis=-1, keepdims=True) + EPS)
    o_ref[...] = (y * g_ref[...]).astype(o_ref.dtype)


def _rmsnorm(x, gains, layer, tm=512):
    m, d = x.shape
    return pl.pallas_call(
        _rmsnorm_kernel,
        grid=(m // tm,),
        in_specs=[pl.BlockSpec((tm, d), lambda i: (i, 0)),
                  pl.BlockSpec((None, 1, d), lambda i: (layer, 0, 0))],
        out_specs=pl.BlockSpec((tm, d), lambda i: (i, 0)),
        out_shape=jax.ShapeDtypeStruct((m, d), BF16),
        compiler_params=_params(1),
        name="rmsnorm",
    )(x, gains)


def _w_spec(k, tn, layer, nt, tile0=0):
    def index(p, i, j):
        return (layer, 0, tile0 + p * nt + jnp.where(i == 0, j, nt - 1))
    return pl.BlockSpec((None, k, tn), index)


def _stage_weights(w_refs, wbf_refs):
    j = pl.program_id(2)

    @pl.when(pl.program_id(1) == 0)
    def _():
        for w_ref, wbf_ref in zip(w_refs, wbf_refs):
            wbf_ref[j] = w_ref[...].astype(BF16)


SUB_ROWS = 512


def _row_blocks(o_ref):
    tm = o_ref.shape[0]
    return [pl.ds(r, SUB_ROWS) for r in range(0, tm, SUB_ROWS)]


def _in_proj_kernel(h_ref, w_ref, o_ref, wbf_ref, *, nt):
    _stage_weights([w_ref], [wbf_ref])
    j = pl.program_id(2)
    g = pl.program_id(0) * nt + j

    def emit(act):
        for rows in _row_blocks(o_ref):
            y = jnp.dot(h_ref[rows, :], wbf_ref[j], preferred_element_type=F32)
            o_ref[rows, :] = (y if act is None else act(y)).astype(o_ref.dtype)

    pl.when(g < QKV_TILES)(lambda: emit(None))
    pl.when((g >= QKV_TILES) & (g < QKV_TILES + UV_TILES))(
        lambda: emit(functools.partial(jax.nn.gelu, approximate=True)))
    pl.when(g >= QKV_TILES + UV_TILES)(lambda: emit(jax.nn.sigmoid))


def _in_proj(h, w, layer, tm=2048, passes=3):
    m, k = h.shape
    n = w.shape[-1]
    nt = n // TN // passes
    return pl.pallas_call(
        functools.partial(_in_proj_kernel, nt=nt),
        grid=(passes, m // tm, nt),
        in_specs=[pl.BlockSpec((tm, k), lambda p, i, j: (i, 0)),
                  _w_spec(k, TN, layer, nt)],
        out_specs=pl.BlockSpec((tm, TN), lambda p, i, j: (i, p * nt + j)),
        out_shape=jax.ShapeDtypeStruct((m, n), BF16),
        scratch_shapes=[pltpu.VMEM((nt, k, TN), BF16)],
        compiler_params=_params(3),
        name="in_proj",
    )(h, w)


def _merge_kernel(a_ref, b_ref, wa_ref, wb_ref, ga_ref, gb_ref, o_ref, wabf_ref, wbbf_ref):
    _stage_weights([wa_ref, wb_ref], [wabf_ref, wbbf_ref])
    j = pl.program_id(2)
    for rows in _row_blocks(o_ref):
        ya = jnp.dot(a_ref[rows, :], wabf_ref[j], preferred_element_type=F32)
        yb = jnp.dot(b_ref[rows, :], wbbf_ref[j], preferred_element_type=F32)
        o_ref[rows, :] = (ga_ref[rows, :] * ya + gb_ref[rows, :] * yb).astype(o_ref.dtype)


def _merge(a, b, wa, wb, proj, layer, tm=2048):
    m, k = a.shape
    d = wa.shape[-1]
    nt = d // TN
    gate0 = QKV_TILES + UV_TILES
    return pl.pallas_call(
        _merge_kernel,
        grid=(1, m // tm, nt),
        in_specs=[pl.BlockSpec((tm, k), lambda p, i, j: (i, 0)),
                  pl.BlockSpec((tm, k), lambda p, i, j: (i, 0)),
                  _w_spec(k, TN, layer, nt),
                  _w_spec(k, TN, layer, nt),
                  pl.BlockSpec((tm, TN), lambda p, i, j: (i, gate0 + j)),
                  pl.BlockSpec((tm, TN), lambda p, i, j: (i, gate0 + nt + j))],
        out_specs=pl.BlockSpec((tm, TN), lambda p, i, j: (i, j)),
        out_shape=jax.ShapeDtypeStruct((m, d), BF16),
        scratch_shapes=[pltpu.VMEM((nt, k, TN), BF16), pltpu.VMEM((nt, k, TN), BF16)],
        compiler_params=_params(3),
        name="merge",
    )(a, b, wa, wb, proj, proj)


def _residual_kernel(h_ref, w_ref, x_ref, o_ref, wbf_ref):
    _stage_weights([w_ref], [wbf_ref])
    j = pl.program_id(2)
    for rows in _row_blocks(o_ref):
        y = jnp.dot(h_ref[rows, :], wbf_ref[j], preferred_element_type=F32)
        o_ref[rows, :] = x_ref[rows, :] + y


def _residual_matmul(h, w, x, layer, tm, tn, passes, name):
    m, k = h.shape
    d = w.shape[-1]
    nt = d // tn // passes
    return pl.pallas_call(
        _residual_kernel,
        grid=(passes, m // tm, nt),
        in_specs=[pl.BlockSpec((tm, k), lambda p, i, j: (i, 0)),
                  _w_spec(k, tn, layer, nt),
                  pl.BlockSpec((tm, tn), lambda p, i, j: (i, p * nt + j))],
        out_specs=pl.BlockSpec((tm, tn), lambda p, i, j: (i, p * nt + j)),
        out_shape=jax.ShapeDtypeStruct((m, d), F32),
        scratch_shapes=[pltpu.VMEM((nt, k, tn), BF16)],
        compiler_params=_params(3),
        name=name,
    )(h, w, x)


def _swiglu_kernel(h_ref, wg_ref, wu_ref, o_ref, wgbf_ref, wubf_ref):
    _stage_weights([wg_ref, wu_ref], [wgbf_ref, wubf_ref])
    j = pl.program_id(2)
    for rows in _row_blocks(o_ref):
        h = h_ref[rows, :]
        g = jnp.dot(h, wgbf_ref[j], preferred_element_type=F32)
        u = jnp.dot(h, wubf_ref[j], preferred_element_type=F32)
        o_ref[rows, :] = (g * jax.nn.sigmoid(g) * u).astype(o_ref.dtype)


def _swiglu(h, wg, wu, layer, tm=2048, tn=256, passes=2):
    m, k = h.shape
    f = wg.shape[-1]
    nt = f // tn // passes
    return pl.pallas_call(
        _swiglu_kernel,
        grid=(passes, m // tm, nt),
        in_specs=[pl.BlockSpec((tm, k), lambda p, i, j: (i, 0)),
                  _w_spec(k, tn, layer, nt),
                  _w_spec(k, tn, layer, nt)],
        out_specs=pl.BlockSpec((tm, tn), lambda p, i, j: (i, p * nt + j)),
        out_shape=jax.ShapeDtypeStruct((m, f), BF16),
        scratch_shapes=[pltpu.VMEM((nt, k, tn), BF16), pltpu.VMEM((nt, k, tn), BF16)],
        compiler_params=_params(3),
        name="swiglu",
    )(h, wg, wu)


def _attn_kernel(sinks_ref, q_ref, kv_ref, cos_ref, sin_ref, cos_t_ref, sin_t_ref, gq_t_ref, gk_ref,
                 o_ref, k_prev_ref, vt_prev_ref):
    n = pl.program_id(1)
    half_dim = HEAD_DIM // 2

    @pl.when(n == 0)
    def _():
        k_prev_ref[...] = jnp.zeros_like(k_prev_ref)
        vt_prev_ref[...] = jnp.zeros_like(vt_prev_ref)

    lane = lax.broadcasted_iota(jnp.int32, (1, V7X_LANES), 1)
    even_quarter = ((lane // half_dim) % 2) == 0
    r = lax.broadcasted_iota(jnp.int32, (V7X_LANES, V7X_LANES), 0)
    c = lax.broadcasted_iota(jnp.int32, (V7X_LANES, V7X_LANES), 1)
    head_mean = jnp.where((r // HEAD_DIM) == (c // HEAD_DIM), 1.0 / HEAD_DIM, 0.0).astype(BF16)
    cos, sin = cos_ref[...], sin_ref[...]
    k_cols = []
    for col in range(KV_WIDTH // V7X_LANES):
        x = kv_ref[:, col * V7X_LANES:(col + 1) * V7X_LANES].astype(F32)
        ms = jnp.dot((x * x).astype(BF16), head_mean, preferred_element_type=F32)
        y = x * lax.rsqrt(ms + EPS) * gk_ref[...]
        partner = jnp.where(even_quarter,
                            pltpu.roll(y, V7X_LANES - half_dim, 1),
                            pltpu.roll(y, half_dim, 1))
        k_cols.append((y * cos + partner * sin).astype(BF16))
    k_win = [jnp.concatenate([k_prev_ref[col], k_cols[col]], axis=0)
             for col in range(KV_WIDTH // V7X_LANES)]
    vt_cur = kv_ref[:, KV_WIDTH:].astype(F32).T.astype(BF16)
    vt_win = jnp.concatenate([vt_prev_ref[...], vt_cur], axis=1)

    key = lax.broadcasted_iota(jnp.int32, (2 * WINDOW, WINDOW), 0)
    qry = lax.broadcasted_iota(jnp.int32, (2 * WINDOW, WINDOW), 1)
    diff = qry + WINDOW - key
    allowed = (diff >= 0) & (diff < WINDOW) & ((key >= WINDOW) | (n > 0))

    cos_t, sin_t, gq_t = cos_t_ref[...], sin_t_ref[...], gq_t_ref[...]
    zeros = jnp.zeros((HEAD_DIM, WINDOW), BF16)

    qt_pads = []
    for pair in range(N_Q_HEADS // 2):
        sl = slice(pair * V7X_LANES, (pair + 1) * V7X_LANES)
        qt_pair = q_ref[:, sl].astype(F32).T
        for half in range(2):
            kv_head = (2 * pair + half) // Q_PER_KV
            x = qt_pair[half * HEAD_DIM:(half + 1) * HEAD_DIM, :]
            y = x * lax.rsqrt(jnp.mean(x * x, axis=0, keepdims=True) + EPS) * gq_t
            y1, y2 = y[:half_dim], y[half_dim:]
            qt = jnp.concatenate([y1 * cos_t - y2 * sin_t, y2 * cos_t + y1 * sin_t],
                                 axis=0).astype(BF16)
            qt_pads.append(jnp.concatenate([zeros, qt] if kv_head % 2 else [qt, zeros], axis=0))

    heads_per_col = N_Q_HEADS // (KV_WIDTH // V7X_LANES)
    sink_row = [jnp.full((1, WINDOW), sinks_ref[h] * LOG2_E, F32) for h in range(N_Q_HEADS)]
    probs, inv_denoms = [], []
    for col in range(KV_WIDTH // V7X_LANES):
        heads = range(col * heads_per_col, (col + 1) * heads_per_col)
        s = jnp.dot(k_win[col], jnp.concatenate([qt_pads[h] for h in heads], axis=1),
                    preferred_element_type=F32)
        s = jnp.where(jnp.concatenate([allowed] * heads_per_col, axis=1), s, MASK_VALUE)
        sink = jnp.concatenate([sink_row[h] for h in heads], axis=1)
        m = jnp.maximum(jnp.max(s, axis=0, keepdims=True), sink)
        p = jnp.exp2(s - m)
        inv_denoms.append(1.0 / (jnp.sum(p, axis=0, keepdims=True) + jnp.exp2(sink - m)))
        probs.append(p.astype(BF16))

    group_w = Q_PER_KV * WINDOW
    for kv_head in range(N_KV_HEADS):
        col, part = divmod(kv_head, N_KV_HEADS // (KV_WIDTH // V7X_LANES))
        lanes = slice(part * group_w, (part + 1) * group_w)
        vt = vt_win[kv_head * HEAD_DIM:(kv_head + 1) * HEAD_DIM, :]
        o = jnp.dot(vt, probs[col][:, lanes], preferred_element_type=F32)
        o = o * inv_denoms[col][:, lanes]
        for pair in range(Q_PER_KV // 2):
            o_pair = jnp.concatenate([o[:, (2 * pair) * WINDOW:(2 * pair + 1) * WINDOW],
                                      o[:, (2 * pair + 1) * WINDOW:(2 * pair + 2) * WINDOW]], axis=0)
            out_lane = (kv_head * Q_PER_KV + 2 * pair) * HEAD_DIM
            o_ref[:, out_lane:out_lane + V7X_LANES] = o_pair.T.astype(o_ref.dtype)

    for col in range(KV_WIDTH // V7X_LANES):
        k_prev_ref[col] = k_cols[col]
    vt_prev_ref[...] = vt_cur


def _attention(proj, sinks, tables, gq_t, gk, layer, batch, seq):
    m = proj.shape[0]
    nb = seq // WINDOW
    kv_blk = ATTN_WIDTH // (2 * KV_WIDTH)
    cos, sin_signed, cos_t, sin_t = tables
    half_dim = HEAD_DIM // 2
    row_table = pl.BlockSpec((WINDOW, V7X_LANES), lambda b, n: (n, 0))
    col_table = pl.BlockSpec((half_dim, WINDOW), lambda b, n: (0, n))
    return pl.pallas_call(
        _attn_kernel,
        grid=(batch, nb),
        in_specs=[pl.BlockSpec(memory_space=pltpu.SMEM),
                  pl.BlockSpec((WINDOW, ATTN_WIDTH), lambda b, n: (b * nb + n, 0)),
                  pl.BlockSpec((WINDOW, 2 * KV_WIDTH), lambda b, n: (b * nb + n, kv_blk)),
                  row_table, row_table, col_table, col_table,
                  pl.BlockSpec((None, HEAD_DIM, WINDOW), lambda b, n: (layer, 0, 0)),
                  pl.BlockSpec((None, 1, V7X_LANES), lambda b, n: (layer, 0, 0))],
        out_specs=pl.BlockSpec((WINDOW, ATTN_WIDTH), lambda b, n: (b * nb + n, 0)),
        out_shape=jax.ShapeDtypeStruct((m, ATTN_WIDTH), BF16),
        scratch_shapes=[pltpu.VMEM((KV_WIDTH // V7X_LANES, WINDOW, V7X_LANES), BF16),
                        pltpu.VMEM((KV_WIDTH, WINDOW), BF16)],
        compiler_params=_params(2),
        name="swa_attention",
    )(sinks, proj, proj, cos, sin_signed, cos_t, sin_t, gq_t, gk)


def _sgu_kernel(u0_ref, u1_ref, v0_ref, v1_ref, w_ref, bt_ref, g_ref, b_ref, o_ref):
    r = lax.broadcasted_iota(jnp.int32, (CHUNK, CHUNK), 0)
    c = lax.broadcasted_iota(jnp.int32, (CHUNK, CHUNK), 1)
    causal = r >= c
    lane_mean = jnp.full((SGU_GROUP_DIM, SGU_GROUP_DIM), 1.0 / SGU_GROUP_DIM, BF16)
    groups_per_tile = TN // SGU_GROUP_DIM
    n_chunks = o_ref.shape[0] // CHUNK
    for grp in range(SGU_GROUPS):
        sl = slice(grp * SGU_GROUP_DIM, (grp + 1) * SGU_GROUP_DIM)
        tsl = slice((grp % groups_per_tile) * SGU_GROUP_DIM, (grp % groups_per_tile + 1) * SGU_GROUP_DIM)
        u_ref = (u0_ref, u1_ref)[grp // groups_per_tile]
        v_ref = (v0_ref, v1_ref)[grp // groups_per_tile]
        v = v_ref[:, tsl].astype(F32)
        mu = jnp.dot(v.astype(BF16), lane_mean, preferred_element_type=F32)
        dv = v - mu
        var = jnp.dot((dv * dv).astype(BF16), lane_mean, preferred_element_type=F32)
        vn = (dv * lax.rsqrt(var + EPS) * g_ref[:, sl] + b_ref[:, sl]).astype(BF16)
        w = jnp.where(causal, w_ref[grp], 0.0).astype(BF16)
        vn_chunks = [vn[ch * CHUNK:(ch + 1) * CHUNK, :] for ch in range(n_chunks)]
        s = jnp.dot(w, jnp.concatenate(vn_chunks, axis=-1), preferred_element_type=F32)
        s = s + bt_ref[:, grp:grp + 1]
        for ch in range(n_chunks):
            u = u_ref[ch * CHUNK:(ch + 1) * CHUNK, tsl].astype(F32)
            o_ref[ch * CHUNK:(ch + 1) * CHUNK, sl] = (
                u * s[:, ch * CHUNK:(ch + 1) * CHUNK]).astype(o_ref.dtype)


def _sgu(proj, w_s, b_t, ln_g, ln_b, layer, rows=4 * CHUNK):
    m = proj.shape[0]
    u0 = QKV_TILES

    def tile(t):
        return pl.BlockSpec((rows, TN), lambda i: (i, t))

    return pl.pallas_call(
        _sgu_kernel,
        grid=(m // rows,),
        in_specs=[tile(u0), tile(u0 + 1), tile(u0 + 2), tile(u0 + 3),
                  pl.BlockSpec((None, SGU_GROUPS, CHUNK, CHUNK), lambda i: (layer, 0, 0, 0)),
                  pl.BlockSpec((None, CHUNK, SGU_GROUPS), lambda i: (layer, 0, 0)),
                  pl.BlockSpec((None, 1, SGU_WIDTH), lambda i: (layer, 0, 0)),
                  pl.BlockSpec((None, 1, SGU_WIDTH), lambda i: (layer, 0, 0))],
        out_specs=pl.BlockSpec((rows, SGU_WIDTH), lambda i: (i, 0)),
        out_shape=jax.ShapeDtypeStruct((m, SGU_WIDTH), BF16),
        compiler_params=_params(1),
        name="sgu",
    )(proj, proj, proj, proj, w_s, b_t, ln_g, ln_b)


def _rope_tables(seq):
    pos = jnp.arange(seq, dtype=F32)
    inv_freq = jnp.power(ROPE_THETA, -jnp.arange(0, HEAD_DIM, 2, dtype=F32) / HEAD_DIM)
    ang = pos[:, None] * inv_freq[None, :]
    cos, sin = jnp.cos(ang), jnp.sin(ang)
    reps = V7X_LANES // HEAD_DIM
    return (jnp.tile(cos, (1, 2 * reps)), jnp.tile(jnp.concatenate([-sin, sin], axis=-1), (1, reps)),
            cos.T, sin.T)


def kernel(x, mix_norm, w_in, q_norm, k_norm, sinks, sgu_ln_g, sgu_ln_b, w_spatial, b_spatial,
           w_attn_branch, w_sgu_branch, w_out, ffn_norm, w_gate, w_up, w_down):
    batch, seq, d = x.shape
    depth = w_in.shape[0]
    m = batch * seq
    assert w_in.shape[-1] == (QKV_TILES + UV_TILES) * TN + 2 * d
    tables = _rope_tables(seq)
    gq_t = jnp.broadcast_to((q_norm * (HEAD_DIM ** -0.5 * LOG2_E))[:, :, None],
                            (depth, HEAD_DIM, WINDOW))
    gk = jnp.tile(k_norm, (1, V7X_LANES // HEAD_DIM)).reshape(depth, 1, V7X_LANES)
    b_t = jnp.swapaxes(b_spatial, 1, 2)
    mix_norm, ffn_norm, sgu_ln_g, sgu_ln_b = (
        p.reshape(depth, 1, -1) for p in (mix_norm, ffn_norm, sgu_ln_g, sgu_ln_b))

    xf = x.reshape(m, d)
    for l in range(depth):
        h = _rmsnorm(xf, mix_norm, l)
        proj = _in_proj(h, w_in, l)
        a = _attention(proj, sinks[l], tables, gq_t, gk, l, batch, seq)
        b = _sgu(proj, w_spatial, b_t, sgu_ln_g, sgu_ln_b, l)
        merged = _merge(a, b, w_attn_branch, w_sgu_branch, proj, l)
        xf = _residual_matmul(merged, w_out, xf, l, tm=2048, tn=TN, passes=1, name="out_proj")
        h2 = _rmsnorm(xf, ffn_norm, l)
        act = _swiglu(h2, w_gate, w_up, l)
        xf = _residual_matmul(act, w_down, xf, l, tm=512, tn=TN, passes=2, name="down_proj")
    return xf.reshape(batch, seq, d)
```

```python
import functools

import jax
import jax.numpy as jnp
from jax import lax
from jax.experimental import pallas as pl
from jax.experimental.pallas import tpu as pltpu

F32 = jnp.float32
BF16 = jnp.bfloat16

HEAD_DIM = 64
N_Q_HEADS = 16
N_KV_HEADS = 4
Q_PER_KV = N_Q_HEADS // N_KV_HEADS
ATTN_WIDTH = N_Q_HEADS * HEAD_DIM
KV_WIDTH = N_KV_HEADS * HEAD_DIM
WINDOW = 128
ROPE_THETA = 10000.0
SGU_GROUPS = 8
SGU_GROUP_DIM = 128
SGU_WIDTH = SGU_GROUPS * SGU_GROUP_DIM
CHUNK = 128
EPS = 1e-6
MASK_VALUE = -1e30
LOG2_E = 1.4426950408889634

V7X_LANES = 128
V7X_VMEM_BYTES = 64 * 1024 * 1024
VMEM_LIMIT = V7X_VMEM_BYTES - 8 * 1024 * 1024

TN = 512
QKV_TILES = (ATTN_WIDTH + 2 * KV_WIDTH) // TN
UV_TILES = 2 * SGU_WIDTH // TN


def _params(n_axes):
    return pltpu.CompilerParams(dimension_semantics=("arbitrary",) * n_axes,
                                vmem_limit_bytes=VMEM_LIMIT)


def _prep_kernel(x_ref, xb_ref, ssq_ref):
    x = x_ref[...]
    xb_ref[...] = x.astype(xb_ref.dtype)
    ssq_ref[...] = jnp.sum(x * x, axis=-1, keepdims=True)


def _prep(x, tm=512):
    m, d = x.shape
    return pl.pallas_call(
        _prep_kernel,
        grid=(m // tm,),
        in_specs=[pl.BlockSpec((tm, d), lambda i: (i, 0))],
        out_specs=[pl.BlockSpec((tm, d), lambda i: (i, 0)),
                   pl.BlockSpec((None, tm, 1), lambda i: (0, i, 0))],
        out_shape=[jax.ShapeDtypeStruct((m, d), BF16), jax.ShapeDtypeStruct((1, m, 1), F32)],
        compiler_params=_params(1),
        name="prep",
    )(x)


def _ssq_spec(ssq, tm):
    return pl.BlockSpec((ssq.shape[0], tm, 1), lambda p, i, j: (0, i, 0))


def _row_scale(ssq_ref, r_ref, width):
    @pl.when(pl.program_id(2) == 0)
    def _():
        ssq = ssq_ref[0]
        for part in range(1, ssq_ref.shape[0]):
            ssq = ssq + ssq_ref[part]
        r_ref[...] = jnp.broadcast_to(lax.rsqrt(ssq * (1.0 / width) + EPS), r_ref.shape)


def _scaled(y, r_ref, rows):
    r = r_ref[rows, :]
    return y * jnp.concatenate([r] * (y.shape[1] // V7X_LANES), axis=1)


def _w_spec(k, tn, layer, nt, tile0=0):
    def index(p, i, j):
        return (layer, 0, tile0 + p * nt + jnp.where(i == 0, j, nt - 1))
    return pl.BlockSpec((None, k, tn), index)


def _stage_weights(w_refs, wbf_refs, gain_ref=None):
    j = pl.program_id(2)

    @pl.when(pl.program_id(1) == 0)
    def _():
        for w_ref, wbf_ref in zip(w_refs, wbf_refs):
            w = w_ref[...]
            wbf_ref[j] = (w if gain_ref is None else w * gain_ref[...]).astype(BF16)


SUB_ROWS = 512


def _row_blocks(o_ref):
    tm = o_ref.shape[0]
    return [pl.ds(r, SUB_ROWS) for r in range(0, tm, SUB_ROWS)]


def _in_proj_kernel(h_ref, ssq_ref, gain_ref, w_ref, o_ref, wbf_ref, r_ref, *, nt):
    _stage_weights([w_ref], [wbf_ref], gain_ref)
    _row_scale(ssq_ref, r_ref, h_ref.shape[1])
    j = pl.program_id(2)
    g = pl.program_id(0) * nt + j

    def emit(act):
        for rows in _row_blocks(o_ref):
            y = jnp.dot(h_ref[rows, :], wbf_ref[j], preferred_element_type=F32)
            y = _scaled(y, r_ref, rows)
            o_ref[rows, :] = (y if act is None else act(y)).astype(o_ref.dtype)

    pl.when(g < QKV_TILES)(lambda: emit(None))
    pl.when((g >= QKV_TILES) & (g < QKV_TILES + UV_TILES))(
        lambda: emit(functools.partial(jax.nn.gelu, approximate=True)))
    pl.when(g >= QKV_TILES + UV_TILES)(lambda: emit(jax.nn.sigmoid))


def _gain_spec(k, layer):
    return pl.BlockSpec((None, k, 1), lambda p, i, j: (layer, 0, 0))


def _in_proj(xb, ssq, gain, w, layer, tm=2048, passes=3):
    m, k = xb.shape
    n = w.shape[-1]
    nt = n // TN // passes
    return pl.pallas_call(
        functools.partial(_in_proj_kernel, nt=nt),
        grid=(passes, m // tm, nt),
        in_specs=[pl.BlockSpec((tm, k), lambda p, i, j: (i, 0)),
                  _ssq_spec(ssq, tm),
                  _gain_spec(k, layer),
                  _w_spec(k, TN, layer, nt)],
        out_specs=pl.BlockSpec((tm, TN), lambda p, i, j: (i, p * nt + j)),
        out_shape=jax.ShapeDtypeStruct((m, n), BF16),
        scratch_shapes=[pltpu.VMEM((nt, k, TN), BF16), pltpu.VMEM((tm, V7X_LANES), F32)],
        compiler_params=_params(3),
        name="in_proj",
    )(xb, ssq, gain, w)


def _merge_kernel(a_ref, b_ref, wa_ref, wb_ref, ga_ref, gb_ref, o_ref, wabf_ref, wbbf_ref):
    _stage_weights([wa_ref, wb_ref], [wabf_ref, wbbf_ref])
    j = pl.program_id(2)
    for rows in _row_blocks(o_ref):
        ya = jnp.dot(a_ref[rows, :], wabf_ref[j], preferred_element_type=F32)
        yb = jnp.dot(b_ref[rows, :], wbbf_ref[j], preferred_element_type=F32)
        o_ref[rows, :] = (ga_ref[rows, :] * ya + gb_ref[rows, :] * yb).astype(o_ref.dtype)


def _merge(a, b, wa, wb, proj, layer, tm=2048):
    m, k = a.shape
    d = wa.shape[-1]
    nt = d // TN
    gate0 = QKV_TILES + UV_TILES
    return pl.pallas_call(
        _merge_kernel,
        grid=(1, m // tm, nt),
        in_specs=[pl.BlockSpec((tm, k), lambda p, i, j: (i, 0)),
                  pl.BlockSpec((tm, k), lambda p, i, j: (i, 0)),
                  _w_spec(k, TN, layer, nt),
                  _w_spec(k, TN, layer, nt),
                  pl.BlockSpec((tm, TN), lambda p, i, j: (i, gate0 + j)),
                  pl.BlockSpec((tm, TN), lambda p, i, j: (i, gate0 + nt + j))],
        out_specs=pl.BlockSpec((tm, TN), lambda p, i, j: (i, j)),
        out_shape=jax.ShapeDtypeStruct((m, d), BF16),
        scratch_shapes=[pltpu.VMEM((nt, k, TN), BF16), pltpu.VMEM((nt, k, TN), BF16)],
        compiler_params=_params(3),
        name="merge",
    )(a, b, wa, wb, proj, proj)


def _residual_kernel(h_ref, w_ref, x_ref, o_ref, *rest):
    *norm_refs, wbf_ref = rest
    _stage_weights([w_ref], [wbf_ref])
    j = pl.program_id(2)
    if norm_refs:
        xb_ref, ssq_ref = norm_refs

        @pl.when(j == 0)
        def _():
            ssq_ref[...] = jnp.zeros_like(ssq_ref)

    for rows in _row_blocks(o_ref):
        y = x_ref[rows, :] + jnp.dot(h_ref[rows, :], wbf_ref[j], preferred_element_type=F32)
        o_ref[rows, :] = y
        if norm_refs:
            xb_ref[rows, :] = y.astype(xb_ref.dtype)
            ssq_ref[rows, :] += jnp.sum(y * y, axis=-1, keepdims=True)


def _residual_matmul(h, w, x, layer, tm, tn, passes, name, emit_norm):
    m, k = h.shape
    d = w.shape[-1]
    nt = d // tn // passes
    tile = pl.BlockSpec((tm, tn), lambda p, i, j: (i, p * nt + j))
    out_specs, out_shape = [tile], [jax.ShapeDtypeStruct((m, d), F32)]
    if emit_norm:
        out_specs += [tile, pl.BlockSpec((None, tm, 1), lambda p, i, j: (p, i, 0))]
        out_shape += [jax.ShapeDtypeStruct((m, d), BF16), jax.ShapeDtypeStruct((passes, m, 1), F32)]
    return pl.pallas_call(
        _residual_kernel,
        grid=(passes, m // tm, nt),
        in_specs=[pl.BlockSpec((tm, k), lambda p, i, j: (i, 0)),
                  _w_spec(k, tn, layer, nt),
                  tile],
        out_specs=out_specs,
        out_shape=out_shape,
        scratch_shapes=[pltpu.VMEM((nt, k, tn), BF16)],
        compiler_params=_params(3),
        name=name,
    )(h, w, x)


def _swiglu_kernel(h_ref, ssq_ref, gain_ref, wg_ref, wu_ref, o_ref, wgbf_ref, wubf_ref, r_ref):
    _stage_weights([wg_ref, wu_ref], [wgbf_ref, wubf_ref], gain_ref)
    _row_scale(ssq_ref, r_ref, h_ref.shape[1])
    j = pl.program_id(2)
    for rows in _row_blocks(o_ref):
        h = h_ref[rows, :]
        g = _scaled(jnp.dot(h, wgbf_ref[j], preferred_element_type=F32), r_ref, rows)
        u = _scaled(jnp.dot(h, wubf_ref[j], preferred_element_type=F32), r_ref, rows)
        o_ref[rows, :] = (g * jax.nn.sigmoid(g) * u).astype(o_ref.dtype)


def _swiglu(xb, ssq, gain, wg, wu, layer, tm=2048, tn=256, passes=2):
    m, k = xb.shape
    f = wg.shape[-1]
    nt = f // tn // passes
    return pl.pallas_call(
        _swiglu_kernel,
        grid=(passes, m // tm, nt),
        in_specs=[pl.BlockSpec((tm, k), lambda p, i, j: (i, 0)),
                  _ssq_spec(ssq, tm),
                  _gain_spec(k, layer),
                  _w_spec(k, tn, layer, nt),
                  _w_spec(k, tn, layer, nt)],
        out_specs=pl.BlockSpec((tm, tn), lambda p, i, j: (i, p * nt + j)),
        out_shape=jax.ShapeDtypeStruct((m, f), BF16),
        scratch_shapes=[pltpu.VMEM((nt, k, tn), BF16), pltpu.VMEM((nt, k, tn), BF16),
                        pltpu.VMEM((tm, V7X_LANES), F32)],
        compiler_params=_params(3),
        name="swiglu",
    )(xb, ssq, gain, wg, wu)


def _attn_kernel(sinks_ref, q_ref, kv_ref, cos_ref, sin_ref, cos_t_ref, sin_t_ref, gq_t_ref, gk_ref,
                 o_ref, k_prev_ref, vt_prev_ref):
    n = pl.program_id(1)
    half_dim = HEAD_DIM // 2

    @pl.when(n == 0)
    def _():
        k_prev_ref[...] = jnp.zeros_like(k_prev_ref)
        vt_prev_ref[...] = jnp.zeros_like(vt_prev_ref)

    lane = lax.broadcasted_iota(jnp.int32, (1, V7X_LANES), 1)
    even_quarter = ((lane // half_dim) % 2) == 0
    r = lax.broadcasted_iota(jnp.int32, (V7X_LANES, V7X_LANES), 0)
    c = lax.broadcasted_iota(jnp.int32, (V7X_LANES, V7X_LANES), 1)
    head_mean = jnp.where((r // HEAD_DIM) == (c // HEAD_DIM), 1.0 / HEAD_DIM, 0.0).astype(BF16)
    cos, sin = cos_ref[...], sin_ref[...]
    k_cols = []
    for col in range(KV_WIDTH // V7X_LANES):
        x = kv_ref[:, col * V7X_LANES:(col + 1) * V7X_LANES].astype(F32)
        ms = jnp.dot((x * x).astype(BF16), head_mean, preferred_element_type=F32)
        y = x * lax.rsqrt(ms + EPS) * gk_ref[...]
        partner = jnp.where(even_quarter,
                            pltpu.roll(y, V7X_LANES - half_dim, 1),
                            pltpu.roll(y, half_dim, 1))
        k_cols.append((y * cos + partner * sin).astype(BF16))
    k_win = [jnp.concatenate([k_prev_ref[col], k_cols[col]], axis=0)
             for col in range(KV_WIDTH // V7X_LANES)]
    vt_cur = kv_ref[:, KV_WIDTH:].astype(F32).T.astype(BF16)
    vt_win = jnp.concatenate([vt_prev_ref[...], vt_cur], axis=1)

    key = lax.broadcasted_iota(jnp.int32, (2 * WINDOW, WINDOW), 0)
    qry = lax.broadcasted_iota(jnp.int32, (2 * WINDOW, WINDOW), 1)
    diff = qry + WINDOW - key
    allowed = (diff >= 0) & (diff < WINDOW) & ((key >= WINDOW) | (n > 0))

    cos_t, sin_t, gq_t = cos_t_ref[...], sin_t_ref[...], gq_t_ref[...]
    zeros = jnp.zeros((HEAD_DIM, WINDOW), BF16)

    qt_pads = []
    for pair in range(N_Q_HEADS // 2):
        sl = slice(pair * V7X_LANES, (pair + 1) * V7X_LANES)
        qt_pair = q_ref[:, sl].astype(F32).T
        for half in range(2):
            kv_head = (2 * pair + half) // Q_PER_KV
            x = qt_pair[half * HEAD_DIM:(half + 1) * HEAD_DIM, :]
            y = x * lax.rsqrt(jnp.mean(x * x, axis=0, keepdims=True) + EPS) * gq_t
            y1, y2 = y[:half_dim], y[half_dim:]
            qt = jnp.concatenate([y1 * cos_t - y2 * sin_t, y2 * cos_t + y1 * sin_t],
                                 axis=0).astype(BF16)
            qt_pads.append(jnp.concatenate([zeros, qt] if kv_head % 2 else [qt, zeros], axis=0))

    heads_per_col = N_Q_HEADS // (KV_WIDTH // V7X_LANES)
    sink_row = [jnp.full((1, WINDOW), sinks_ref[h] * LOG2_E, F32) for h in range(N_Q_HEADS)]
    probs, inv_denoms = [], []
    for col in range(KV_WIDTH // V7X_LANES):
        heads = range(col * heads_per_col, (col + 1) * heads_per_col)
        s = jnp.dot(k_win[col], jnp.concatenate([qt_pads[h] for h in heads], axis=1),
                    preferred_element_type=F32)
        s = jnp.where(jnp.concatenate([allowed] * heads_per_col, axis=1), s, MASK_VALUE)
        sink = jnp.concatenate([sink_row[h] for h in heads], axis=1)
        m = jnp.maximum(jnp.max(s, axis=0, keepdims=True), sink)
        p = jnp.exp2(s - m)
        inv_denoms.append(1.0 / (jnp.sum(p, axis=0, keepdims=True) + jnp.exp2(sink - m)))
        probs.append(p.astype(BF16))

    group_w = Q_PER_KV * WINDOW
    for kv_head in range(N_KV_HEADS):
        col, part = divmod(kv_head, N_KV_HEADS // (KV_WIDTH // V7X_LANES))
        lanes = slice(part * group_w, (part + 1) * group_w)
        vt = vt_win[kv_head * HEAD_DIM:(kv_head + 1) * HEAD_DIM, :]
        o = jnp.dot(vt, probs[col][:, lanes], preferred_element_type=F32)
        o = o * inv_denoms[col][:, lanes]
        for pair in range(Q_PER_KV // 2):
            o_pair = jnp.concatenate([o[:, (2 * pair) * WINDOW:(2 * pair + 1) * WINDOW],
                                      o[:, (2 * pair + 1) * WINDOW:(2 * pair + 2) * WINDOW]], axis=0)
            out_lane = (kv_head * Q_PER_KV + 2 * pair) * HEAD_DIM
            o_ref[:, out_lane:out_lane + V7X_LANES] = o_pair.T.astype(o_ref.dtype)

    for col in range(KV_WIDTH // V7X_LANES):
        k_prev_ref[col] = k_cols[col]
    vt_prev_ref[...] = vt_cur


def _attention(proj, sinks, tables, gq_t, gk, layer, batch, seq):
    m = proj.shape[0]
    nb = seq // WINDOW
    kv_blk = ATTN_WIDTH // (2 * KV_WIDTH)
    cos, sin_signed, cos_t, sin_t = tables
    half_dim = HEAD_DIM // 2
    row_table = pl.BlockSpec((WINDOW, V7X_LANES), lambda b, n: (n, 0))
    col_table = pl.BlockSpec((half_dim, WINDOW), lambda b, n: (0, n))
    return pl.pallas_call(
        _attn_kernel,
        grid=(batch, nb),
        in_specs=[pl.BlockSpec(memory_space=pltpu.SMEM),
                  pl.BlockSpec((WINDOW, ATTN_WIDTH), lambda b, n: (b * nb + n, 0)),
                  pl.BlockSpec((WINDOW, 2 * KV_WIDTH), lambda b, n: (b * nb + n, kv_blk)),
                  row_table, row_table, col_table, col_table,
                  pl.BlockSpec((None, HEAD_DIM, WINDOW), lambda b, n: (layer, 0, 0)),
                  pl.BlockSpec((None, 1, V7X_LANES), lambda b, n: (layer, 0, 0))],
        out_specs=pl.BlockSpec((WINDOW, ATTN_WIDTH), lambda b, n: (b * nb + n, 0)),
        out_shape=jax.ShapeDtypeStruct((m, ATTN_WIDTH), BF16),
        scratch_shapes=[pltpu.VMEM((KV_WIDTH // V7X_LANES, WINDOW, V7X_LANES), BF16),
                        pltpu.VMEM((KV_WIDTH, WINDOW), BF16)],
        compiler_params=_params(2),
        name="swa_attention",
    )(sinks, proj, proj, cos, sin_signed, cos_t, sin_t, gq_t, gk)


def _sgu_kernel(u0_ref, u1_ref, v0_ref, v1_ref, w_ref, bt_ref, g_ref, b_ref, o_ref):
    r = lax.broadcasted_iota(jnp.int32, (CHUNK, CHUNK), 0)
    c = lax.broadcasted_iota(jnp.int32, (CHUNK, CHUNK), 1)
    causal = r >= c
    lane_mean = jnp.full((SGU_GROUP_DIM, SGU_GROUP_DIM), 1.0 / SGU_GROUP_DIM, BF16)
    groups_per_tile = TN // SGU_GROUP_DIM
    n_chunks = o_ref.shape[0] // CHUNK
    for grp in range(SGU_GROUPS):
        sl = slice(grp * SGU_GROUP_DIM, (grp + 1) * SGU_GROUP_DIM)
        tsl = slice((grp % groups_per_tile) * SGU_GROUP_DIM, (grp % groups_per_tile + 1) * SGU_GROUP_DIM)
        u_ref = (u0_ref, u1_ref)[grp // groups_per_tile]
        v_ref = (v0_ref, v1_ref)[grp // groups_per_tile]
        v = v_ref[:, tsl].astype(F32)
        mu = jnp.dot(v.astype(BF16), lane_mean, preferred_element_type=F32)
        dv = v - mu
        var = jnp.dot((dv * dv).astype(BF16), lane_mean, preferred_element_type=F32)
        vn = (dv * lax.rsqrt(var + EPS) * g_ref[:, sl] + b_ref[:, sl]).astype(BF16)
        w = jnp.where(causal, w_ref[grp], 0.0).astype(BF16)
        vn_chunks = [vn[ch * CHUNK:(ch + 1) * CHUNK, :] for ch in range(n_chunks)]
        s = jnp.dot(w, jnp.concatenate(vn_chunks, axis=-1), preferred_element_type=F32)
        s = s + bt_ref[:, grp:grp + 1]
        for ch in range(n_chunks):
            u = u_ref[ch * CHUNK:(ch + 1) * CHUNK, tsl].astype(F32)
            o_ref[ch * CHUNK:(ch + 1) * CHUNK, sl] = (
                u * s[:, ch * CHUNK:(ch + 1) * CHUNK]).astype(o_ref.dtype)


def _sgu(proj, w_s, b_t, ln_g, ln_b, layer, rows=4 * CHUNK):
    m = proj.shape[0]
    u0 = QKV_TILES

    def tile(t):
        return pl.BlockSpec((rows, TN), lambda i: (i, t))

    return pl.pallas_call(
        _sgu_kernel,
        grid=(m // rows,),
        in_specs=[tile(u0), tile(u0 + 1), tile(u0 + 2), tile(u0 + 3),
                  pl.BlockSpec((None, SGU_GROUPS, CHUNK, CHUNK), lambda i: (layer, 0, 0, 0)),
                  pl.BlockSpec((None, CHUNK, SGU_GROUPS), lambda i: (layer, 0, 0)),
                  pl.BlockSpec((None, 1, SGU_WIDTH), lambda i: (layer, 0, 0)),
                  pl.BlockSpec((None, 1, SGU_WIDTH), lambda i: (layer, 0, 0))],
        out_specs=pl.BlockSpec((rows, SGU_WIDTH), lambda i: (i, 0)),
        out_shape=jax.ShapeDtypeStruct((m, SGU_WIDTH), BF16),
        compiler_params=_params(1),
        name="sgu",
    )(proj, proj, proj, proj, w_s, b_t, ln_g, ln_b)


def _rope_tables(seq):
    pos = jnp.arange(seq, dtype=F32)
    inv_freq = jnp.power(ROPE_THETA, -jnp.arange(0, HEAD_DIM, 2, dtype=F32) / HEAD_DIM)
    ang = pos[:, None] * inv_freq[None, :]
    cos, sin = jnp.cos(ang), jnp.sin(ang)
    reps = V7X_LANES // HEAD_DIM
    return (jnp.tile(cos, (1, 2 * reps)), jnp.tile(jnp.concatenate([-sin, sin], axis=-1), (1, reps)),
            cos.T, sin.T)


def kernel(x, mix_norm, w_in, q_norm, k_norm, sinks, sgu_ln_g, sgu_ln_b, w_spatial, b_spatial,
           w_attn_branch, w_sgu_branch, w_out, ffn_norm, w_gate, w_up, w_down):
    batch, seq, d = x.shape
    depth = w_in.shape[0]
    m = batch * seq
    assert w_in.shape[-1] == (QKV_TILES + UV_TILES) * TN + 2 * d
    tables = _rope_tables(seq)
    gq_t = jnp.broadcast_to((q_norm * (HEAD_DIM ** -0.5 * LOG2_E))[:, :, None],
                            (depth, HEAD_DIM, WINDOW))
    gk = jnp.tile(k_norm, (1, V7X_LANES // HEAD_DIM)).reshape(depth, 1, V7X_LANES)
    b_t = jnp.swapaxes(b_spatial, 1, 2)
    sgu_ln_g, sgu_ln_b = (p.reshape(depth, 1, -1) for p in (sgu_ln_g, sgu_ln_b))
    mix_norm, ffn_norm = (p.reshape(depth, d, 1) for p in (mix_norm, ffn_norm))

    xf = x.reshape(m, d)
    xb, ssq = _prep(xf)
    for l in range(depth):
        proj = _in_proj(xb, ssq, mix_norm, w_in, l)
        a = _attention(proj, sinks[l], tables, gq_t, gk, l, batch, seq)
        b = _sgu(proj, w_spatial, b_t, sgu_ln_g, sgu_ln_b, l)
        merged = _merge(a, b, w_attn_branch, w_sgu_branch, proj, l)
        xf, xb, ssq = _residual_matmul(merged, w_out, xf, l, tm=1024, tn=TN, passes=1,
                                       name="out_proj", emit_norm=True)
        act = _swiglu(xb, ssq, ffn_norm, w_gate, w_up, l)
        last = l == depth - 1
        outs = _residual_matmul(act, w_down, xf, l, tm=512, tn=TN, passes=2,
                                name="down_proj_last" if last else "down_proj", emit_norm=not last)
        xf, xb, ssq = (outs[0], None, None) if last else outs
    return xf.reshape(batch, seq, d)
```

```python
import functools

import jax
import jax.numpy as jnp
from jax import lax
from jax.experimental import pallas as pl
from jax.experimental.pallas import tpu as pltpu

F32 = jnp.float32
BF16 = jnp.bfloat16

HEAD_DIM = 64
N_Q_HEADS = 16
N_KV_HEADS = 4
Q_PER_KV = N_Q_HEADS // N_KV_HEADS
ATTN_WIDTH = N_Q_HEADS * HEAD_DIM
KV_WIDTH = N_KV_HEADS * HEAD_DIM
WINDOW = 128
ROPE_THETA = 10000.0
SGU_GROUPS = 8
SGU_GROUP_DIM = 128
SGU_WIDTH = SGU_GROUPS * SGU_GROUP_DIM
CHUNK = 128
EPS = 1e-6
MASK_VALUE = -1e30
LOG2_E = 1.4426950408889634

V7X_LANES = 128
V7X_VMEM_BYTES = 64 * 1024 * 1024
VMEM_LIMIT = V7X_VMEM_BYTES - 8 * 1024 * 1024

TN = 512
QKV_TILES = (ATTN_WIDTH + 2 * KV_WIDTH) // TN
UV_TILES = 2 * SGU_WIDTH // TN


def _params(n_axes):
    return pltpu.CompilerParams(dimension_semantics=("arbitrary",) * n_axes,
                                vmem_limit_bytes=VMEM_LIMIT)


def _rmsnorm_kernel(x_ref, g_ref, o_ref):
    x = x_ref[...]
    y = x * lax.rsqrt(jnp.mean(x * x, axis=-1, keepdims=True) + EPS)
    o_ref[...] = (y * g_ref[...]).astype(o_ref.dtype)


def _rmsnorm(x, gains, layer, tm=512):
    m, d = x.shape
    return pl.pallas_call(
        _rmsnorm_kernel,
        grid=(m // tm,),
        in_specs=[pl.BlockSpec((tm, d), lambda i: (i, 0)),
                  pl.BlockSpec((None, 1, d), lambda i: (layer, 0, 0))],
        out_specs=pl.BlockSpec((tm, d), lambda i: (i, 0)),
        out_shape=jax.ShapeDtypeStruct((m, d), BF16),
        compiler_params=_params(1),
        name="rmsnorm",
    )(x, gains)


SUB_ROWS = 512


def _row_blocks(o_ref):
    tm = o_ref.shape[0]
    return [pl.ds(r, SUB_ROWS) for r in range(0, tm, SUB_ROWS)]


def _cast_weights(w_refs, wbf_refs):
    @pl.when(pl.program_id(1) == 0)
    def _():
        for w_ref, wbf_ref in zip(w_refs, wbf_refs):
            wbf_ref[...] = w_ref[...].astype(BF16)


def _in_proj_kernel(h_ref, w_ref, o_ref, wbf_ref):
    _cast_weights([w_ref], [wbf_ref])
    j = pl.program_id(0)

    def emit(act):
        for rows in _row_blocks(o_ref):
            y = jnp.dot(h_ref[rows, :], wbf_ref[...], preferred_element_type=F32)
            o_ref[rows, :] = (y if act is None else act(y)).astype(o_ref.dtype)

    pl.when(j < QKV_TILES)(lambda: emit(None))
    pl.when((j >= QKV_TILES) & (j < QKV_TILES + UV_TILES))(
        lambda: emit(functools.partial(jax.nn.gelu, approximate=True)))
    pl.when(j >= QKV_TILES + UV_TILES)(lambda: emit(jax.nn.sigmoid))


def _in_proj(h, w, layer, tm=4096):
    m, k = h.shape
    n = w.shape[-1]
    return pl.pallas_call(
        _in_proj_kernel,
        grid=(n // TN, m // tm),
        in_specs=[pl.BlockSpec((tm, k), lambda j, i: (i, 0)),
                  pl.BlockSpec((None, k, TN), lambda j, i: (layer, 0, j))],
        out_specs=pl.BlockSpec((tm, TN), lambda j, i: (i, j)),
        out_shape=jax.ShapeDtypeStruct((m, n), BF16),
        scratch_shapes=[pltpu.VMEM((k, TN), BF16)],
        compiler_params=_params(2),
        name="in_proj",
    )(h, w)


def _swiglu_kernel(h_ref, wg_ref, wu_ref, o_ref, wgbf_ref, wubf_ref):
    _cast_weights([wg_ref, wu_ref], [wgbf_ref, wubf_ref])
    for rows in _row_blocks(o_ref):
        h = h_ref[rows, :]
        g = jnp.dot(h, wgbf_ref[...], preferred_element_type=F32)
        u = jnp.dot(h, wubf_ref[...], preferred_element_type=F32)
        o_ref[rows, :] = (g * jax.nn.sigmoid(g) * u).astype(o_ref.dtype)


def _swiglu(h, wg, wu, layer, tm=2048):
    m, k = h.shape
    f = wg.shape[-1]
    w_spec = pl.BlockSpec((None, k, TN), lambda j, i: (layer, 0, j))
    return pl.pallas_call(
        _swiglu_kernel,
        grid=(f // TN, m // tm),
        in_specs=[pl.BlockSpec((tm, k), lambda j, i: (i, 0)), w_spec, w_spec],
        out_specs=pl.BlockSpec((tm, TN), lambda j, i: (i, j)),
        out_shape=jax.ShapeDtypeStruct((m, f), BF16),
        scratch_shapes=[pltpu.VMEM((k, TN), BF16), pltpu.VMEM((k, TN), BF16)],
        compiler_params=_params(2),
        name="swiglu",
    )(h, wg, wu)


def _w_spec(k, tn, layer, nt, tile0=0):
    def index(p, i, j):
        return (layer, 0, tile0 + p * nt + jnp.where(i == 0, j, nt - 1))
    return pl.BlockSpec((None, k, tn), index)


def _stage_weights(w_refs, wbf_refs):
    j = pl.program_id(2)

    @pl.when(pl.program_id(1) == 0)
    def _():
        for w_ref, wbf_ref in zip(w_refs, wbf_refs):
            wbf_ref[j] = w_ref[...].astype(BF16)


def _merge_kernel(a_ref, b_ref, wa_ref, wb_ref, ga_ref, gb_ref, o_ref, wabf_ref, wbbf_ref):
    _stage_weights([wa_ref, wb_ref], [wabf_ref, wbbf_ref])
    j = pl.program_id(2)
    for rows in _row_blocks(o_ref):
        ya = jnp.dot(a_ref[rows, :], wabf_ref[j], preferred_element_type=F32)
        yb = jnp.dot(b_ref[rows, :], wbbf_ref[j], preferred_element_type=F32)
        o_ref[rows, :] = (ga_ref[rows, :] * ya + gb_ref[rows, :] * yb).astype(o_ref.dtype)


def _merge(a, b, wa, wb, proj, layer, tm=2048):
    m, k = a.shape
    d = wa.shape[-1]
    nt = d // TN
    gate0 = QKV_TILES + UV_TILES
    return pl.pallas_call(
        _merge_kernel,
        grid=(1, m // tm, nt),
        in_specs=[pl.BlockSpec((tm, k), lambda p, i, j: (i, 0)),
                  pl.BlockSpec((tm, k), lambda p, i, j: (i, 0)),
                  _w_spec(k, TN, layer, nt),
                  _w_spec(k, TN, layer, nt),
                  pl.BlockSpec((tm, TN), lambda p, i, j: (i, gate0 + j)),
                  pl.BlockSpec((tm, TN), lambda p, i, j: (i, gate0 + nt + j))],
        out_specs=pl.BlockSpec((tm, TN), lambda p, i, j: (i, j)),
        out_shape=jax.ShapeDtypeStruct((m, d), BF16),
        scratch_shapes=[pltpu.VMEM((nt, k, TN), BF16), pltpu.VMEM((nt, k, TN), BF16)],
        compiler_params=_params(3),
        name="merge",
    )(a, b, wa, wb, proj, proj)


def _residual_kernel(h_ref, w_ref, x_ref, o_ref, wbf_ref):
    _stage_weights([w_ref], [wbf_ref])
    j = pl.program_id(2)
    for rows in _row_blocks(o_ref):
        y = jnp.dot(h_ref[rows, :], wbf_ref[j], preferred_element_type=F32)
        o_ref[rows, :] = x_ref[rows, :] + y


def _residual_matmul(h, w, x, layer, tm, tn, passes, name):
    m, k = h.shape
    d = w.shape[-1]
    nt = d // tn // passes
    return pl.pallas_call(
        _residual_kernel,
        grid=(passes, m // tm, nt),
        in_specs=[pl.BlockSpec((tm, k), lambda p, i, j: (i, 0)),
                  _w_spec(k, tn, layer, nt),
                  pl.BlockSpec((tm, tn), lambda p, i, j: (i, p * nt + j))],
        out_specs=pl.BlockSpec((tm, tn), lambda p, i, j: (i, p * nt + j)),
        out_shape=jax.ShapeDtypeStruct((m, d), F32),
        scratch_shapes=[pltpu.VMEM((nt, k, tn), BF16)],
        compiler_params=_params(3),
        name=name,
    )(h, w, x)


def _attn_kernel(sinks_ref, q_ref, kv_ref, cos_ref, sin_ref, cos_t_ref, sin_t_ref, gq_t_ref, gk_ref,
                 o_ref, k_prev_ref, vt_prev_ref):
    n = pl.program_id(1)
    half_dim = HEAD_DIM // 2

    @pl.when(n == 0)
    def _():
        k_prev_ref[...] = jnp.zeros_like(k_prev_ref)
        vt_prev_ref[...] = jnp.zeros_like(vt_prev_ref)

    lane = lax.broadcasted_iota(jnp.int32, (1, V7X_LANES), 1)
    even_quarter = ((lane // half_dim) % 2) == 0
    r = lax.broadcasted_iota(jnp.int32, (V7X_LANES, V7X_LANES), 0)
    c = lax.broadcasted_iota(jnp.int32, (V7X_LANES, V7X_LANES), 1)
    head_mean = jnp.where((r // HEAD_DIM) == (c // HEAD_DIM), 1.0 / HEAD_DIM, 0.0).astype(BF16)
    cos, sin = cos_ref[...], sin_ref[...]
    k_cols = []
    for col in range(KV_WIDTH // V7X_LANES):
        x = kv_ref[:, col * V7X_LANES:(col + 1) * V7X_LANES].astype(F32)
        ms = jnp.dot((x * x).astype(BF16), head_mean, preferred_element_type=F32)
        y = x * lax.rsqrt(ms + EPS) * gk_ref[...]
        partner = jnp.where(even_quarter,
                            pltpu.roll(y, V7X_LANES - half_dim, 1),
                            pltpu.roll(y, half_dim, 1))
        k_cols.append((y * cos + partner * sin).astype(BF16))
    k_win = [jnp.concatenate([k_prev_ref[col], k_cols[col]], axis=0)
             for col in range(KV_WIDTH // V7X_LANES)]
    vt_cur = kv_ref[:, KV_WIDTH:].astype(F32).T.astype(BF16)
    vt_win = jnp.concatenate([vt_prev_ref[...], vt_cur], axis=1)

    key = lax.broadcasted_iota(jnp.int32, (2 * WINDOW, WINDOW), 0)
    qry = lax.broadcasted_iota(jnp.int32, (2 * WINDOW, WINDOW), 1)
    diff = qry + WINDOW - key
    allowed = (diff >= 0) & (diff < WINDOW) & ((key >= WINDOW) | (n > 0))

    cos_t, sin_t, gq_t = cos_t_ref[...], sin_t_ref[...], gq_t_ref[...]
    zeros = jnp.zeros((HEAD_DIM, WINDOW), BF16)

    qt_pads = []
    for pair in range(N_Q_HEADS // 2):
        sl = slice(pair * V7X_LANES, (pair + 1) * V7X_LANES)
        qt_pair = q_ref[:, sl].astype(F32).T
        for half in range(2):
            kv_head = (2 * pair + half) // Q_PER_KV
            x = qt_pair[half * HEAD_DIM:(half + 1) * HEAD_DIM, :]
            y = x * lax.rsqrt(jnp.mean(x * x, axis=0, keepdims=True) + EPS) * gq_t
            y1, y2 = y[:half_dim], y[half_dim:]
            qt = jnp.concatenate([y1 * cos_t - y2 * sin_t, y2 * cos_t + y1 * sin_t],
                                 axis=0).astype(BF16)
            qt_pads.append(jnp.concatenate([zeros, qt] if kv_head % 2 else [qt, zeros], axis=0))

    heads_per_col = N_Q_HEADS // (KV_WIDTH // V7X_LANES)
    sink_row = [jnp.full((1, WINDOW), sinks_ref[h] * LOG2_E, F32) for h in range(N_Q_HEADS)]
    probs, inv_denoms = [], []
    for col in range(KV_WIDTH // V7X_LANES):
        heads = range(col * heads_per_col, (col + 1) * heads_per_col)
        s = jnp.dot(k_win[col], jnp.concatenate([qt_pads[h] for h in heads], axis=1),
                    preferred_element_type=F32)
        s = jnp.where(jnp.concatenate([allowed] * heads_per_col, axis=1), s, MASK_VALUE)
        sink = jnp.concatenate([sink_row[h] for h in heads], axis=1)
        m = jnp.maximum(jnp.max(s, axis=0, keepdims=True), sink)
        p = jnp.exp2(s - m)
        inv_denoms.append(1.0 / (jnp.sum(p, axis=0, keepdims=True) + jnp.exp2(sink - m)))
        probs.append(p.astype(BF16))

    group_w = Q_PER_KV * WINDOW
    for kv_head in range(N_KV_HEADS):
        col, part = divmod(kv_head, N_KV_HEADS // (KV_WIDTH // V7X_LANES))
        lanes = slice(part * group_w, (part + 1) * group_w)
        vt = vt_win[kv_head * HEAD_DIM:(kv_head + 1) * HEAD_DIM, :]
        o = jnp.dot(vt, probs[col][:, lanes], preferred_element_type=F32)
        o = o * inv_denoms[col][:, lanes]
        for pair in range(Q_PER_KV // 2):
            o_pair = jnp.concatenate([o[:, (2 * pair) * WINDOW:(2 * pair + 1) * WINDOW],
                                      o[:, (2 * pair + 1) * WINDOW:(2 * pair + 2) * WINDOW]], axis=0)
            out_lane = (kv_head * Q_PER_KV + 2 * pair) * HEAD_DIM
            o_ref[:, out_lane:out_lane + V7X_LANES] = o_pair.T.astype(o_ref.dtype)

    for col in range(KV_WIDTH // V7X_LANES):
        k_prev_ref[col] = k_cols[col]
    vt_prev_ref[...] = vt_cur


def _attention(proj, sinks, tables, gq_t, gk, layer, batch, seq):
    m = proj.shape[0]
    nb = seq // WINDOW
    kv_blk = ATTN_WIDTH // (2 * KV_WIDTH)
    cos, sin_signed, cos_t, sin_t = tables
    half_dim = HEAD_DIM // 2
    row_table = pl.BlockSpec((WINDOW, V7X_LANES), lambda b, n: (n, 0))
    col_table = pl.BlockSpec((half_dim, WINDOW), lambda b, n: (0, n))
    return pl.pallas_call(
        _attn_kernel,
        grid=(batch, nb),
        in_specs=[pl.BlockSpec(memory_space=pltpu.SMEM),
                  pl.BlockSpec((WINDOW, ATTN_WIDTH), lambda b, n: (b * nb + n, 0)),
                  pl.BlockSpec((WINDOW, 2 * KV_WIDTH), lambda b, n: (b * nb + n, kv_blk)),
                  row_table, row_table, col_table, col_table,
                  pl.BlockSpec((None, HEAD_DIM, WINDOW), lambda b, n: (layer, 0, 0)),
                  pl.BlockSpec((None, 1, V7X_LANES), lambda b, n: (layer, 0, 0))],
        out_specs=pl.BlockSpec((WINDOW, ATTN_WIDTH), lambda b, n: (b * nb + n, 0)),
        out_shape=jax.ShapeDtypeStruct((m, ATTN_WIDTH), BF16),
        scratch_shapes=[pltpu.VMEM((KV_WIDTH // V7X_LANES, WINDOW, V7X_LANES), BF16),
                        pltpu.VMEM((KV_WIDTH, WINDOW), BF16)],
        compiler_params=_params(2),
        name="swa_attention",
    )(sinks, proj, proj, cos, sin_signed, cos_t, sin_t, gq_t, gk)


def _sgu_kernel(u0_ref, u1_ref, v0_ref, v1_ref, w_ref, bt_ref, g_ref, b_ref, o_ref):
    r = lax.broadcasted_iota(jnp.int32, (CHUNK, CHUNK), 0)
    c = lax.broadcasted_iota(jnp.int32, (CHUNK, CHUNK), 1)
    causal = r >= c
    lane_mean = jnp.full((SGU_GROUP_DIM, SGU_GROUP_DIM), 1.0 / SGU_GROUP_DIM, BF16)
    groups_per_tile = TN // SGU_GROUP_DIM
    n_chunks = o_ref.shape[0] // CHUNK
    for grp in range(SGU_GROUPS):
        sl = slice(grp * SGU_GROUP_DIM, (grp + 1) * SGU_GROUP_DIM)
        tsl = slice((grp % groups_per_tile) * SGU_GROUP_DIM, (grp % groups_per_tile + 1) * SGU_GROUP_DIM)
        u_ref = (u0_ref, u1_ref)[grp // groups_per_tile]
        v_ref = (v0_ref, v1_ref)[grp // groups_per_tile]
        v = v_ref[:, tsl].astype(F32)
        mu = jnp.dot(v.astype(BF16), lane_mean, preferred_element_type=F32)
        dv = v - mu
        var = jnp.dot((dv * dv).astype(BF16), lane_mean, preferred_element_type=F32)
        vn = (dv * lax.rsqrt(var + EPS) * g_ref[:, sl] + b_ref[:, sl]).astype(BF16)
        w = jnp.where(causal, w_ref[grp], 0.0).astype(BF16)
        vn_chunks = [vn[ch * CHUNK:(ch + 1) * CHUNK, :] for ch in range(n_chunks)]
        s = jnp.dot(w, jnp.concatenate(vn_chunks, axis=-1), preferred_element_type=F32)
        s = s + bt_ref[:, grp:grp + 1]
        for ch in range(n_chunks):
            u = u_ref[ch * CHUNK:(ch + 1) * CHUNK, tsl].astype(F32)
            o_ref[ch * CHUNK:(ch + 1) * CHUNK, sl] = (
                u * s[:, ch * CHUNK:(ch + 1) * CHUNK]).astype(o_ref.dtype)


def _sgu(proj, w_s, b_t, ln_g, ln_b, layer, rows=4 * CHUNK):
    m = proj.shape[0]
    u0 = QKV_TILES

    def tile(t):
        return pl.BlockSpec((rows, TN), lambda i: (i, t))

    return pl.pallas_call(
        _sgu_kernel,
        grid=(m // rows,),
        in_specs=[tile(u0), tile(u0 + 1), tile(u0 + 2), tile(u0 + 3),
                  pl.BlockSpec((None, SGU_GROUPS, CHUNK, CHUNK), lambda i: (layer, 0, 0, 0)),
                  pl.BlockSpec((None, CHUNK, SGU_GROUPS), lambda i: (layer, 0, 0)),
                  pl.BlockSpec((None, 1, SGU_WIDTH), lambda i: (layer, 0, 0)),
                  pl.BlockSpec((None, 1, SGU_WIDTH), lambda i: (layer, 0, 0))],
        out_specs=pl.BlockSpec((rows, SGU_WIDTH), lambda i: (i, 0)),
        out_shape=jax.ShapeDtypeStruct((m, SGU_WIDTH), BF16),
        compiler_params=_params(1),
        name="sgu",
    )(proj, proj, proj, proj, w_s, b_t, ln_g, ln_b)


def _rope_tables(seq):
    pos = jnp.arange(seq, dtype=F32)
    inv_freq = jnp.power(ROPE_THETA, -jnp.arange(0, HEAD_DIM, 2, dtype=F32) / HEAD_DIM)
    ang = pos[:, None] * inv_freq[None, :]
    cos, sin = jnp.cos(ang), jnp.sin(ang)
    reps = V7X_LANES // HEAD_DIM
    return (jnp.tile(cos, (1, 2 * reps)), jnp.tile(jnp.concatenate([-sin, sin], axis=-1), (1, reps)),
            cos.T, sin.T)


def kernel(x, mix_norm, w_in, q_norm, k_norm, sinks, sgu_ln_g, sgu_ln_b, w_spatial, b_spatial,
           w_attn_branch, w_sgu_branch, w_out, ffn_norm, w_gate, w_up, w_down):
    batch, seq, d = x.shape
    depth = w_in.shape[0]
    m = batch * seq
    assert w_in.shape[-1] == (QKV_TILES + UV_TILES) * TN + 2 * d
    tables = _rope_tables(seq)
    gq_t = jnp.broadcast_to((q_norm * (HEAD_DIM ** -0.5 * LOG2_E))[:, :, None],
                            (depth, HEAD_DIM, WINDOW))
    gk = jnp.tile(k_norm, (1, V7X_LANES // HEAD_DIM)).reshape(depth, 1, V7X_LANES)
    b_t = jnp.swapaxes(b_spatial, 1, 2)
    mix_norm, ffn_norm, sgu_ln_g, sgu_ln_b = (
        p.reshape(depth, 1, -1) for p in (mix_norm, ffn_norm, sgu_ln_g, sgu_ln_b))

    xf = x.reshape(m, d)
    for l in range(depth):
        h = _rmsnorm(xf, mix_norm, l)
        proj = _in_proj(h, w_in, l)
        a = _attention(proj, sinks[l], tables, gq_t, gk, l, batch, seq)
        b = _sgu(proj, w_spatial, b_t, sgu_ln_g, sgu_ln_b, l)
        merged = _merge(a, b, w_attn_branch, w_sgu_branch, proj, l)
        xf = _residual_matmul(merged, w_out, xf, l, tm=2048, tn=TN, passes=1, name="out_proj")
        h2 = _rmsnorm(xf, ffn_norm, l)
        act = _swiglu(h2, w_gate, w_up, l)
        xf = _residual_matmul(act, w_down, xf, l, tm=512, tn=TN, passes=2, name="down_proj")
    return xf.reshape(batch, seq, d)
```

```python
import functools

import jax
import jax.numpy as jnp
from jax import lax
from jax.experimental import pallas as pl
from jax.experimental.pallas import tpu as pltpu

F32 = jnp.float32
BF16 = jnp.bfloat16

HEAD_DIM = 64
N_Q_HEADS = 16
N_KV_HEADS = 4
Q_PER_KV = N_Q_HEADS // N_KV_HEADS
ATTN_WIDTH = N_Q_HEADS * HEAD_DIM
KV_WIDTH = N_KV_HEADS * HEAD_DIM
WINDOW = 128
ROPE_THETA = 10000.0
SGU_GROUPS = 8
SGU_GROUP_DIM = 128
SGU_WIDTH = SGU_GROUPS * SGU_GROUP_DIM
CHUNK = 128
EPS = 1e-6
MASK_VALUE = -1e30
LOG2_E = 1.4426950408889634

V7X_LANES = 128
V7X_VMEM_BYTES = 64 * 1024 * 1024
VMEM_LIMIT = V7X_VMEM_BYTES - 8 * 1024 * 1024

TN = 512
QKV_TILES = (ATTN_WIDTH + 2 * KV_WIDTH) // TN
UV_TILES = 2 * SGU_WIDTH // TN


def _params(n_axes):
    return pltpu.CompilerParams(dimension_semantics=("arbitrary",) * n_axes,
                                vmem_limit_bytes=VMEM_LIMIT)


def _rmsnorm_kernel(x_ref, g_ref, o_ref):
    x = x_ref[...]
    y = x * lax.rsqrt(jnp.mean(x * x, axis=-1, keepdims=True) + EPS)
    o_ref[...] = (y * g_ref[...]).astype(o_ref.dtype)


def _rmsnorm(x, gains, layer, tm=512):
    m, d = x.shape
    return pl.pallas_call(
        _rmsnorm_kernel,
        grid=(m // tm,),
        in_specs=[pl.BlockSpec((tm, d), lambda i: (i, 0)),
                  pl.BlockSpec((None, 1, d), lambda i: (layer, 0, 0))],
        out_specs=pl.BlockSpec((tm, d), lambda i: (i, 0)),
        out_shape=jax.ShapeDtypeStruct((m, d), BF16),
        compiler_params=_params(1),
        name="rmsnorm",
    )(x, gains)


SUB_ROWS = 512


def _row_blocks(o_ref):
    tm = o_ref.shape[0]
    sub = min(SUB_ROWS, tm // 2)
    return [pl.ds(r, sub) for r in range(0, tm, sub)]


def _cast_weights(w_refs, wbf_refs):
    @pl.when(pl.program_id(1) == 0)
    def _():
        for w_ref, wbf_ref in zip(w_refs, wbf_refs):
            wbf_ref[...] = w_ref[...].astype(BF16)


def _in_proj_kernel(h_ref, w_ref, o_ref, wbf_ref):
    _cast_weights([w_ref], [wbf_ref])
    j = pl.program_id(0)

    def emit(act):
        for rows in _row_blocks(o_ref):
            y = jnp.dot(h_ref[rows, :], wbf_ref[...], preferred_element_type=F32)
            o_ref[rows, :] = (y if act is None else act(y)).astype(o_ref.dtype)

    pl.when(j < QKV_TILES)(lambda: emit(None))
    pl.when((j >= QKV_TILES) & (j < QKV_TILES + UV_TILES))(
        lambda: emit(functools.partial(jax.nn.gelu, approximate=True)))
    pl.when(j >= QKV_TILES + UV_TILES)(lambda: emit(jax.nn.sigmoid))


def _in_proj(h, w, layer, tm=4096):
    m, k = h.shape
    n = w.shape[-1]
    return pl.pallas_call(
        _in_proj_kernel,
        grid=(n // TN, m // tm),
        in_specs=[pl.BlockSpec((tm, k), lambda j, i: (i, 0)),
                  pl.BlockSpec((None, k, TN), lambda j, i: (layer, 0, j))],
        out_specs=pl.BlockSpec((tm, TN), lambda j, i: (i, j)),
        out_shape=jax.ShapeDtypeStruct((m, n), BF16),
        scratch_shapes=[pltpu.VMEM((k, TN), BF16)],
        compiler_params=_params(2),
        name="in_proj",
    )(h, w)


def _swiglu_kernel(h_ref, wg_ref, wu_ref, o_ref, wgbf_ref, wubf_ref):
    _cast_weights([wg_ref, wu_ref], [wgbf_ref, wubf_ref])
    for rows in _row_blocks(o_ref):
        h = h_ref[rows, :]
        g = jnp.dot(h, wgbf_ref[...], preferred_element_type=F32)
        u = jnp.dot(h, wubf_ref[...], preferred_element_type=F32)
        o_ref[rows, :] = (g * jax.nn.sigmoid(g) * u).astype(o_ref.dtype)


def _swiglu(h, wg, wu, layer, tm=2048):
    m, k = h.shape
    f = wg.shape[-1]
    w_spec = pl.BlockSpec((None, k, TN), lambda j, i: (layer, 0, j))
    return pl.pallas_call(
        _swiglu_kernel,
        grid=(f // TN, m // tm),
        in_specs=[pl.BlockSpec((tm, k), lambda j, i: (i, 0)), w_spec, w_spec],
        out_specs=pl.BlockSpec((tm, TN), lambda j, i: (i, j)),
        out_shape=jax.ShapeDtypeStruct((m, f), BF16),
        scratch_shapes=[pltpu.VMEM((k, TN), BF16), pltpu.VMEM((k, TN), BF16)],
        compiler_params=_params(2),
        name="swiglu",
    )(h, wg, wu)


def _w_spec(k, tn, layer, nt, tile0=0):
    def index(p, i, j):
        return (layer, 0, tile0 + p * nt + jnp.where(i == 0, j, nt - 1))
    return pl.BlockSpec((None, k, tn), index)


def _stage_weights(w_refs, wbf_refs):
    j = pl.program_id(2)

    @pl.when(pl.program_id(1) == 0)
    def _():
        for w_ref, wbf_ref in zip(w_refs, wbf_refs):
            wbf_ref[j] = w_ref[...].astype(BF16)


def _merge_kernel(a_ref, b_ref, wa_ref, wb_ref, ga_ref, gb_ref, o_ref, wabf_ref, wbbf_ref):
    _stage_weights([wa_ref, wb_ref], [wabf_ref, wbbf_ref])
    j = pl.program_id(2)
    for rows in _row_blocks(o_ref):
        ya = jnp.dot(a_ref[rows, :], wabf_ref[j], preferred_element_type=F32)
        yb = jnp.dot(b_ref[rows, :], wbbf_ref[j], preferred_element_type=F32)
        o_ref[rows, :] = (ga_ref[rows, :] * ya + gb_ref[rows, :] * yb).astype(o_ref.dtype)


def _merge(a, b, wa, wb, proj, layer, tm=2048):
    m, k = a.shape
    d = wa.shape[-1]
    nt = d // TN
    gate0 = QKV_TILES + UV_TILES
    return pl.pallas_call(
        _merge_kernel,
        grid=(1, m // tm, nt),
        in_specs=[pl.BlockSpec((tm, k), lambda p, i, j: (i, 0)),
                  pl.BlockSpec((tm, k), lambda p, i, j: (i, 0)),
                  _w_spec(k, TN, layer, nt),
                  _w_spec(k, TN, layer, nt),
                  pl.BlockSpec((tm, TN), lambda p, i, j: (i, gate0 + j)),
                  pl.BlockSpec((tm, TN), lambda p, i, j: (i, gate0 + nt + j))],
        out_specs=pl.BlockSpec((tm, TN), lambda p, i, j: (i, j)),
        out_shape=jax.ShapeDtypeStruct((m, d), BF16),
        scratch_shapes=[pltpu.VMEM((nt, k, TN), BF16), pltpu.VMEM((nt, k, TN), BF16)],
        compiler_params=_params(3),
        name="merge",
    )(a, b, wa, wb, proj, proj)


def _residual_kernel(h_ref, w_ref, x_ref, o_ref, wbf_ref):
    _stage_weights([w_ref], [wbf_ref])
    j = pl.program_id(2)
    for rows in _row_blocks(o_ref):
        y = jnp.dot(h_ref[rows, :], wbf_ref[j], preferred_element_type=F32)
        o_ref[rows, :] = x_ref[rows, :] + y


def _residual_matmul(h, w, x, layer, tm, tn, passes, name):
    m, k = h.shape
    d = w.shape[-1]
    nt = d // tn // passes
    return pl.pallas_call(
        _residual_kernel,
        grid=(passes, m // tm, nt),
        in_specs=[pl.BlockSpec((tm, k), lambda p, i, j: (i, 0)),
                  _w_spec(k, tn, layer, nt),
                  pl.BlockSpec((tm, tn), lambda p, i, j: (i, p * nt + j))],
        out_specs=pl.BlockSpec((tm, tn), lambda p, i, j: (i, p * nt + j)),
        out_shape=jax.ShapeDtypeStruct((m, d), F32),
        scratch_shapes=[pltpu.VMEM((nt, k, tn), BF16)],
        compiler_params=_params(3),
        name=name,
    )(h, w, x)


def _attn_kernel(sinks_ref, q_ref, kv_ref, cos_ref, sin_ref, cos_t_ref, sin_t_ref, gq_t_ref, gk_ref,
                 o_ref, k_prev_ref, vt_prev_ref):
    n = pl.program_id(1)
    half_dim = HEAD_DIM // 2

    @pl.when(n == 0)
    def _():
        k_prev_ref[...] = jnp.zeros_like(k_prev_ref)
        vt_prev_ref[...] = jnp.zeros_like(vt_prev_ref)

    lane = lax.broadcasted_iota(jnp.int32, (1, V7X_LANES), 1)
    even_quarter = ((lane // half_dim) % 2) == 0
    r = lax.broadcasted_iota(jnp.int32, (V7X_LANES, V7X_LANES), 0)
    c = lax.broadcasted_iota(jnp.int32, (V7X_LANES, V7X_LANES), 1)
    head_mean = jnp.where((r // HEAD_DIM) == (c // HEAD_DIM), 1.0 / HEAD_DIM, 0.0).astype(BF16)
    cos, sin = cos_ref[...], sin_ref[...]
    k_cols = []
    for col in range(KV_WIDTH // V7X_LANES):
        x = kv_ref[:, col * V7X_LANES:(col + 1) * V7X_LANES].astype(F32)
        ms = jnp.dot((x * x).astype(BF16), head_mean, preferred_element_type=F32)
        y = x * lax.rsqrt(ms + EPS) * gk_ref[...]
        partner = jnp.where(even_quarter,
                            pltpu.roll(y, V7X_LANES - half_dim, 1),
                            pltpu.roll(y, half_dim, 1))
        k_cols.append((y * cos + partner * sin).astype(BF16))
    k_win = [jnp.concatenate([k_prev_ref[col], k_cols[col]], axis=0)
             for col in range(KV_WIDTH // V7X_LANES)]
    vt_cur = kv_ref[:, KV_WIDTH:].astype(F32).T.astype(BF16)
    vt_win = jnp.concatenate([vt_prev_ref[...], vt_cur], axis=1)

    key = lax.broadcasted_iota(jnp.int32, (2 * WINDOW, WINDOW), 0)
    qry = lax.broadcasted_iota(jnp.int32, (2 * WINDOW, WINDOW), 1)
    diff = qry + WINDOW - key
    allowed = (diff >= 0) & (diff < WINDOW) & ((key >= WINDOW) | (n > 0))

    cos_t, sin_t, gq_t = cos_t_ref[...], sin_t_ref[...], gq_t_ref[...]
    zeros = jnp.zeros((HEAD_DIM, WINDOW), BF16)

    qt_pads = []
    for pair in range(N_Q_HEADS // 2):
        sl = slice(pair * V7X_LANES, (pair + 1) * V7X_LANES)
        qt_pair = q_ref[:, sl].astype(F32).T
        for half in range(2):
            kv_head = (2 * pair + half) // Q_PER_KV
            x = qt_pair[half * HEAD_DIM:(half + 1) * HEAD_DIM, :]
            y = x * lax.rsqrt(jnp.mean(x * x, axis=0, keepdims=True) + EPS) * gq_t
            y1, y2 = y[:half_dim], y[half_dim:]
            qt = jnp.concatenate([y1 * cos_t - y2 * sin_t, y2 * cos_t + y1 * sin_t],
                                 axis=0).astype(BF16)
            qt_pads.append(jnp.concatenate([zeros, qt] if kv_head % 2 else [qt, zeros], axis=0))

    heads_per_col = N_Q_HEADS // (KV_WIDTH // V7X_LANES)
    sink_row = [jnp.full((1, WINDOW), sinks_ref[h] * LOG2_E, F32) for h in range(N_Q_HEADS)]
    probs, inv_denoms = [], []
    for col in range(KV_WIDTH // V7X_LANES):
        heads = range(col * heads_per_col, (col + 1) * heads_per_col)
        s = jnp.dot(k_win[col], jnp.concatenate([qt_pads[h] for h in heads], axis=1),
                    preferred_element_type=F32)
        s = jnp.where(jnp.concatenate([allowed] * heads_per_col, axis=1), s, MASK_VALUE)
        sink = jnp.concatenate([sink_row[h] for h in heads], axis=1)
        m = jnp.maximum(jnp.max(s, axis=0, keepdims=True), sink)
        p = jnp.exp2(s - m)
        inv_denoms.append(1.0 / (jnp.sum(p, axis=0, keepdims=True) + jnp.exp2(sink - m)))
        probs.append(p.astype(BF16))

    group_w = Q_PER_KV * WINDOW
    for kv_head in range(N_KV_HEADS):
        col, part = divmod(kv_head, N_KV_HEADS // (KV_WIDTH // V7X_LANES))
        lanes = slice(part * group_w, (part + 1) * group_w)
        vt = vt_win[kv_head * HEAD_DIM:(kv_head + 1) * HEAD_DIM, :]
        o = jnp.dot(vt, probs[col][:, lanes], preferred_element_type=F32)
        o = o * inv_denoms[col][:, lanes]
        for pair in range(Q_PER_KV // 2):
            o_pair = jnp.concatenate([o[:, (2 * pair) * WINDOW:(2 * pair + 1) * WINDOW],
                                      o[:, (2 * pair + 1) * WINDOW:(2 * pair + 2) * WINDOW]], axis=0)
            out_lane = (kv_head * Q_PER_KV + 2 * pair) * HEAD_DIM
            o_ref[:, out_lane:out_lane + V7X_LANES] = o_pair.T.astype(o_ref.dtype)

    for col in range(KV_WIDTH // V7X_LANES):
        k_prev_ref[col] = k_cols[col]
    vt_prev_ref[...] = vt_cur


def _attention(proj, sinks, tables, gq_t, gk, layer, batch, seq):
    m = proj.shape[0]
    nb = seq // WINDOW
    kv_blk = ATTN_WIDTH // (2 * KV_WIDTH)
    cos, sin_signed, cos_t, sin_t = tables
    half_dim = HEAD_DIM // 2
    row_table = pl.BlockSpec((WINDOW, V7X_LANES), lambda b, n: (n, 0))
    col_table = pl.BlockSpec((half_dim, WINDOW), lambda b, n: (0, n))
    return pl.pallas_call(
        _attn_kernel,
        grid=(batch, nb),
        in_specs=[pl.BlockSpec(memory_space=pltpu.SMEM),
                  pl.BlockSpec((WINDOW, ATTN_WIDTH), lambda b, n: (b * nb + n, 0)),
                  pl.BlockSpec((WINDOW, 2 * KV_WIDTH), lambda b, n: (b * nb + n, kv_blk)),
                  row_table, row_table, col_table, col_table,
                  pl.BlockSpec((None, HEAD_DIM, WINDOW), lambda b, n: (layer, 0, 0)),
                  pl.BlockSpec((None, 1, V7X_LANES), lambda b, n: (layer, 0, 0))],
        out_specs=pl.BlockSpec((WINDOW, ATTN_WIDTH), lambda b, n: (b * nb + n, 0)),
        out_shape=jax.ShapeDtypeStruct((m, ATTN_WIDTH), BF16),
        scratch_shapes=[pltpu.VMEM((KV_WIDTH // V7X_LANES, WINDOW, V7X_LANES), BF16),
                        pltpu.VMEM((KV_WIDTH, WINDOW), BF16)],
        compiler_params=_params(2),
        name="swa_attention",
    )(sinks, proj, proj, cos, sin_signed, cos_t, sin_t, gq_t, gk)


def _sgu_kernel(u0_ref, u1_ref, v0_ref, v1_ref, w_ref, bt_ref, g_ref, b_ref, o_ref):
    r = lax.broadcasted_iota(jnp.int32, (CHUNK, CHUNK), 0)
    c = lax.broadcasted_iota(jnp.int32, (CHUNK, CHUNK), 1)
    causal = r >= c
    lane_mean = jnp.full((SGU_GROUP_DIM, SGU_GROUP_DIM), 1.0 / SGU_GROUP_DIM, BF16)
    groups_per_tile = TN // SGU_GROUP_DIM
    n_chunks = o_ref.shape[0] // CHUNK
    groups = range(SGU_GROUPS)

    def lanes(grp):
        return slice(grp * SGU_GROUP_DIM, (grp + 1) * SGU_GROUP_DIM)

    def tile_lanes(grp):
        return lanes(grp % groups_per_tile)

    vs = [(v0_ref, v1_ref)[grp // groups_per_tile][:, tile_lanes(grp)].astype(F32) for grp in groups]
    mus = [jnp.dot(v.astype(BF16), lane_mean, preferred_element_type=F32) for v in vs]
    dvs = [v - mu for v, mu in zip(vs, mus)]
    vars_ = [jnp.dot((dv * dv).astype(BF16), lane_mean, preferred_element_type=F32) for dv in dvs]
    ss = []
    for grp in groups:
        vn = (dvs[grp] * lax.rsqrt(vars_[grp] + EPS) * g_ref[:, lanes(grp)]
              + b_ref[:, lanes(grp)]).astype(BF16)
        w = jnp.where(causal, w_ref[grp], 0.0).astype(BF16)
        vn_chunks = [vn[ch * CHUNK:(ch + 1) * CHUNK, :] for ch in range(n_chunks)]
        ss.append(jnp.dot(w, jnp.concatenate(vn_chunks, axis=-1), preferred_element_type=F32))
    for grp in groups:
        s = ss[grp] + bt_ref[:, grp:grp + 1]
        u_ref = (u0_ref, u1_ref)[grp // groups_per_tile]
        for ch in range(n_chunks):
            u = u_ref[ch * CHUNK:(ch + 1) * CHUNK, tile_lanes(grp)].astype(F32)
            o_ref[ch * CHUNK:(ch + 1) * CHUNK, lanes(grp)] = (
                u * s[:, ch * CHUNK:(ch + 1) * CHUNK]).astype(o_ref.dtype)


def _sgu(proj, w_s, b_t, ln_g, ln_b, layer, rows=4 * CHUNK):
    m = proj.shape[0]
    u0 = QKV_TILES

    def tile(t):
        return pl.BlockSpec((rows, TN), lambda i: (i, t))

    return pl.pallas_call(
        _sgu_kernel,
        grid=(m // rows,),
        in_specs=[tile(u0), tile(u0 + 1), tile(u0 + 2), tile(u0 + 3),
                  pl.BlockSpec((None, SGU_GROUPS, CHUNK, CHUNK), lambda i: (layer, 0, 0, 0)),
                  pl.BlockSpec((None, CHUNK, SGU_GROUPS), lambda i: (layer, 0, 0)),
                  pl.BlockSpec((None, 1, SGU_WIDTH), lambda i: (layer, 0, 0)),
                  pl.BlockSpec((None, 1, SGU_WIDTH), lambda i: (layer, 0, 0))],
        out_specs=pl.BlockSpec((rows, SGU_WIDTH), lambda i: (i, 0)),
        out_shape=jax.ShapeDtypeStruct((m, SGU_WIDTH), BF16),
        compiler_params=_params(1),
        name="sgu",
    )(proj, proj, proj, proj, w_s, b_t, ln_g, ln_b)


def _rope_tables(seq):
    pos = jnp.arange(seq, dtype=F32)
    inv_freq = jnp.power(ROPE_THETA, -jnp.arange(0, HEAD_DIM, 2, dtype=F32) / HEAD_DIM)
    ang = pos[:, None] * inv_freq[None, :]
    cos, sin = jnp.cos(ang), jnp.sin(ang)
    reps = V7X_LANES // HEAD_DIM
    return (jnp.tile(cos, (1, 2 * reps)), jnp.tile(jnp.concatenate([-sin, sin], axis=-1), (1, reps)),
            cos.T, sin.T)


def kernel(x, mix_norm, w_in, q_norm, k_norm, sinks, sgu_ln_g, sgu_ln_b, w_spatial, b_spatial,
           w_attn_branch, w_sgu_branch, w_out, ffn_norm, w_gate, w_up, w_down):
    batch, seq, d = x.shape
    depth = w_in.shape[0]
    m = batch * seq
    assert w_in.shape[-1] == (QKV_TILES + UV_TILES) * TN + 2 * d
    tables = _rope_tables(seq)
    gq_t = jnp.broadcast_to((q_norm * (HEAD_DIM ** -0.5 * LOG2_E))[:, :, None],
                            (depth, HEAD_DIM, WINDOW))
    gk = jnp.tile(k_norm, (1, V7X_LANES // HEAD_DIM)).reshape(depth, 1, V7X_LANES)
    b_t = jnp.swapaxes(b_spatial, 1, 2)
    mix_norm, ffn_norm, sgu_ln_g, sgu_ln_b = (
        p.reshape(depth, 1, -1) for p in (mix_norm, ffn_norm, sgu_ln_g, sgu_ln_b))

    xf = x.reshape(m, d)
    for l in range(depth):
        h = _rmsnorm(xf, mix_norm, l)
        proj = _in_proj(h, w_in, l)
        a = _attention(proj, sinks[l], tables, gq_t, gk, l, batch, seq)
        b = _sgu(proj, w_spatial, b_t, sgu_ln_g, sgu_ln_b, l)
        merged = _merge(a, b, w_attn_branch, w_sgu_branch, proj, l)
        xf = _residual_matmul(merged, w_out, xf, l, tm=2048, tn=TN, passes=1, name="out_proj")
        h2 = _rmsnorm(xf, ffn_norm, l)
        act = _swiglu(h2, w_gate, w_up, l)
        xf = _residual_matmul(act, w_down, xf, l, tm=512, tn=TN, passes=2, name="down_proj")
    return xf.reshape(batch, seq, d)
```

```python
import functools

import jax
import jax.numpy as jnp
from jax import lax
from jax.experimental import pallas as pl
from jax.experimental.pallas import tpu as pltpu

F32 = jnp.float32
BF16 = jnp.bfloat16

HEAD_DIM = 64
N_Q_HEADS = 16
N_KV_HEADS = 4
Q_PER_KV = N_Q_HEADS // N_KV_HEADS
ATTN_WIDTH = N_Q_HEADS * HEAD_DIM
KV_WIDTH = N_KV_HEADS * HEAD_DIM
WINDOW = 128
ROPE_THETA = 10000.0
SGU_GROUPS = 8
SGU_GROUP_DIM = 128
SGU_WIDTH = SGU_GROUPS * SGU_GROUP_DIM
CHUNK = 128
EPS = 1e-6
MASK_VALUE = -1e30
LOG2_E = 1.4426950408889634

V7X_LANES = 128
V7X_VMEM_BYTES = 64 * 1024 * 1024
VMEM_LIMIT = V7X_VMEM_BYTES - 8 * 1024 * 1024

TN = 512
D_MODEL = 2048
QKV_TILES = (ATTN_WIDTH + 2 * KV_WIDTH) // TN
UV_TILES = 2 * SGU_WIDTH // TN
GATE_TILES = 2 * D_MODEL // TN
PROJ_QKV_COL = GATE_TILES * TN
PROJ_UV_TILE = GATE_TILES + QKV_TILES


def _params(n_axes):
    return pltpu.CompilerParams(dimension_semantics=("arbitrary",) * n_axes,
                                vmem_limit_bytes=VMEM_LIMIT)


def _rmsnorm_kernel(x_ref, g_ref, o_ref):
    x = x_ref[...]
    y = x * lax.rsqrt(jnp.mean(x * x, axis=-1, keepdims=True) + EPS)
    o_ref[...] = (y * g_ref[...]).astype(o_ref.dtype)


def _rmsnorm(x, gains, layer, tm=512):
    m, d = x.shape
    return pl.pallas_call(
        _rmsnorm_kernel,
        grid=(m // tm,),
        in_specs=[pl.BlockSpec((tm, d), lambda i: (i, 0)),
                  pl.BlockSpec((None, 1, d), lambda i: (layer, 0, 0))],
        out_specs=pl.BlockSpec((tm, d), lambda i: (i, 0)),
        out_shape=jax.ShapeDtypeStruct((m, d), BF16),
        compiler_params=_params(1),
        name="rmsnorm",
    )(x, gains)


SUB_ROWS = 512


def _row_blocks(o_ref):
    tm = o_ref.shape[0]
    sub = min(SUB_ROWS, tm // 2)
    return [pl.ds(r, sub) for r in range(0, tm, sub)]


def _cast_weights(w_refs, wbf_refs):
    @pl.when(pl.program_id(1) == 0)
    def _():
        for w_ref, wbf_ref in zip(w_refs, wbf_refs):
            wbf_ref[...] = w_ref[...].astype(BF16)


def _in_proj_kernel(h_ref, w_ref, o_ref, wbf_ref):
    _cast_weights([w_ref], [wbf_ref])
    j = pl.program_id(0)

    def emit(act):
        for rows in _row_blocks(o_ref):
            y = jnp.dot(h_ref[rows, :], wbf_ref[...], preferred_element_type=F32)
            o_ref[rows, :] = (y if act is None else act(y)).astype(o_ref.dtype)

    pl.when(j < QKV_TILES)(lambda: emit(None))
    pl.when((j >= QKV_TILES) & (j < QKV_TILES + UV_TILES))(
        lambda: emit(functools.partial(jax.nn.gelu, approximate=True)))
    pl.when(j >= QKV_TILES + UV_TILES)(lambda: emit(jax.nn.sigmoid))


def _in_proj(h, w, layer, tm=4096):
    m, k = h.shape
    n = w.shape[-1]
    return pl.pallas_call(
        _in_proj_kernel,
        grid=(n // TN, m // tm),
        in_specs=[pl.BlockSpec((tm, k), lambda j, i: (i, 0)),
                  pl.BlockSpec((None, k, TN), lambda j, i: (layer, 0, j))],
        out_specs=pl.BlockSpec(
            (tm, TN), lambda j, i: (i, jnp.where(j < QKV_TILES + UV_TILES, j + GATE_TILES,
                                                 j - (QKV_TILES + UV_TILES)))),
        out_shape=jax.ShapeDtypeStruct((m, n), BF16),
        scratch_shapes=[pltpu.VMEM((k, TN), BF16)],
        compiler_params=_params(2),
        name="in_proj",
    )(h, w)


def _swiglu_kernel(h_ref, wg_ref, wu_ref, o_ref, wgbf_ref, wubf_ref):
    _cast_weights([wg_ref, wu_ref], [wgbf_ref, wubf_ref])
    for rows in _row_blocks(o_ref):
        h = h_ref[rows, :]
        g = jnp.dot(h, wgbf_ref[...], preferred_element_type=F32)
        u = jnp.dot(h, wubf_ref[...], preferred_element_type=F32)
        o_ref[rows, :] = (g * jax.nn.sigmoid(g) * u).astype(o_ref.dtype)


def _swiglu(h, wg, wu, layer, tm=2048):
    m, k = h.shape
    f = wg.shape[-1]
    w_spec = pl.BlockSpec((None, k, TN), lambda j, i: (layer, 0, j))
    return pl.pallas_call(
        _swiglu_kernel,
        grid=(f // TN, m // tm),
        in_specs=[pl.BlockSpec((tm, k), lambda j, i: (i, 0)), w_spec, w_spec],
        out_specs=pl.BlockSpec((tm, TN), lambda j, i: (i, j)),
        out_shape=jax.ShapeDtypeStruct((m, f), BF16),
        scratch_shapes=[pltpu.VMEM((k, TN), BF16), pltpu.VMEM((k, TN), BF16)],
        compiler_params=_params(2),
        name="swiglu",
    )(h, wg, wu)


def _w_spec(k, tn, layer, nt, tile0=0):
    def index(p, i, j):
        return (layer, 0, tile0 + p * nt + jnp.where(i == 0, j, nt - 1))
    return pl.BlockSpec((None, k, tn), index)


def _stage_weights(w_refs, wbf_refs):
    j = pl.program_id(2)

    @pl.when(pl.program_id(1) == 0)
    def _():
        for w_ref, wbf_ref in zip(w_refs, wbf_refs):
            wbf_ref[j] = w_ref[...].astype(BF16)


def _mix_out_kernel(a_ref, b_ref, ga_ref, gb_ref, x_ref, wa_ref, wb_ref, wo_ref, gain_ref,
                    o_ref, h_ref, merged_ref):
    tm, d = o_ref.shape
    tiles = [slice(t * TN, (t + 1) * TN) for t in range(d // TN)]
    halves = [pl.ds(r, tm // 2) for r in (0, tm // 2)]
    for rows in halves:
        for cols in tiles:
            ya = jnp.dot(a_ref[rows, :], wa_ref[:, cols], preferred_element_type=F32)
            yb = jnp.dot(b_ref[rows, :], wb_ref[:, cols], preferred_element_type=F32)
            merged_ref[rows, cols] = (ga_ref[rows, cols] * ya + gb_ref[rows, cols] * yb
                                      ).astype(merged_ref.dtype)
    for rows in halves:
        ssq = jnp.zeros((tm // 2, 1), F32)
        for cols in tiles:
            y = x_ref[rows, cols] + jnp.dot(merged_ref[rows, :], wo_ref[:, cols],
                                            preferred_element_type=F32)
            o_ref[rows, cols] = y
            ssq = ssq + jnp.sum(y * y, axis=-1, keepdims=True)
        scale = lax.rsqrt(ssq * (1.0 / d) + EPS)
        h_ref[rows, :] = (o_ref[rows, :] * scale * gain_ref[...]).astype(h_ref.dtype)


def _mix_out(a, b, proj, x, wa_bf, wb_bf, wo_bf, gains, layer, tm=512):
    m, k = a.shape
    d = x.shape[1]
    row_tile = pl.BlockSpec((tm, d), lambda i: (i, 0))

    def whole(w):
        return pl.BlockSpec(w.shape, lambda i: (0, 0))

    return pl.pallas_call(
        _mix_out_kernel,
        grid=(m // tm,),
        in_specs=[pl.BlockSpec((tm, k), lambda i: (i, 0)),
                  pl.BlockSpec((tm, k), lambda i: (i, 0)),
                  pl.BlockSpec((tm, d), lambda i: (i, 0)),
                  pl.BlockSpec((tm, d), lambda i: (i, 1)),
                  row_tile, whole(wa_bf), whole(wb_bf), whole(wo_bf),
                  pl.BlockSpec((None, 1, d), lambda i: (layer, 0, 0))],
        out_specs=[row_tile, row_tile],
        out_shape=[jax.ShapeDtypeStruct((m, d), F32), jax.ShapeDtypeStruct((m, d), BF16)],
        scratch_shapes=[pltpu.VMEM((tm, d), BF16)],
        compiler_params=_params(1),
        name="mix_out",
    )(a, b, proj, proj, x, wa_bf, wb_bf, wo_bf, gains)


def _residual_kernel(h_ref, w_ref, x_ref, o_ref, wbf_ref):
    _stage_weights([w_ref], [wbf_ref])
    j = pl.program_id(2)
    for rows in _row_blocks(o_ref):
        y = jnp.dot(h_ref[rows, :], wbf_ref[j], preferred_element_type=F32)
        o_ref[rows, :] = x_ref[rows, :] + y


def _residual_matmul(h, w, x, layer, tm, tn, passes, name):
    m, k = h.shape
    d = w.shape[-1]
    nt = d // tn // passes
    return pl.pallas_call(
        _residual_kernel,
        grid=(passes, m // tm, nt),
        in_specs=[pl.BlockSpec((tm, k), lambda p, i, j: (i, 0)),
                  _w_spec(k, tn, layer, nt),
                  pl.BlockSpec((tm, tn), lambda p, i, j: (i, p * nt + j))],
        out_specs=pl.BlockSpec((tm, tn), lambda p, i, j: (i, p * nt + j)),
        out_shape=jax.ShapeDtypeStruct((m, d), F32),
        scratch_shapes=[pltpu.VMEM((nt, k, tn), BF16)],
        compiler_params=_params(3),
        name=name,
    )(h, w, x)


def _attn_kernel(sinks_ref, q_ref, kv_ref, cos_ref, sin_ref, cos_t_ref, sin_t_ref, gq_t_ref, gk_ref,
                 o_ref, k_prev_ref, vt_prev_ref):
    n = pl.program_id(1)
    half_dim = HEAD_DIM // 2

    @pl.when(n == 0)
    def _():
        k_prev_ref[...] = jnp.zeros_like(k_prev_ref)
        vt_prev_ref[...] = jnp.zeros_like(vt_prev_ref)

    lane = lax.broadcasted_iota(jnp.int32, (1, V7X_LANES), 1)
    even_quarter = ((lane // half_dim) % 2) == 0
    r = lax.broadcasted_iota(jnp.int32, (V7X_LANES, V7X_LANES), 0)
    c = lax.broadcasted_iota(jnp.int32, (V7X_LANES, V7X_LANES), 1)
    head_mean = jnp.where((r // HEAD_DIM) == (c // HEAD_DIM), 1.0 / HEAD_DIM, 0.0).astype(BF16)
    cos, sin = cos_ref[...], sin_ref[...]
    k_cols = []
    for col in range(KV_WIDTH // V7X_LANES):
        x = kv_ref[:, col * V7X_LANES:(col + 1) * V7X_LANES].astype(F32)
        ms = jnp.dot((x * x).astype(BF16), head_mean, preferred_element_type=F32)
        y = x * lax.rsqrt(ms + EPS) * gk_ref[...]
        partner = jnp.where(even_quarter,
                            pltpu.roll(y, V7X_LANES - half_dim, 1),
                            pltpu.roll(y, half_dim, 1))
        k_cols.append((y * cos + partner * sin).astype(BF16))
    k_win = [jnp.concatenate([k_prev_ref[col], k_cols[col]], axis=0)
             for col in range(KV_WIDTH // V7X_LANES)]
    vt_cur = kv_ref[:, KV_WIDTH:].astype(F32).T.astype(BF16)
    vt_win = jnp.concatenate([vt_prev_ref[...], vt_cur], axis=1)

    key = lax.broadcasted_iota(jnp.int32, (2 * WINDOW, WINDOW), 0)
    qry = lax.broadcasted_iota(jnp.int32, (2 * WINDOW, WINDOW), 1)
    diff = qry + WINDOW - key
    allowed = (diff >= 0) & (diff < WINDOW) & ((key >= WINDOW) | (n > 0))

    cos_t, sin_t, gq_t = cos_t_ref[...], sin_t_ref[...], gq_t_ref[...]
    zeros = jnp.zeros((HEAD_DIM, WINDOW), BF16)

    qt_pads = []
    for pair in range(N_Q_HEADS // 2):
        sl = slice(pair * V7X_LANES, (pair + 1) * V7X_LANES)
        qt_pair = q_ref[:, sl].astype(F32).T
        for half in range(2):
            kv_head = (2 * pair + half) // Q_PER_KV
            x = qt_pair[half * HEAD_DIM:(half + 1) * HEAD_DIM, :]
            y = x * lax.rsqrt(jnp.mean(x * x, axis=0, keepdims=True) + EPS) * gq_t
            y1, y2 = y[:half_dim], y[half_dim:]
            qt = jnp.concatenate([y1 * cos_t - y2 * sin_t, y2 * cos_t + y1 * sin_t],
                                 axis=0).astype(BF16)
            qt_pads.append(jnp.concatenate([zeros, qt] if kv_head % 2 else [qt, zeros], axis=0))

    heads_per_col = N_Q_HEADS // (KV_WIDTH // V7X_LANES)
    sink_row = [jnp.full((1, WINDOW), sinks_ref[h] * LOG2_E, F32) for h in range(N_Q_HEADS)]
    probs, inv_denoms = [], []
    for col in range(KV_WIDTH // V7X_LANES):
        heads = range(col * heads_per_col, (col + 1) * heads_per_col)
        s = jnp.dot(k_win[col], jnp.concatenate([qt_pads[h] for h in heads], axis=1),
                    preferred_element_type=F32)
        s = jnp.where(jnp.concatenate([allowed] * heads_per_col, axis=1), s, MASK_VALUE)
        sink = jnp.concatenate([sink_row[h] for h in heads], axis=1)
        m = jnp.maximum(jnp.max(s, axis=0, keepdims=True), sink)
        p = jnp.exp2(s - m)
        inv_denoms.append(1.0 / (jnp.sum(p, axis=0, keepdims=True) + jnp.exp2(sink - m)))
        probs.append(p.astype(BF16))

    group_w = Q_PER_KV * WINDOW
    for kv_head in range(N_KV_HEADS):
        col, part = divmod(kv_head, N_KV_HEADS // (KV_WIDTH // V7X_LANES))
        lanes = slice(part * group_w, (part + 1) * group_w)
        vt = vt_win[kv_head * HEAD_DIM:(kv_head + 1) * HEAD_DIM, :]
        o = jnp.dot(vt, probs[col][:, lanes], preferred_element_type=F32)
        o = o * inv_denoms[col][:, lanes]
        for pair in range(Q_PER_KV // 2):
            o_pair = jnp.concatenate([o[:, (2 * pair) * WINDOW:(2 * pair + 1) * WINDOW],
                                      o[:, (2 * pair + 1) * WINDOW:(2 * pair + 2) * WINDOW]], axis=0)
            out_lane = (kv_head * Q_PER_KV + 2 * pair) * HEAD_DIM
            o_ref[:, out_lane:out_lane + V7X_LANES] = o_pair.T.astype(o_ref.dtype)

    for col in range(KV_WIDTH // V7X_LANES):
        k_prev_ref[col] = k_cols[col]
    vt_prev_ref[...] = vt_cur


def _attention(proj, sinks, tables, gq_t, gk, layer, batch, seq):
    m = proj.shape[0]
    nb = seq // WINDOW
    q_blk = PROJ_QKV_COL // ATTN_WIDTH
    kv_blk = (PROJ_QKV_COL + ATTN_WIDTH) // (2 * KV_WIDTH)
    cos, sin_signed, cos_t, sin_t = tables
    half_dim = HEAD_DIM // 2
    row_table = pl.BlockSpec((WINDOW, V7X_LANES), lambda b, n: (n, 0))
    col_table = pl.BlockSpec((half_dim, WINDOW), lambda b, n: (0, n))
    return pl.pallas_call(
        _attn_kernel,
        grid=(batch, nb),
        in_specs=[pl.BlockSpec(memory_space=pltpu.SMEM),
                  pl.BlockSpec((WINDOW, ATTN_WIDTH), lambda b, n: (b * nb + n, q_blk)),
                  pl.BlockSpec((WINDOW, 2 * KV_WIDTH), lambda b, n: (b * nb + n, kv_blk)),
                  row_table, row_table, col_table, col_table,
                  pl.BlockSpec((None, HEAD_DIM, WINDOW), lambda b, n: (layer, 0, 0)),
                  pl.BlockSpec((None, 1, V7X_LANES), lambda b, n: (layer, 0, 0))],
        out_specs=pl.BlockSpec((WINDOW, ATTN_WIDTH), lambda b, n: (b * nb + n, 0)),
        out_shape=jax.ShapeDtypeStruct((m, ATTN_WIDTH), BF16),
        scratch_shapes=[pltpu.VMEM((KV_WIDTH // V7X_LANES, WINDOW, V7X_LANES), BF16),
                        pltpu.VMEM((KV_WIDTH, WINDOW), BF16)],
        compiler_params=_params(2),
        name="swa_attention",
    )(sinks, proj, proj, cos, sin_signed, cos_t, sin_t, gq_t, gk)


def _sgu_kernel(u0_ref, u1_ref, v0_ref, v1_ref, w_ref, bt_ref, g_ref, b_ref, wa_ref, wb_ref, wo_ref,
                o_ref, wa_bf_ref, wb_bf_ref, wo_bf_ref):
    for src, dst in ((wa_ref, wa_bf_ref), (wb_ref, wb_bf_ref), (wo_ref, wo_bf_ref)):
        dst[...] = src[...].astype(dst.dtype)

    r = lax.broadcasted_iota(jnp.int32, (CHUNK, CHUNK), 0)
    c = lax.broadcasted_iota(jnp.int32, (CHUNK, CHUNK), 1)
    causal = r >= c
    lane_mean = jnp.full((SGU_GROUP_DIM, SGU_GROUP_DIM), 1.0 / SGU_GROUP_DIM, BF16)
    groups_per_tile = TN // SGU_GROUP_DIM
    n_chunks = o_ref.shape[0] // CHUNK
    groups = range(SGU_GROUPS)

    def lanes(grp):
        return slice(grp * SGU_GROUP_DIM, (grp + 1) * SGU_GROUP_DIM)

    def tile_lanes(grp):
        return lanes(grp % groups_per_tile)

    vs = [(v0_ref, v1_ref)[grp // groups_per_tile][:, tile_lanes(grp)].astype(F32) for grp in groups]
    mus = [jnp.dot(v.astype(BF16), lane_mean, preferred_element_type=F32) for v in vs]
    dvs = [v - mu for v, mu in zip(vs, mus)]
    vars_ = [jnp.dot((dv * dv).astype(BF16), lane_mean, preferred_element_type=F32) for dv in dvs]
    ss = []
    for grp in groups:
        vn = (dvs[grp] * lax.rsqrt(vars_[grp] + EPS) * g_ref[:, lanes(grp)]
              + b_ref[:, lanes(grp)]).astype(BF16)
        w = jnp.where(causal, w_ref[grp], 0.0).astype(BF16)
        vn_chunks = [vn[ch * CHUNK:(ch + 1) * CHUNK, :] for ch in range(n_chunks)]
        ss.append(jnp.dot(w, jnp.concatenate(vn_chunks, axis=-1), preferred_element_type=F32))
    for grp in groups:
        s = ss[grp] + bt_ref[:, grp:grp + 1]
        u_ref = (u0_ref, u1_ref)[grp // groups_per_tile]
        for ch in range(n_chunks):
            u = u_ref[ch * CHUNK:(ch + 1) * CHUNK, tile_lanes(grp)].astype(F32)
            o_ref[ch * CHUNK:(ch + 1) * CHUNK, lanes(grp)] = (
                u * s[:, ch * CHUNK:(ch + 1) * CHUNK]).astype(o_ref.dtype)


def _sgu(proj, w_s, b_t, ln_g, ln_b, wa, wb, wo, layer, rows=4 * CHUNK):
    m = proj.shape[0]
    steps = m // rows
    u0 = PROJ_UV_TILE

    def tile(t):
        return pl.BlockSpec((rows, TN), lambda i: (i, t))

    def slab(w):
        return pl.BlockSpec((None, w.shape[1] // steps, w.shape[2]), lambda i: (layer, i, 0))

    def slab_out(w):
        return pl.BlockSpec((w.shape[1] // steps, w.shape[2]), lambda i: (i, 0))

    return pl.pallas_call(
        _sgu_kernel,
        grid=(steps,),
        in_specs=[tile(u0), tile(u0 + 1), tile(u0 + 2), tile(u0 + 3),
                  pl.BlockSpec((None, SGU_GROUPS, CHUNK, CHUNK), lambda i: (layer, 0, 0, 0)),
                  pl.BlockSpec((None, CHUNK, SGU_GROUPS), lambda i: (layer, 0, 0)),
                  pl.BlockSpec((None, 1, SGU_WIDTH), lambda i: (layer, 0, 0)),
                  pl.BlockSpec((None, 1, SGU_WIDTH), lambda i: (layer, 0, 0)),
                  slab(wa), slab(wb), slab(wo)],
        out_specs=[pl.BlockSpec((rows, SGU_WIDTH), lambda i: (i, 0)),
                   slab_out(wa), slab_out(wb), slab_out(wo)],
        out_shape=[jax.ShapeDtypeStruct((m, SGU_WIDTH), BF16)]
        + [jax.ShapeDtypeStruct(w.shape[1:], BF16) for w in (wa, wb, wo)],
        compiler_params=_params(1),
        name="sgu",
    )(proj, proj, proj, proj, w_s, b_t, ln_g, ln_b, wa, wb, wo)


def _rope_tables(seq):
    pos = jnp.arange(seq, dtype=F32)
    inv_freq = jnp.power(ROPE_THETA, -jnp.arange(0, HEAD_DIM, 2, dtype=F32) / HEAD_DIM)
    ang = pos[:, None] * inv_freq[None, :]
    cos, sin = jnp.cos(ang), jnp.sin(ang)
    reps = V7X_LANES // HEAD_DIM
    return (jnp.tile(cos, (1, 2 * reps)), jnp.tile(jnp.concatenate([-sin, sin], axis=-1), (1, reps)),
            cos.T, sin.T)


def kernel(x, mix_norm, w_in, q_norm, k_norm, sinks, sgu_ln_g, sgu_ln_b, w_spatial, b_spatial,
           w_attn_branch, w_sgu_branch, w_out, ffn_norm, w_gate, w_up, w_down):
    batch, seq, d = x.shape
    depth = w_in.shape[0]
    m = batch * seq
    assert d == D_MODEL and w_in.shape[-1] == (QKV_TILES + UV_TILES + GATE_TILES) * TN
    tables = _rope_tables(seq)
    gq_t = jnp.broadcast_to((q_norm * (HEAD_DIM ** -0.5 * LOG2_E))[:, :, None],
                            (depth, HEAD_DIM, WINDOW))
    gk = jnp.tile(k_norm, (1, V7X_LANES // HEAD_DIM)).reshape(depth, 1, V7X_LANES)
    b_t = jnp.swapaxes(b_spatial, 1, 2)
    mix_norm, ffn_norm, sgu_ln_g, sgu_ln_b = (
        p.reshape(depth, 1, -1) for p in (mix_norm, ffn_norm, sgu_ln_g, sgu_ln_b))

    xf = x.reshape(m, d)
    for l in range(depth):
        h = _rmsnorm(xf, mix_norm, l)
        proj = _in_proj(h, w_in, l)
        a = _attention(proj, sinks[l], tables, gq_t, gk, l, batch, seq)
        b, wa_bf, wb_bf, wo_bf = _sgu(proj, w_spatial, b_t, sgu_ln_g, sgu_ln_b,
                                      w_attn_branch, w_sgu_branch, w_out, l)
        xf, h2 = _mix_out(a, b, proj, xf, wa_bf, wb_bf, wo_bf, ffn_norm, l)
        act = _swiglu(h2, w_gate, w_up, l)
        xf = _residual_matmul(act, w_down, xf, l, tm=512, tn=TN, passes=2, name="down_proj")
    return xf.reshape(batch, seq, d)
```

```python
import functools

import jax
import jax.numpy as jnp
from jax import lax
from jax.experimental import pallas as pl
from jax.experimental.pallas import tpu as pltpu

F32 = jnp.float32
BF16 = jnp.bfloat16

HEAD_DIM = 64
N_Q_HEADS = 16
N_KV_HEADS = 4
Q_PER_KV = N_Q_HEADS // N_KV_HEADS
ATTN_WIDTH = N_Q_HEADS * HEAD_DIM
KV_WIDTH = N_KV_HEADS * HEAD_DIM
WINDOW = 128
ROPE_THETA = 10000.0
SGU_GROUPS = 8
SGU_GROUP_DIM = 128
SGU_WIDTH = SGU_GROUPS * SGU_GROUP_DIM
CHUNK = 128
EPS = 1e-6
MASK_VALUE = -1e30
LOG2_E = 1.4426950408889634

V7X_LANES = 128
V7X_VMEM_BYTES = 64 * 1024 * 1024
VMEM_LIMIT = V7X_VMEM_BYTES - 8 * 1024 * 1024

TN = 512
D_MODEL = 2048
QKV_TILES = (ATTN_WIDTH + 2 * KV_WIDTH) // TN
UV_TILES = 2 * SGU_WIDTH // TN
GATE_TILES = 2 * D_MODEL // TN
PROJ_QKV_COL = GATE_TILES * TN
PROJ_UV_TILE = GATE_TILES + QKV_TILES


def _params(n_axes):
    return pltpu.CompilerParams(dimension_semantics=("arbitrary",) * n_axes,
                                vmem_limit_bytes=VMEM_LIMIT)


def _rmsnorm_kernel(x_ref, g_ref, o_ref):
    x = x_ref[...]
    y = x * lax.rsqrt(jnp.mean(x * x, axis=-1, keepdims=True) + EPS)
    o_ref[...] = (y * g_ref[...]).astype(o_ref.dtype)


def _rmsnorm(x, gains, layer, tm=512):
    m, d = x.shape
    return pl.pallas_call(
        _rmsnorm_kernel,
        grid=(m // tm,),
        in_specs=[pl.BlockSpec((tm, d), lambda i: (i, 0)),
                  pl.BlockSpec((None, 1, d), lambda i: (layer, 0, 0))],
        out_specs=pl.BlockSpec((tm, d), lambda i: (i, 0)),
        out_shape=jax.ShapeDtypeStruct((m, d), BF16),
        compiler_params=_params(1),
        name="rmsnorm",
    )(x, gains)


SUB_ROWS = 512


def _row_blocks(o_ref):
    tm = o_ref.shape[0]
    sub = min(SUB_ROWS, tm // 2)
    return [pl.ds(r, sub) for r in range(0, tm, sub)]


def _cast_weights(w_refs, wbf_refs):
    @pl.when(pl.program_id(1) == 0)
    def _():
        for w_ref, wbf_ref in zip(w_refs, wbf_refs):
            wbf_ref[...] = w_ref[...].astype(BF16)


def _in_proj_kernel(h_ref, w_ref, o_ref, wbf_ref):
    _cast_weights([w_ref], [wbf_ref])
    j = pl.program_id(0)

    def emit(act):
        for rows in _row_blocks(o_ref):
            y = jnp.dot(h_ref[rows, :], wbf_ref[...], preferred_element_type=F32)
            o_ref[rows, :] = (y if act is None else act(y)).astype(o_ref.dtype)

    pl.when(j < QKV_TILES)(lambda: emit(None))
    pl.when((j >= QKV_TILES) & (j < QKV_TILES + UV_TILES))(
        lambda: emit(functools.partial(jax.nn.gelu, approximate=True)))
    pl.when(j >= QKV_TILES + UV_TILES)(lambda: emit(jax.nn.sigmoid))


def _in_proj(h, w, layer, tm=4096):
    m, k = h.shape
    n = w.shape[-1]
    return pl.pallas_call(
        _in_proj_kernel,
        grid=(n // TN, m // tm),
        in_specs=[pl.BlockSpec((tm, k), lambda j, i: (i, 0)),
                  pl.BlockSpec((None, k, TN), lambda j, i: (layer, 0, j))],
        out_specs=pl.BlockSpec(
            (tm, TN), lambda j, i: (i, jnp.where(j < QKV_TILES + UV_TILES, j + GATE_TILES,
                                                 j - (QKV_TILES + UV_TILES)))),
        out_shape=jax.ShapeDtypeStruct((m, n), BF16),
        scratch_shapes=[pltpu.VMEM((k, TN), BF16)],
        compiler_params=_params(2),
        name="in_proj",
    )(h, w)


def _swiglu_kernel(h_ref, wg_ref, wu_ref, wd_ref, o_ref, wd_bf_ref, wgbf_ref, wubf_ref):
    _cast_weights([wg_ref, wu_ref], [wgbf_ref, wubf_ref])
    wd_bf_ref[...] = wd_ref[...].astype(wd_bf_ref.dtype)
    for rows in _row_blocks(o_ref):
        h = h_ref[rows, :]
        g = jnp.dot(h, wgbf_ref[...], preferred_element_type=F32)
        u = jnp.dot(h, wubf_ref[...], preferred_element_type=F32)
        o_ref[rows, :] = (g * jax.nn.sigmoid(g) * u).astype(o_ref.dtype)


def _swiglu(h, wg, wu, wd, layer, tm=2048):
    m, k = h.shape
    f = wg.shape[-1]
    mt = m // tm
    slab = wd.shape[1] // (f // TN * mt)
    w_spec = pl.BlockSpec((None, k, TN), lambda j, i: (layer, 0, j))
    return pl.pallas_call(
        _swiglu_kernel,
        grid=(f // TN, mt),
        in_specs=[pl.BlockSpec((tm, k), lambda j, i: (i, 0)), w_spec, w_spec,
                  pl.BlockSpec((None, slab, wd.shape[2]), lambda j, i: (layer, j * mt + i, 0))],
        out_specs=[pl.BlockSpec((tm, TN), lambda j, i: (i, j)),
                   pl.BlockSpec((slab, wd.shape[2]), lambda j, i: (j * mt + i, 0))],
        out_shape=[jax.ShapeDtypeStruct((m, f), BF16), jax.ShapeDtypeStruct(wd.shape[1:], BF16)],
        scratch_shapes=[pltpu.VMEM((k, TN), BF16), pltpu.VMEM((k, TN), BF16)],
        compiler_params=_params(2),
        name="swiglu",
    )(h, wg, wu, wd)


def _mix_out_kernel(a_ref, b_ref, ga_ref, gb_ref, x_ref, wa_ref, wb_ref, wo_ref, gain_ref,
                    o_ref, h_ref, merged_ref):
    tm, d = o_ref.shape
    tiles = [slice(t * TN, (t + 1) * TN) for t in range(d // TN)]
    halves = [pl.ds(r, tm // 2) for r in (0, tm // 2)]
    for rows in halves:
        for cols in tiles:
            ya = jnp.dot(a_ref[rows, :], wa_ref[:, cols], preferred_element_type=F32)
            yb = jnp.dot(b_ref[rows, :], wb_ref[:, cols], preferred_element_type=F32)
            merged_ref[rows, cols] = (ga_ref[rows, cols] * ya + gb_ref[rows, cols] * yb
                                      ).astype(merged_ref.dtype)
    for rows in halves:
        ssq = jnp.zeros((tm // 2, 1), F32)
        for cols in tiles:
            y = x_ref[rows, cols] + jnp.dot(merged_ref[rows, :], wo_ref[:, cols],
                                            preferred_element_type=F32)
            o_ref[rows, cols] = y
            ssq = ssq + jnp.sum(y * y, axis=-1, keepdims=True)
        scale = lax.rsqrt(ssq * (1.0 / d) + EPS)
        h_ref[rows, :] = (o_ref[rows, :] * scale * gain_ref[...]).astype(h_ref.dtype)


def _mix_out(a, b, proj, x, wa_bf, wb_bf, wo_bf, gains, layer, tm=512):
    m, k = a.shape
    d = x.shape[1]
    row_tile = pl.BlockSpec((tm, d), lambda i: (i, 0))

    def whole(w):
        return pl.BlockSpec(w.shape, lambda i: (0, 0))

    return pl.pallas_call(
        _mix_out_kernel,
        grid=(m // tm,),
        in_specs=[pl.BlockSpec((tm, k), lambda i: (i, 0)),
                  pl.BlockSpec((tm, k), lambda i: (i, 0)),
                  pl.BlockSpec((tm, d), lambda i: (i, 0)),
                  pl.BlockSpec((tm, d), lambda i: (i, 1)),
                  row_tile, whole(wa_bf), whole(wb_bf), whole(wo_bf),
                  pl.BlockSpec((None, 1, d), lambda i: (layer, 0, 0))],
        out_specs=[row_tile, row_tile],
        out_shape=[jax.ShapeDtypeStruct((m, d), F32), jax.ShapeDtypeStruct((m, d), BF16)],
        scratch_shapes=[pltpu.VMEM((tm, d), BF16)],
        compiler_params=_params(1),
        name="mix_out",
    )(a, b, proj, proj, x, wa_bf, wb_bf, wo_bf, gains)


def _down_out_kernel(act_ref, x_ref, w_ref, *rest):
    gain_ref, o_ref, h_ref = rest if len(rest) == 3 else (None, rest[0], None)
    tm, d = o_ref.shape
    tiles = [slice(t * TN, (t + 1) * TN) for t in range(d // TN)]
    for rows in (pl.ds(r, tm // 2) for r in (0, tm // 2)):
        ssq = jnp.zeros((tm // 2, 1), F32)
        for cols in tiles:
            y = x_ref[rows, cols] + jnp.dot(act_ref[rows, :], w_ref[:, cols],
                                            preferred_element_type=F32)
            o_ref[rows, cols] = y
            if h_ref is not None:
                ssq = ssq + jnp.sum(y * y, axis=-1, keepdims=True)
        if h_ref is not None:
            scale = lax.rsqrt(ssq * (1.0 / d) + EPS)
            h_ref[rows, :] = (o_ref[rows, :] * scale * gain_ref[...]).astype(h_ref.dtype)


def _down_out(act, x, w_bf, gains, layer, tm=512):
    m, k = act.shape
    d = x.shape[1]
    row_tile = pl.BlockSpec((tm, d), lambda i: (i, 0))
    in_specs = [pl.BlockSpec((tm, k), lambda i: (i, 0)), row_tile,
                pl.BlockSpec(w_bf.shape, lambda i: (0, 0))]
    operands = [act, x, w_bf]
    out_specs, out_shape = [row_tile], [jax.ShapeDtypeStruct((m, d), F32)]
    if gains is not None:
        in_specs.append(pl.BlockSpec((None, 1, d), lambda i: (layer, 0, 0)))
        operands.append(gains)
        out_specs.append(row_tile)
        out_shape.append(jax.ShapeDtypeStruct((m, d), BF16))
    return pl.pallas_call(
        _down_out_kernel,
        grid=(m // tm,),
        in_specs=in_specs,
        out_specs=out_specs,
        out_shape=out_shape,
        compiler_params=_params(1),
        name="down_out" if gains is not None else "down_out_last",
    )(*operands)


def _attn_kernel(sinks_ref, q_ref, kv_ref, cos_ref, sin_ref, cos_t_ref, sin_t_ref, gq_t_ref, gk_ref,
                 o_ref, k_prev_ref, vt_prev_ref):
    n = pl.program_id(1)
    half_dim = HEAD_DIM // 2

    @pl.when(n == 0)
    def _():
        k_prev_ref[...] = jnp.zeros_like(k_prev_ref)
        vt_prev_ref[...] = jnp.zeros_like(vt_prev_ref)

    lane = lax.broadcasted_iota(jnp.int32, (1, V7X_LANES), 1)
    even_quarter = ((lane // half_dim) % 2) == 0
    r = lax.broadcasted_iota(jnp.int32, (V7X_LANES, V7X_LANES), 0)
    c = lax.broadcasted_iota(jnp.int32, (V7X_LANES, V7X_LANES), 1)
    head_mean = jnp.where((r // HEAD_DIM) == (c // HEAD_DIM), 1.0 / HEAD_DIM, 0.0).astype(BF16)
    cos, sin = cos_ref[...], sin_ref[...]
    k_cols = []
    for col in range(KV_WIDTH // V7X_LANES):
        x = kv_ref[:, col * V7X_LANES:(col + 1) * V7X_LANES].astype(F32)
        ms = jnp.dot((x * x).astype(BF16), head_mean, preferred_element_type=F32)
        y = x * lax.rsqrt(ms + EPS) * gk_ref[...]
        partner = jnp.where(even_quarter,
                            pltpu.roll(y, V7X_LANES - half_dim, 1),
                            pltpu.roll(y, half_dim, 1))
        k_cols.append((y * cos + partner * sin).astype(BF16))
    k_win = [jnp.concatenate([k_prev_ref[col], k_cols[col]], axis=0)
             for col in range(KV_WIDTH // V7X_LANES)]
    vt_cur = kv_ref[:, KV_WIDTH:].astype(F32).T.astype(BF16)
    vt_win = jnp.concatenate([vt_prev_ref[...], vt_cur], axis=1)

    key = lax.broadcasted_iota(jnp.int32, (2 * WINDOW, WINDOW), 0)
    qry = lax.broadcasted_iota(jnp.int32, (2 * WINDOW, WINDOW), 1)
    diff = qry + WINDOW - key
    allowed = (diff >= 0) & (diff < WINDOW) & ((key >= WINDOW) | (n > 0))

    cos_t, sin_t, gq_t = cos_t_ref[...], sin_t_ref[...], gq_t_ref[...]
    zeros = jnp.zeros((HEAD_DIM, WINDOW), BF16)

    qt_pads = []
    for pair in range(N_Q_HEADS // 2):
        sl = slice(pair * V7X_LANES, (pair + 1) * V7X_LANES)
        qt_pair = q_ref[:, sl].astype(F32).T
        for half in range(2):
            kv_head = (2 * pair + half) // Q_PER_KV
            x = qt_pair[half * HEAD_DIM:(half + 1) * HEAD_DIM, :]
            y = x * lax.rsqrt(jnp.mean(x * x, axis=0, keepdims=True) + EPS) * gq_t
            y1, y2 = y[:half_dim], y[half_dim:]
            qt = jnp.concatenate([y1 * cos_t - y2 * sin_t, y2 * cos_t + y1 * sin_t],
                                 axis=0).astype(BF16)
            qt_pads.append(jnp.concatenate([zeros, qt] if kv_head % 2 else [qt, zeros], axis=0))

    heads_per_col = N_Q_HEADS // (KV_WIDTH // V7X_LANES)
    sink_row = [jnp.full((1, WINDOW), sinks_ref[h] * LOG2_E, F32) for h in range(N_Q_HEADS)]
    probs, inv_denoms = [], []
    for col in range(KV_WIDTH // V7X_LANES):
        heads = range(col * heads_per_col, (col + 1) * heads_per_col)
        s = jnp.dot(k_win[col], jnp.concatenate([qt_pads[h] for h in heads], axis=1),
                    preferred_element_type=F32)
        s = jnp.where(jnp.concatenate([allowed] * heads_per_col, axis=1), s, MASK_VALUE)
        sink = jnp.concatenate([sink_row[h] for h in heads], axis=1)
        m = jnp.maximum(jnp.max(s, axis=0, keepdims=True), sink)
        p = jnp.exp2(s - m)
        inv_denoms.append(1.0 / (jnp.sum(p, axis=0, keepdims=True) + jnp.exp2(sink - m)))
        probs.append(p.astype(BF16))

    group_w = Q_PER_KV * WINDOW
    for kv_head in range(N_KV_HEADS):
        col, part = divmod(kv_head, N_KV_HEADS // (KV_WIDTH // V7X_LANES))
        lanes = slice(part * group_w, (part + 1) * group_w)
        vt = vt_win[kv_head * HEAD_DIM:(kv_head + 1) * HEAD_DIM, :]
        o = jnp.dot(vt, probs[col][:, lanes], preferred_element_type=F32)
        o = o * inv_denoms[col][:, lanes]
        for pair in range(Q_PER_KV // 2):
            o_pair = jnp.concatenate([o[:, (2 * pair) * WINDOW:(2 * pair + 1) * WINDOW],
                                      o[:, (2 * pair + 1) * WINDOW:(2 * pair + 2) * WINDOW]], axis=0)
            out_lane = (kv_head * Q_PER_KV + 2 * pair) * HEAD_DIM
            o_ref[:, out_lane:out_lane + V7X_LANES] = o_pair.T.astype(o_ref.dtype)

    for col in range(KV_WIDTH // V7X_LANES):
        k_prev_ref[col] = k_cols[col]
    vt_prev_ref[...] = vt_cur


def _attention(proj, sinks, tables, gq_t, gk, layer, batch, seq):
    m = proj.shape[0]
    nb = seq // WINDOW
    q_blk = PROJ_QKV_COL // ATTN_WIDTH
    kv_blk = (PROJ_QKV_COL + ATTN_WIDTH) // (2 * KV_WIDTH)
    cos, sin_signed, cos_t, sin_t = tables
    half_dim = HEAD_DIM // 2
    row_table = pl.BlockSpec((WINDOW, V7X_LANES), lambda b, n: (n, 0))
    col_table = pl.BlockSpec((half_dim, WINDOW), lambda b, n: (0, n))
    return pl.pallas_call(
        _attn_kernel,
        grid=(batch, nb),
        in_specs=[pl.BlockSpec(memory_space=pltpu.SMEM),
                  pl.BlockSpec((WINDOW, ATTN_WIDTH), lambda b, n: (b * nb + n, q_blk)),
                  pl.BlockSpec((WINDOW, 2 * KV_WIDTH), lambda b, n: (b * nb + n, kv_blk)),
                  row_table, row_table, col_table, col_table,
                  pl.BlockSpec((None, HEAD_DIM, WINDOW), lambda b, n: (layer, 0, 0)),
                  pl.BlockSpec((None, 1, V7X_LANES), lambda b, n: (layer, 0, 0))],
        out_specs=pl.BlockSpec((WINDOW, ATTN_WIDTH), lambda b, n: (b * nb + n, 0)),
        out_shape=jax.ShapeDtypeStruct((m, ATTN_WIDTH), BF16),
        scratch_shapes=[pltpu.VMEM((KV_WIDTH // V7X_LANES, WINDOW, V7X_LANES), BF16),
                        pltpu.VMEM((KV_WIDTH, WINDOW), BF16)],
        compiler_params=_params(2),
        name="swa_attention",
    )(sinks, proj, proj, cos, sin_signed, cos_t, sin_t, gq_t, gk)


def _sgu_kernel(u0_ref, u1_ref, v0_ref, v1_ref, w_ref, bt_ref, g_ref, b_ref, wa_ref, wb_ref, wo_ref,
                o_ref, wa_bf_ref, wb_bf_ref, wo_bf_ref):
    for src, dst in ((wa_ref, wa_bf_ref), (wb_ref, wb_bf_ref), (wo_ref, wo_bf_ref)):
        dst[...] = src[...].astype(dst.dtype)

    r = lax.broadcasted_iota(jnp.int32, (CHUNK, CHUNK), 0)
    c = lax.broadcasted_iota(jnp.int32, (CHUNK, CHUNK), 1)
    causal = r >= c
    lane_mean = jnp.full((SGU_GROUP_DIM, SGU_GROUP_DIM), 1.0 / SGU_GROUP_DIM, BF16)
    groups_per_tile = TN // SGU_GROUP_DIM
    n_chunks = o_ref.shape[0] // CHUNK
    groups = range(SGU_GROUPS)

    def lanes(grp):
        return slice(grp * SGU_GROUP_DIM, (grp + 1) * SGU_GROUP_DIM)

    def tile_lanes(grp):
        return lanes(grp % groups_per_tile)

    vs = [(v0_ref, v1_ref)[grp // groups_per_tile][:, tile_lanes(grp)].astype(F32) for grp in groups]
    mus = [jnp.dot(v.astype(BF16), lane_mean, preferred_element_type=F32) for v in vs]
    dvs = [v - mu for v, mu in zip(vs, mus)]
    vars_ = [jnp.dot((dv * dv).astype(BF16), lane_mean, preferred_element_type=F32) for dv in dvs]
    ss = []
    for grp in groups:
        vn = (dvs[grp] * lax.rsqrt(vars_[grp] + EPS) * g_ref[:, lanes(grp)]
              + b_ref[:, lanes(grp)]).astype(BF16)
        w = jnp.where(causal, w_ref[grp], 0.0).astype(BF16)
        vn_chunks = [vn[ch * CHUNK:(ch + 1) * CHUNK, :] for ch in range(n_chunks)]
        ss.append(jnp.dot(w, jnp.concatenate(vn_chunks, axis=-1), preferred_element_type=F32))
    for grp in groups:
        s = ss[grp] + bt_ref[:, grp:grp + 1]
        u_ref = (u0_ref, u1_ref)[grp // groups_per_tile]
        for ch in range(n_chunks):
            u = u_ref[ch * CHUNK:(ch + 1) * CHUNK, tile_lanes(grp)].astype(F32)
            o_ref[ch * CHUNK:(ch + 1) * CHUNK, lanes(grp)] = (
                u * s[:, ch * CHUNK:(ch + 1) * CHUNK]).astype(o_ref.dtype)


def _sgu(proj, w_s, b_t, ln_g, ln_b, wa, wb, wo, layer, rows=4 * CHUNK):
    m = proj.shape[0]
    steps = m // rows
    u0 = PROJ_UV_TILE

    def tile(t):
        return pl.BlockSpec((rows, TN), lambda i: (i, t))

    def slab(w):
        return pl.BlockSpec((None, w.shape[1] // steps, w.shape[2]), lambda i: (layer, i, 0))

    def slab_out(w):
        return pl.BlockSpec((w.shape[1] // steps, w.shape[2]), lambda i: (i, 0))

    return pl.pallas_call(
        _sgu_kernel,
        grid=(steps,),
        in_specs=[tile(u0), tile(u0 + 1), tile(u0 + 2), tile(u0 + 3),
                  pl.BlockSpec((None, SGU_GROUPS, CHUNK, CHUNK), lambda i: (layer, 0, 0, 0)),
                  pl.BlockSpec((None, CHUNK, SGU_GROUPS), lambda i: (layer, 0, 0)),
                  pl.BlockSpec((None, 1, SGU_WIDTH), lambda i: (layer, 0, 0)),
                  pl.BlockSpec((None, 1, SGU_WIDTH), lambda i: (layer, 0, 0)),
                  slab(wa), slab(wb), slab(wo)],
        out_specs=[pl.BlockSpec((rows, SGU_WIDTH), lambda i: (i, 0)),
                   slab_out(wa), slab_out(wb), slab_out(wo)],
        out_shape=[jax.ShapeDtypeStruct((m, SGU_WIDTH), BF16)]
        + [jax.ShapeDtypeStruct(w.shape[1:], BF16) for w in (wa, wb, wo)],
        compiler_params=_params(1),
        name="sgu",
    )(proj, proj, proj, proj, w_s, b_t, ln_g, ln_b, wa, wb, wo)


def _rope_tables(seq):
    pos = jnp.arange(seq, dtype=F32)
    inv_freq = jnp.power(ROPE_THETA, -jnp.arange(0, HEAD_DIM, 2, dtype=F32) / HEAD_DIM)
    ang = pos[:, None] * inv_freq[None, :]
    cos, sin = jnp.cos(ang), jnp.sin(ang)
    reps = V7X_LANES // HEAD_DIM
    return (jnp.tile(cos, (1, 2 * reps)), jnp.tile(jnp.concatenate([-sin, sin], axis=-1), (1, reps)),
            cos.T, sin.T)


def kernel(x, mix_norm, w_in, q_norm, k_norm, sinks, sgu_ln_g, sgu_ln_b, w_spatial, b_spatial,
           w_attn_branch, w_sgu_branch, w_out, ffn_norm, w_gate, w_up, w_down):
    batch, seq, d = x.shape
    depth = w_in.shape[0]
    m = batch * seq
    assert d == D_MODEL and w_in.shape[-1] == (QKV_TILES + UV_TILES + GATE_TILES) * TN
    tables = _rope_tables(seq)
    gq_t = jnp.broadcast_to((q_norm * (HEAD_DIM ** -0.5 * LOG2_E))[:, :, None],
                            (depth, HEAD_DIM, WINDOW))
    gk = jnp.tile(k_norm, (1, V7X_LANES // HEAD_DIM)).reshape(depth, 1, V7X_LANES)
    b_t = jnp.swapaxes(b_spatial, 1, 2)
    mix_norm, ffn_norm, sgu_ln_g, sgu_ln_b = (
        p.reshape(depth, 1, -1) for p in (mix_norm, ffn_norm, sgu_ln_g, sgu_ln_b))

    xf = x.reshape(m, d)
    h = _rmsnorm(xf, mix_norm, 0)
    for l in range(depth):
        proj = _in_proj(h, w_in, l)
        a = _attention(proj, sinks[l], tables, gq_t, gk, l, batch, seq)
        b, wa_bf, wb_bf, wo_bf = _sgu(proj, w_spatial, b_t, sgu_ln_g, sgu_ln_b,
                                      w_attn_branch, w_sgu_branch, w_out, l)
        xf, h2 = _mix_out(a, b, proj, xf, wa_bf, wb_bf, wo_bf, ffn_norm, l)
        act, wd_bf = _swiglu(h2, w_gate, w_up, w_down, l)
        if l + 1 < depth:
            xf, h = _down_out(act, xf, wd_bf, mix_norm, l + 1)
        else:
            (xf,) = _down_out(act, xf, wd_bf, None, l)
    return xf.reshape(batch, seq, d)
```

```python
import functools

import jax
import jax.numpy as jnp
from jax import lax
from jax.experimental import pallas as pl
from jax.experimental.pallas import tpu as pltpu

F32 = jnp.float32
BF16 = jnp.bfloat16

HEAD_DIM = 64
N_Q_HEADS = 16
N_KV_HEADS = 4
Q_PER_KV = N_Q_HEADS // N_KV_HEADS
ATTN_WIDTH = N_Q_HEADS * HEAD_DIM
KV_WIDTH = N_KV_HEADS * HEAD_DIM
WINDOW = 128
ROPE_THETA = 10000.0
SGU_GROUPS = 8
SGU_GROUP_DIM = 128
SGU_WIDTH = SGU_GROUPS * SGU_GROUP_DIM
CHUNK = 128
EPS = 1e-6
MASK_VALUE = -1e30
LOG2_E = 1.4426950408889634

V7X_LANES = 128
V7X_VMEM_BYTES = 64 * 1024 * 1024
VMEM_LIMIT = V7X_VMEM_BYTES - 8 * 1024 * 1024

TN = 512
D_MODEL = 2048
QKV_TILES = (ATTN_WIDTH + 2 * KV_WIDTH) // TN
UV_TILES = 2 * SGU_WIDTH // TN
GATE_TILES = 2 * D_MODEL // TN
PROJ_QKV_COL = GATE_TILES * TN
PROJ_UV_TILE = GATE_TILES + QKV_TILES


def _params(n_axes):
    return pltpu.CompilerParams(dimension_semantics=("arbitrary",) * n_axes,
                                vmem_limit_bytes=VMEM_LIMIT)


def _rmsnorm_kernel(x_ref, g_ref, o_ref):
    x = x_ref[...]
    y = x * lax.rsqrt(jnp.mean(x * x, axis=-1, keepdims=True) + EPS)
    o_ref[...] = (y * g_ref[...]).astype(o_ref.dtype)


def _rmsnorm(x, gains, layer, tm=512):
    m, d = x.shape
    return pl.pallas_call(
        _rmsnorm_kernel,
        grid=(m // tm,),
        in_specs=[pl.BlockSpec((tm, d), lambda i: (i, 0)),
                  pl.BlockSpec((None, 1, d), lambda i: (layer, 0, 0))],
        out_specs=pl.BlockSpec((tm, d), lambda i: (i, 0)),
        out_shape=jax.ShapeDtypeStruct((m, d), BF16),
        compiler_params=_params(1),
        name="rmsnorm",
    )(x, gains)


SUB_ROWS = 512


def _row_blocks(o_ref):
    tm = o_ref.shape[0]
    sub = min(SUB_ROWS, tm // 2)
    return [pl.ds(r, sub) for r in range(0, tm, sub)]


def _cast_weights(w_refs, wbf_refs):
    @pl.when(pl.program_id(1) == 0)
    def _():
        for w_ref, wbf_ref in zip(w_refs, wbf_refs):
            wbf_ref[...] = w_ref[...].astype(BF16)


def _in_proj_kernel(h_ref, w_ref, o_ref, wbf_ref):
    _cast_weights([w_ref], [wbf_ref])
    j = pl.program_id(0)

    def emit(act):
        for rows in _row_blocks(o_ref):
            y = jnp.dot(h_ref[rows, :], wbf_ref[...], preferred_element_type=F32)
            o_ref[rows, :] = (y if act is None else act(y)).astype(o_ref.dtype)

    pl.when(j < QKV_TILES)(lambda: emit(None))
    pl.when((j >= QKV_TILES) & (j < QKV_TILES + UV_TILES))(
        lambda: emit(functools.partial(jax.nn.gelu, approximate=True)))
    pl.when(j >= QKV_TILES + UV_TILES)(lambda: emit(jax.nn.sigmoid))


def _in_proj(h, w, layer, tm=4096):
    m, k = h.shape
    n = w.shape[-1]
    return pl.pallas_call(
        _in_proj_kernel,
        grid=(n // TN, m // tm),
        in_specs=[pl.BlockSpec((tm, k), lambda j, i: (i, 0)),
                  pl.BlockSpec((None, k, TN), lambda j, i: (layer, 0, j))],
        out_specs=pl.BlockSpec(
            (tm, TN), lambda j, i: (i, jnp.where(j < QKV_TILES + UV_TILES, j + GATE_TILES,
                                                 j - (QKV_TILES + UV_TILES)))),
        out_shape=jax.ShapeDtypeStruct((m, n), BF16),
        scratch_shapes=[pltpu.VMEM((k, TN), BF16)],
        compiler_params=_params(2),
        name="in_proj",
    )(h, w)


def _swiglu_kernel(h_ref, wg_ref, wu_ref, wd_ref, o_ref, wd_bf_ref, wgbf_ref, wubf_ref):
    _cast_weights([wg_ref, wu_ref], [wgbf_ref, wubf_ref])
    wd_bf_ref[...] = wd_ref[...].astype(wd_bf_ref.dtype)
    for rows in _row_blocks(o_ref):
        h = h_ref[rows, :]
        g = jnp.dot(h, wgbf_ref[...], preferred_element_type=F32)
        u = jnp.dot(h, wubf_ref[...], preferred_element_type=F32)
        o_ref[rows, :] = (g * jax.nn.sigmoid(g) * u).astype(o_ref.dtype)


def _swiglu(h, wg, wu, wd, layer, tm=2048):
    m, k = h.shape
    f = wg.shape[-1]
    mt = m // tm
    slab = wd.shape[1] // (f // TN * mt)
    w_spec = pl.BlockSpec((None, k, TN), lambda j, i: (layer, 0, j))
    return pl.pallas_call(
        _swiglu_kernel,
        grid=(f // TN, mt),
        in_specs=[pl.BlockSpec((tm, k), lambda j, i: (i, 0)), w_spec, w_spec,
                  pl.BlockSpec((None, slab, wd.shape[2]), lambda j, i: (layer, j * mt + i, 0))],
        out_specs=[pl.BlockSpec((tm, TN), lambda j, i: (i, j)),
                   pl.BlockSpec((slab, wd.shape[2]), lambda j, i: (j * mt + i, 0))],
        out_shape=[jax.ShapeDtypeStruct((m, f), BF16), jax.ShapeDtypeStruct(wd.shape[1:], BF16)],
        scratch_shapes=[pltpu.VMEM((k, TN), BF16), pltpu.VMEM((k, TN), BF16)],
        compiler_params=_params(2),
        name="swiglu",
    )(h, wg, wu, wd)


def _mix_out_kernel(a_ref, b_ref, ga_ref, gb_ref, x_ref, wa_ref, wb_ref, wo_ref, gain_ref,
                    o_ref, h_ref, merged_ref):
    tm, d = o_ref.shape
    tiles = [slice(t * TN, (t + 1) * TN) for t in range(d // TN)]
    halves = [pl.ds(r, tm // 2) for r in (0, tm // 2)]
    for rows in halves:
        for cols in tiles:
            ya = jnp.dot(a_ref[rows, :], wa_ref[:, cols], preferred_element_type=F32)
            yb = jnp.dot(b_ref[rows, :], wb_ref[:, cols], preferred_element_type=F32)
            merged_ref[rows, cols] = (ga_ref[rows, cols] * ya + gb_ref[rows, cols] * yb
                                      ).astype(merged_ref.dtype)
    for rows in halves:
        ssq = jnp.zeros((tm // 2, 1), F32)
        for cols in tiles:
            y = x_ref[rows, cols] + jnp.dot(merged_ref[rows, :], wo_ref[:, cols],
                                            preferred_element_type=F32)
            o_ref[rows, cols] = y
            ssq = ssq + jnp.sum(y * y, axis=-1, keepdims=True)
        scale = lax.rsqrt(ssq * (1.0 / d) + EPS)
        h_ref[rows, :] = (o_ref[rows, :] * scale * gain_ref[...]).astype(h_ref.dtype)


def _mix_out(a, b, proj, x, wa_bf, wb_bf, wo_bf, gains, layer, tm=512):
    m, k = a.shape
    d = x.shape[1]
    row_tile = pl.BlockSpec((tm, d), lambda i: (i, 0))

    def whole(w):
        return pl.BlockSpec(w.shape, lambda i: (0, 0))

    return pl.pallas_call(
        _mix_out_kernel,
        grid=(m // tm,),
        in_specs=[pl.BlockSpec((tm, k), lambda i: (i, 0)),
                  pl.BlockSpec((tm, k), lambda i: (i, 0)),
                  pl.BlockSpec((tm, d), lambda i: (i, 0)),
                  pl.BlockSpec((tm, d), lambda i: (i, 1)),
                  row_tile, whole(wa_bf), whole(wb_bf), whole(wo_bf),
                  pl.BlockSpec((None, 1, d), lambda i: (layer, 0, 0))],
        out_specs=[row_tile, row_tile],
        out_shape=[jax.ShapeDtypeStruct((m, d), F32), jax.ShapeDtypeStruct((m, d), BF16)],
        scratch_shapes=[pltpu.VMEM((tm, d), BF16)],
        compiler_params=_params(1),
        name="mix_out",
    )(a, b, proj, proj, x, wa_bf, wb_bf, wo_bf, gains)


def _down_out_kernel(act_ref, x_ref, w_ref, *rest):
    gain_ref, o_ref, h_ref = rest if len(rest) == 3 else (None, rest[0], None)
    tm, d = o_ref.shape
    tiles = [slice(t * TN, (t + 1) * TN) for t in range(d // TN)]
    for rows in (pl.ds(r, tm // 2) for r in (0, tm // 2)):
        ssq = jnp.zeros((tm // 2, 1), F32)
        for cols in tiles:
            y = x_ref[rows, cols] + jnp.dot(act_ref[rows, :], w_ref[:, cols],
                                            preferred_element_type=F32)
            o_ref[rows, cols] = y
            if h_ref is not None:
                ssq = ssq + jnp.sum(y * y, axis=-1, keepdims=True)
        if h_ref is not None:
            scale = lax.rsqrt(ssq * (1.0 / d) + EPS)
            h_ref[rows, :] = (o_ref[rows, :] * scale * gain_ref[...]).astype(h_ref.dtype)


def _down_out(act, x, w_bf, gains, layer, tm=512):
    m, k = act.shape
    d = x.shape[1]
    row_tile = pl.BlockSpec((tm, d), lambda i: (i, 0))
    in_specs = [pl.BlockSpec((tm, k), lambda i: (i, 0)), row_tile,
                pl.BlockSpec(w_bf.shape, lambda i: (0, 0))]
    operands = [act, x, w_bf]
    out_specs, out_shape = [row_tile], [jax.ShapeDtypeStruct((m, d), F32)]
    if gains is not None:
        in_specs.append(pl.BlockSpec((None, 1, d), lambda i: (layer, 0, 0)))
        operands.append(gains)
        out_specs.append(row_tile)
        out_shape.append(jax.ShapeDtypeStruct((m, d), BF16))
    return pl.pallas_call(
        _down_out_kernel,
        grid=(m // tm,),
        in_specs=in_specs,
        out_specs=out_specs,
        out_shape=out_shape,
        compiler_params=_params(1),
        name="down_out" if gains is not None else "down_out_last",
    )(*operands)


def _attn_kernel(sinks_ref, q_ref, kv_ref, cos_ref, sin_ref, cos_t_ref, sin_t_ref, gq_t_ref, gk_ref,
                 wa_ref, wb_ref, wo_ref, o_ref, wa_bf_ref, wb_bf_ref, wo_bf_ref,
                 k_prev_ref, vt_prev_ref):
    n = pl.program_id(1)
    half_dim = HEAD_DIM // 2

    for src, dst in ((wa_ref, wa_bf_ref), (wb_ref, wb_bf_ref), (wo_ref, wo_bf_ref)):
        dst[...] = src[...].astype(dst.dtype)

    @pl.when(n == 0)
    def _():
        k_prev_ref[...] = jnp.zeros_like(k_prev_ref)
        vt_prev_ref[...] = jnp.zeros_like(vt_prev_ref)

    lane = lax.broadcasted_iota(jnp.int32, (1, V7X_LANES), 1)
    even_quarter = ((lane // half_dim) % 2) == 0
    r = lax.broadcasted_iota(jnp.int32, (V7X_LANES, V7X_LANES), 0)
    c = lax.broadcasted_iota(jnp.int32, (V7X_LANES, V7X_LANES), 1)
    head_mean = jnp.where((r // HEAD_DIM) == (c // HEAD_DIM), 1.0 / HEAD_DIM, 0.0).astype(BF16)
    cos, sin = cos_ref[...], sin_ref[...]
    k_cols = []
    for col in range(KV_WIDTH // V7X_LANES):
        x = kv_ref[:, col * V7X_LANES:(col + 1) * V7X_LANES].astype(F32)
        ms = jnp.dot((x * x).astype(BF16), head_mean, preferred_element_type=F32)
        y = x * lax.rsqrt(ms + EPS) * gk_ref[...]
        partner = jnp.where(even_quarter,
                            pltpu.roll(y, V7X_LANES - half_dim, 1),
                            pltpu.roll(y, half_dim, 1))
        k_cols.append((y * cos + partner * sin).astype(BF16))
    k_win = [jnp.concatenate([k_prev_ref[col], k_cols[col]], axis=0)
             for col in range(KV_WIDTH // V7X_LANES)]
    vt_cur = kv_ref[:, KV_WIDTH:].astype(F32).T.astype(BF16)
    vt_win = jnp.concatenate([vt_prev_ref[...], vt_cur], axis=1)

    key = lax.broadcasted_iota(jnp.int32, (2 * WINDOW, WINDOW), 0)
    qry = lax.broadcasted_iota(jnp.int32, (2 * WINDOW, WINDOW), 1)
    diff = qry + WINDOW - key
    allowed = (diff >= 0) & (diff < WINDOW) & ((key >= WINDOW) | (n > 0))

    cos_t, sin_t, gq_t = cos_t_ref[...], sin_t_ref[...], gq_t_ref[...]
    zeros = jnp.zeros((HEAD_DIM, WINDOW), BF16)

    qt_pads = []
    for pair in range(N_Q_HEADS // 2):
        sl = slice(pair * V7X_LANES, (pair + 1) * V7X_LANES)
        qt_pair = q_ref[:, sl].astype(F32).T
        for half in range(2):
            kv_head = (2 * pair + half) // Q_PER_KV
            x = qt_pair[half * HEAD_DIM:(half + 1) * HEAD_DIM, :]
            y = x * lax.rsqrt(jnp.mean(x * x, axis=0, keepdims=True) + EPS) * gq_t
            y1, y2 = y[:half_dim], y[half_dim:]
            qt = jnp.concatenate([y1 * cos_t - y2 * sin_t, y2 * cos_t + y1 * sin_t],
                                 axis=0).astype(BF16)
            qt_pads.append(jnp.concatenate([zeros, qt] if kv_head % 2 else [qt, zeros], axis=0))

    heads_per_col = N_Q_HEADS // (KV_WIDTH // V7X_LANES)
    sink_row = [jnp.full((1, WINDOW), sinks_ref[h] * LOG2_E, F32) for h in range(N_Q_HEADS)]
    probs, inv_denoms = [], []
    for col in range(KV_WIDTH // V7X_LANES):
        heads = range(col * heads_per_col, (col + 1) * heads_per_col)
        s = jnp.dot(k_win[col], jnp.concatenate([qt_pads[h] for h in heads], axis=1),
                    preferred_element_type=F32)
        s = jnp.where(jnp.concatenate([allowed] * heads_per_col, axis=1), s, MASK_VALUE)
        sink = jnp.concatenate([sink_row[h] for h in heads], axis=1)
        m = jnp.maximum(jnp.max(s, axis=0, keepdims=True), sink)
        p = jnp.exp2(s - m)
        inv_denoms.append(1.0 / (jnp.sum(p, axis=0, keepdims=True) + jnp.exp2(sink - m)))
        probs.append(p.astype(BF16))

    group_w = Q_PER_KV * WINDOW
    for kv_head in range(N_KV_HEADS):
        col, part = divmod(kv_head, N_KV_HEADS // (KV_WIDTH // V7X_LANES))
        lanes = slice(part * group_w, (part + 1) * group_w)
        vt = vt_win[kv_head * HEAD_DIM:(kv_head + 1) * HEAD_DIM, :]
        o = jnp.dot(vt, probs[col][:, lanes], preferred_element_type=F32)
        o = o * inv_denoms[col][:, lanes]
        for pair in range(Q_PER_KV // 2):
            o_pair = jnp.concatenate([o[:, (2 * pair) * WINDOW:(2 * pair + 1) * WINDOW],
                                      o[:, (2 * pair + 1) * WINDOW:(2 * pair + 2) * WINDOW]], axis=0)
            out_lane = (kv_head * Q_PER_KV + 2 * pair) * HEAD_DIM
            o_ref[:, out_lane:out_lane + V7X_LANES] = o_pair.T.astype(o_ref.dtype)

    for col in range(KV_WIDTH // V7X_LANES):
        k_prev_ref[col] = k_cols[col]
    vt_prev_ref[...] = vt_cur


def _attention(proj, sinks, tables, gq_t, gk, wa, wb, wo, layer, batch, seq):
    m = proj.shape[0]
    nb = seq // WINDOW
    steps = batch * nb

    def slab(w):
        return pl.BlockSpec((None, w.shape[1] // steps, w.shape[2]),
                            lambda b, n: (layer, b * nb + n, 0))

    def slab_out(w):
        return pl.BlockSpec((w.shape[1] // steps, w.shape[2]), lambda b, n: (b * nb + n, 0))

    q_blk = PROJ_QKV_COL // ATTN_WIDTH
    kv_blk = (PROJ_QKV_COL + ATTN_WIDTH) // (2 * KV_WIDTH)
    cos, sin_signed, cos_t, sin_t = tables
    half_dim = HEAD_DIM // 2
    row_table = pl.BlockSpec((WINDOW, V7X_LANES), lambda b, n: (n, 0))
    col_table = pl.BlockSpec((half_dim, WINDOW), lambda b, n: (0, n))
    return pl.pallas_call(
        _attn_kernel,
        grid=(batch, nb),
        in_specs=[pl.BlockSpec(memory_space=pltpu.SMEM),
                  pl.BlockSpec((WINDOW, ATTN_WIDTH), lambda b, n: (b * nb + n, q_blk)),
                  pl.BlockSpec((WINDOW, 2 * KV_WIDTH), lambda b, n: (b * nb + n, kv_blk)),
                  row_table, row_table, col_table, col_table,
                  pl.BlockSpec((None, HEAD_DIM, WINDOW), lambda b, n: (layer, 0, 0)),
                  pl.BlockSpec((None, 1, V7X_LANES), lambda b, n: (layer, 0, 0)),
                  slab(wa), slab(wb), slab(wo)],
        out_specs=[pl.BlockSpec((WINDOW, ATTN_WIDTH), lambda b, n: (b * nb + n, 0)),
                   slab_out(wa), slab_out(wb), slab_out(wo)],
        out_shape=[jax.ShapeDtypeStruct((m, ATTN_WIDTH), BF16)]
        + [jax.ShapeDtypeStruct(w.shape[1:], BF16) for w in (wa, wb, wo)],
        scratch_shapes=[pltpu.VMEM((KV_WIDTH // V7X_LANES, WINDOW, V7X_LANES), BF16),
                        pltpu.VMEM((KV_WIDTH, WINDOW), BF16)],
        compiler_params=_params(2),
        name="swa_attention",
    )(sinks, proj, proj, cos, sin_signed, cos_t, sin_t, gq_t, gk, wa, wb, wo)


def _sgu_kernel(u0_ref, u1_ref, v0_ref, v1_ref, w_ref, bt_ref, g_ref, b_ref, o_ref):
    r = lax.broadcasted_iota(jnp.int32, (CHUNK, CHUNK), 0)
    c = lax.broadcasted_iota(jnp.int32, (CHUNK, CHUNK), 1)
    causal = r >= c
    lane_mean = jnp.full((SGU_GROUP_DIM, SGU_GROUP_DIM), 1.0 / SGU_GROUP_DIM, BF16)
    groups_per_tile = TN // SGU_GROUP_DIM
    n_chunks = o_ref.shape[0] // CHUNK
    groups = range(SGU_GROUPS)

    def lanes(grp):
        return slice(grp * SGU_GROUP_DIM, (grp + 1) * SGU_GROUP_DIM)

    def tile_lanes(grp):
        return lanes(grp % groups_per_tile)

    vs = [(v0_ref, v1_ref)[grp // groups_per_tile][:, tile_lanes(grp)].astype(F32) for grp in groups]
    mus = [jnp.dot(v.astype(BF16), lane_mean, preferred_element_type=F32) for v in vs]
    dvs = [v - mu for v, mu in zip(vs, mus)]
    vars_ = [jnp.dot((dv * dv).astype(BF16), lane_mean, preferred_element_type=F32) for dv in dvs]
    ss = []
    for grp in groups:
        vn = (dvs[grp] * lax.rsqrt(vars_[grp] + EPS) * g_ref[:, lanes(grp)]
              + b_ref[:, lanes(grp)]).astype(BF16)
        w = jnp.where(causal, w_ref[grp], 0.0).astype(BF16)
        vn_chunks = [vn[ch * CHUNK:(ch + 1) * CHUNK, :] for ch in range(n_chunks)]
        ss.append(jnp.dot(w, jnp.concatenate(vn_chunks, axis=-1), preferred_element_type=F32))
    for grp in groups:
        s = ss[grp] + bt_ref[:, grp:grp + 1]
        u_ref = (u0_ref, u1_ref)[grp // groups_per_tile]
        for ch in range(n_chunks):
            u = u_ref[ch * CHUNK:(ch + 1) * CHUNK, tile_lanes(grp)].astype(F32)
            o_ref[ch * CHUNK:(ch + 1) * CHUNK, lanes(grp)] = (
                u * s[:, ch * CHUNK:(ch + 1) * CHUNK]).astype(o_ref.dtype)


def _sgu(proj, w_s, b_t, ln_g, ln_b, layer, rows=4 * CHUNK):
    m = proj.shape[0]
    u0 = PROJ_UV_TILE

    def tile(t):
        return pl.BlockSpec((rows, TN), lambda i: (i, t))

    return pl.pallas_call(
        _sgu_kernel,
        grid=(m // rows,),
        in_specs=[tile(u0), tile(u0 + 1), tile(u0 + 2), tile(u0 + 3),
                  pl.BlockSpec((None, SGU_GROUPS, CHUNK, CHUNK), lambda i: (layer, 0, 0, 0)),
                  pl.BlockSpec((None, CHUNK, SGU_GROUPS), lambda i: (layer, 0, 0)),
                  pl.BlockSpec((None, 1, SGU_WIDTH), lambda i: (layer, 0, 0)),
                  pl.BlockSpec((None, 1, SGU_WIDTH), lambda i: (layer, 0, 0))],
        out_specs=pl.BlockSpec((rows, SGU_WIDTH), lambda i: (i, 0)),
        out_shape=jax.ShapeDtypeStruct((m, SGU_WIDTH), BF16),
        compiler_params=_params(1),
        name="sgu",
    )(proj, proj, proj, proj, w_s, b_t, ln_g, ln_b)


def _rope_tables(seq):
    pos = jnp.arange(seq, dtype=F32)
    inv_freq = jnp.power(ROPE_THETA, -jnp.arange(0, HEAD_DIM, 2, dtype=F32) / HEAD_DIM)
    ang = pos[:, None] * inv_freq[None, :]
    cos, sin = jnp.cos(ang), jnp.sin(ang)
    reps = V7X_LANES // HEAD_DIM
    return (jnp.tile(cos, (1, 2 * reps)), jnp.tile(jnp.concatenate([-sin, sin], axis=-1), (1, reps)),
            cos.T, sin.T)


def kernel(x, mix_norm, w_in, q_norm, k_norm, sinks, sgu_ln_g, sgu_ln_b, w_spatial, b_spatial,
           w_attn_branch, w_sgu_branch, w_out, ffn_norm, w_gate, w_up, w_down):
    batch, seq, d = x.shape
    depth = w_in.shape[0]
    m = batch * seq
    assert d == D_MODEL and w_in.shape[-1] == (QKV_TILES + UV_TILES + GATE_TILES) * TN
    tables = _rope_tables(seq)
    gq_t = jnp.broadcast_to((q_norm * (HEAD_DIM ** -0.5 * LOG2_E))[:, :, None],
                            (depth, HEAD_DIM, WINDOW))
    gk = jnp.tile(k_norm, (1, V7X_LANES // HEAD_DIM)).reshape(depth, 1, V7X_LANES)
    b_t = jnp.swapaxes(b_spatial, 1, 2)
    mix_norm, ffn_norm, sgu_ln_g, sgu_ln_b = (
        p.reshape(depth, 1, -1) for p in (mix_norm, ffn_norm, sgu_ln_g, sgu_ln_b))

    xf = x.reshape(m, d)
    h = _rmsnorm(xf, mix_norm, 0)
    for l in range(depth):
        proj = _in_proj(h, w_in, l)
        a, wa_bf, wb_bf, wo_bf = _attention(proj, sinks[l], tables, gq_t, gk,
                                            w_attn_branch, w_sgu_branch, w_out, l, batch, seq)
        b = _sgu(proj, w_spatial, b_t, sgu_ln_g, sgu_ln_b, l)
        xf, h2 = _mix_out(a, b, proj, xf, wa_bf, wb_bf, wo_bf, ffn_norm, l)
        act, wd_bf = _swiglu(h2, w_gate, w_up, w_down, l)
        if l + 1 < depth:
            xf, h = _down_out(act, xf, wd_bf, mix_norm, l + 1)
        else:
            (xf,) = _down_out(act, xf, wd_bf, None, l)
    return xf.reshape(batch, seq, d)
```

```python
import functools

import jax
import jax.numpy as jnp
from jax import lax
from jax.experimental import pallas as pl
from jax.experimental.pallas import tpu as pltpu

F32 = jnp.float32
BF16 = jnp.bfloat16

HEAD_DIM = 64
N_Q_HEADS = 16
N_KV_HEADS = 4
Q_PER_KV = N_Q_HEADS // N_KV_HEADS
ATTN_WIDTH = N_Q_HEADS * HEAD_DIM
KV_WIDTH = N_KV_HEADS * HEAD_DIM
WINDOW = 128
ROPE_THETA = 10000.0
SGU_GROUPS = 8
SGU_GROUP_DIM = 128
SGU_WIDTH = SGU_GROUPS * SGU_GROUP_DIM
CHUNK = 128
EPS = 1e-6
MASK_VALUE = -1e30
LOG2_E = 1.4426950408889634

V7X_LANES = 128
V7X_VMEM_BYTES = 64 * 1024 * 1024
VMEM_LIMIT = V7X_VMEM_BYTES - 8 * 1024 * 1024

TN = 512
D_MODEL = 2048
QKV_TILES = (ATTN_WIDTH + 2 * KV_WIDTH) // TN
UV_TILES = 2 * SGU_WIDTH // TN
GATE_TILES = 2 * D_MODEL // TN
PROJ_QKV_COL = GATE_TILES * TN
PROJ_UV_TILE = GATE_TILES + QKV_TILES


def _params(n_axes):
    return pltpu.CompilerParams(dimension_semantics=("arbitrary",) * n_axes,
                                vmem_limit_bytes=VMEM_LIMIT)


def _rmsnorm_kernel(x_ref, g_ref, o_ref):
    x = x_ref[...]
    y = x * lax.rsqrt(jnp.mean(x * x, axis=-1, keepdims=True) + EPS)
    o_ref[...] = (y * g_ref[...]).astype(o_ref.dtype)


def _rmsnorm(x, gains, layer, tm=512):
    m, d = x.shape
    return pl.pallas_call(
        _rmsnorm_kernel,
        grid=(m // tm,),
        in_specs=[pl.BlockSpec((tm, d), lambda i: (i, 0)),
                  pl.BlockSpec((None, 1, d), lambda i: (layer, 0, 0))],
        out_specs=pl.BlockSpec((tm, d), lambda i: (i, 0)),
        out_shape=jax.ShapeDtypeStruct((m, d), BF16),
        compiler_params=_params(1),
        name="rmsnorm",
    )(x, gains)


SUB_ROWS = 512


def _row_blocks(o_ref):
    tm = o_ref.shape[0]
    sub = min(SUB_ROWS, tm // 2)
    return [pl.ds(r, sub) for r in range(0, tm, sub)]


def _cast_weights(w_refs, wbf_refs):
    @pl.when(pl.program_id(1) == 0)
    def _():
        for w_ref, wbf_ref in zip(w_refs, wbf_refs):
            wbf_ref[...] = w_ref[...].astype(BF16)


def _in_proj_kernel(h_ref, w_ref, o_ref, wbf_ref):
    _cast_weights([w_ref], [wbf_ref])
    j = pl.program_id(0)

    def emit(act):
        for rows in _row_blocks(o_ref):
            y = jnp.dot(h_ref[rows, :], wbf_ref[...], preferred_element_type=F32)
            o_ref[rows, :] = (y if act is None else act(y)).astype(o_ref.dtype)

    pl.when(j < QKV_TILES)(lambda: emit(None))
    pl.when((j >= QKV_TILES) & (j < QKV_TILES + UV_TILES))(
        lambda: emit(functools.partial(jax.nn.gelu, approximate=True)))
    pl.when(j >= QKV_TILES + UV_TILES)(lambda: emit(jax.nn.sigmoid))


def _in_proj(h, w, layer, tm=4096):
    m, k = h.shape
    n = w.shape[-1]
    return pl.pallas_call(
        _in_proj_kernel,
        grid=(n // TN, m // tm),
        in_specs=[pl.BlockSpec((tm, k), lambda j, i: (i, 0)),
                  pl.BlockSpec((None, k, TN), lambda j, i: (layer, 0, j))],
        out_specs=pl.BlockSpec(
            (tm, TN), lambda j, i: (i, jnp.where(j < QKV_TILES + UV_TILES, j + GATE_TILES,
                                                 j - (QKV_TILES + UV_TILES)))),
        out_shape=jax.ShapeDtypeStruct((m, n), BF16),
        scratch_shapes=[pltpu.VMEM((k, TN), BF16)],
        compiler_params=_params(2),
        name="in_proj",
    )(h, w)


def _swiglu_kernel(h_ref, wg_ref, wu_ref, wd_ref, o_ref, wd_bf_ref, wgbf_ref, wubf_ref):
    _cast_weights([wg_ref, wu_ref], [wgbf_ref, wubf_ref])
    wd_bf_ref[...] = wd_ref[...].astype(wd_bf_ref.dtype)
    for rows in _row_blocks(o_ref):
        h = h_ref[rows, :]
        g = jnp.dot(h, wgbf_ref[...], preferred_element_type=F32)
        u = jnp.dot(h, wubf_ref[...], preferred_element_type=F32)
        o_ref[rows, :] = (g * jax.nn.sigmoid(g) * u).astype(o_ref.dtype)


def _swiglu(h, wg, wu, wd, layer, tm=2048):
    m, k = h.shape
    f = wg.shape[-1]
    mt = m // tm
    slab = wd.shape[1] // (f // TN * mt)
    w_spec = pl.BlockSpec((None, k, TN), lambda j, i: (layer, 0, j))
    return pl.pallas_call(
        _swiglu_kernel,
        grid=(f // TN, mt),
        in_specs=[pl.BlockSpec((tm, k), lambda j, i: (i, 0)), w_spec, w_spec,
                  pl.BlockSpec((None, slab, wd.shape[2]), lambda j, i: (layer, j * mt + i, 0))],
        out_specs=[pl.BlockSpec((tm, TN), lambda j, i: (i, j)),
                   pl.BlockSpec((slab, wd.shape[2]), lambda j, i: (j * mt + i, 0))],
        out_shape=[jax.ShapeDtypeStruct((m, f), BF16), jax.ShapeDtypeStruct(wd.shape[1:], BF16)],
        scratch_shapes=[pltpu.VMEM((k, TN), BF16), pltpu.VMEM((k, TN), BF16)],
        compiler_params=_params(2),
        name="swiglu",
    )(h, wg, wu, wd)


def _mix_out_kernel(a_ref, b_ref, ga_ref, gb_ref, x_ref, wa_ref, wb_ref, wo_ref, gain_ref,
                    o_ref, h_ref, merged_ref):
    tm, d = o_ref.shape
    tiles = [slice(t * TN, (t + 1) * TN) for t in range(d // TN)]
    halves = [pl.ds(r, tm // 2) for r in (0, tm // 2)]
    for rows in halves:
        for cols in tiles:
            ya = jnp.dot(a_ref[rows, :], wa_ref[:, cols], preferred_element_type=F32)
            yb = jnp.dot(b_ref[rows, :], wb_ref[:, cols], preferred_element_type=F32)
            merged_ref[rows, cols] = (ga_ref[rows, cols] * ya + gb_ref[rows, cols] * yb
                                      ).astype(merged_ref.dtype)
    for rows in halves:
        ssq = jnp.zeros((tm // 2, 1), F32)
        for cols in tiles:
            y = x_ref[rows, cols] + jnp.dot(merged_ref[rows, :], wo_ref[:, cols],
                                            preferred_element_type=F32)
            o_ref[rows, cols] = y
            ssq = ssq + jnp.sum(y * y, axis=-1, keepdims=True)
        scale = lax.rsqrt(ssq * (1.0 / d) + EPS)
        h_ref[rows, :] = (o_ref[rows, :] * scale * gain_ref[...]).astype(h_ref.dtype)


def _mix_out(a, b, proj, x, wa_bf, wb_bf, wo_bf, gains, layer, tm=512):
    m, k = a.shape
    d = x.shape[1]
    row_tile = pl.BlockSpec((tm, d), lambda i: (i, 0))

    def whole(w):
        return pl.BlockSpec(w.shape, lambda i: (0, 0))

    return pl.pallas_call(
        _mix_out_kernel,
        grid=(m // tm,),
        in_specs=[pl.BlockSpec((tm, k), lambda i: (i, 0)),
                  pl.BlockSpec((tm, k), lambda i: (i, 0)),
                  pl.BlockSpec((tm, d), lambda i: (i, 0)),
                  pl.BlockSpec((tm, d), lambda i: (i, 1)),
                  row_tile, whole(wa_bf), whole(wb_bf), whole(wo_bf),
                  pl.BlockSpec((None, 1, d), lambda i: (layer, 0, 0))],
        out_specs=[row_tile, row_tile],
        out_shape=[jax.ShapeDtypeStruct((m, d), F32), jax.ShapeDtypeStruct((m, d), BF16)],
        scratch_shapes=[pltpu.VMEM((tm, d), BF16)],
        compiler_params=_params(1),
        name="mix_out",
    )(a, b, proj, proj, x, wa_bf, wb_bf, wo_bf, gains)


def _down_out_kernel(act_ref, x_ref, w_ref, *rest):
    gain_ref, o_ref, h_ref = rest if len(rest) == 3 else (None, rest[0], None)
    tm, d = o_ref.shape
    tiles = [slice(t * TN, (t + 1) * TN) for t in range(d // TN)]
    for rows in (pl.ds(r, tm // 2) for r in (0, tm // 2)):
        ssq = jnp.zeros((tm // 2, 1), F32)
        for cols in tiles:
            y = x_ref[rows, cols] + jnp.dot(act_ref[rows, :], w_ref[:, cols],
                                            preferred_element_type=F32)
            o_ref[rows, cols] = y
            if h_ref is not None:
                ssq = ssq + jnp.sum(y * y, axis=-1, keepdims=True)
        if h_ref is not None:
            scale = lax.rsqrt(ssq * (1.0 / d) + EPS)
            h_ref[rows, :] = (o_ref[rows, :] * scale * gain_ref[...]).astype(h_ref.dtype)


def _down_out(act, x, w_bf, gains, layer, tm=512):
    m, k = act.shape
    d = x.shape[1]
    row_tile = pl.BlockSpec((tm, d), lambda i: (i, 0))
    in_specs = [pl.BlockSpec((tm, k), lambda i: (i, 0)), row_tile,
                pl.BlockSpec(w_bf.shape, lambda i: (0, 0))]
    operands = [act, x, w_bf]
    out_specs, out_shape = [row_tile], [jax.ShapeDtypeStruct((m, d), F32)]
    if gains is not None:
        in_specs.append(pl.BlockSpec((None, 1, d), lambda i: (layer, 0, 0)))
        operands.append(gains)
        out_specs.append(row_tile)
        out_shape.append(jax.ShapeDtypeStruct((m, d), BF16))
    return pl.pallas_call(
        _down_out_kernel,
        grid=(m // tm,),
        in_specs=in_specs,
        out_specs=out_specs,
        out_shape=out_shape,
        compiler_params=_params(1),
        name="down_out" if gains is not None else "down_out_last",
    )(*operands)


ATTN_BLOCKS_PER_STEP = 4


def _attn_kernel(sinks_ref, q_ref, kv_ref, cos_ref, sin_ref, cos_t_ref, sin_t_ref, gq_t_ref, gk_ref,
                 wa_ref, wb_ref, wo_ref, o_ref, wa_bf_ref, wb_bf_ref, wo_bf_ref,
                 k_prev_ref, vt_prev_ref):
    n = pl.program_id(1)
    half_dim = HEAD_DIM // 2

    for src, dst in ((wa_ref, wa_bf_ref), (wb_ref, wb_bf_ref), (wo_ref, wo_bf_ref)):
        dst[...] = src[...].astype(dst.dtype)

    @pl.when(n == 0)
    def _():
        k_prev_ref[...] = jnp.zeros_like(k_prev_ref)
        vt_prev_ref[...] = jnp.zeros_like(vt_prev_ref)

    lane = lax.broadcasted_iota(jnp.int32, (1, V7X_LANES), 1)
    even_quarter = ((lane // half_dim) % 2) == 0
    r = lax.broadcasted_iota(jnp.int32, (V7X_LANES, V7X_LANES), 0)
    c = lax.broadcasted_iota(jnp.int32, (V7X_LANES, V7X_LANES), 1)
    head_mean = jnp.where((r // HEAD_DIM) == (c // HEAD_DIM), 1.0 / HEAD_DIM, 0.0).astype(BF16)
    key = lax.broadcasted_iota(jnp.int32, (2 * WINDOW, WINDOW), 0)
    qry = lax.broadcasted_iota(jnp.int32, (2 * WINDOW, WINDOW), 1)
    diff = qry + WINDOW - key
    band = (diff >= 0) & (diff < WINDOW)
    gq_t = gq_t_ref[...]
    zeros = jnp.zeros((HEAD_DIM, WINDOW), BF16)
    n_cols = KV_WIDTH // V7X_LANES
    heads_per_col = N_Q_HEADS // n_cols
    group_w = Q_PER_KV * WINDOW
    sink_row = [jnp.full((1, WINDOW), sinks_ref[h] * LOG2_E, F32) for h in range(N_Q_HEADS)]

    def one_block(rows, k_prev, vt_prev, allowed):
        cos, sin = cos_ref[rows, :], sin_ref[rows, :]
        k_cols = []
        for col in range(n_cols):
            x = kv_ref[rows, col * V7X_LANES:(col + 1) * V7X_LANES].astype(F32)
            ms = jnp.dot((x * x).astype(BF16), head_mean, preferred_element_type=F32)
            y = x * lax.rsqrt(ms + EPS) * gk_ref[...]
            partner = jnp.where(even_quarter,
                                pltpu.roll(y, V7X_LANES - half_dim, 1),
                                pltpu.roll(y, half_dim, 1))
            k_cols.append((y * cos + partner * sin).astype(BF16))
        k_win = [jnp.concatenate([k_prev[col], k_cols[col]], axis=0) for col in range(n_cols)]
        vt_cur = kv_ref[rows, KV_WIDTH:].astype(F32).T.astype(BF16)
        vt_win = jnp.concatenate([vt_prev, vt_cur], axis=1)

        cos_t, sin_t = cos_t_ref[:, rows], sin_t_ref[:, rows]
        qt_pads = []
        for pair in range(N_Q_HEADS // 2):
            sl = slice(pair * V7X_LANES, (pair + 1) * V7X_LANES)
            qt_pair = q_ref[rows, sl].astype(F32).T
            for half in range(2):
                kv_head = (2 * pair + half) // Q_PER_KV
                x = qt_pair[half * HEAD_DIM:(half + 1) * HEAD_DIM, :]
                y = x * lax.rsqrt(jnp.mean(x * x, axis=0, keepdims=True) + EPS) * gq_t
                y1, y2 = y[:half_dim], y[half_dim:]
                qt = jnp.concatenate([y1 * cos_t - y2 * sin_t, y2 * cos_t + y1 * sin_t],
                                     axis=0).astype(BF16)
                qt_pads.append(jnp.concatenate([zeros, qt] if kv_head % 2 else [qt, zeros], axis=0))

        probs, inv_denoms = [], []
        for col in range(n_cols):
            heads = range(col * heads_per_col, (col + 1) * heads_per_col)
            s = jnp.dot(k_win[col], jnp.concatenate([qt_pads[h] for h in heads], axis=1),
                        preferred_element_type=F32)
            s = jnp.where(jnp.concatenate([allowed] * heads_per_col, axis=1), s, MASK_VALUE)
            sink = jnp.concatenate([sink_row[h] for h in heads], axis=1)
            m = jnp.maximum(jnp.max(s, axis=0, keepdims=True), sink)
            p = jnp.exp2(s - m)
            inv_denoms.append(1.0 / (jnp.sum(p, axis=0, keepdims=True) + jnp.exp2(sink - m)))
            probs.append(p.astype(BF16))

        for kv_head in range(N_KV_HEADS):
            col, part = divmod(kv_head, N_KV_HEADS // n_cols)
            lanes = slice(part * group_w, (part + 1) * group_w)
            vt = vt_win[kv_head * HEAD_DIM:(kv_head + 1) * HEAD_DIM, :]
            o = jnp.dot(vt, probs[col][:, lanes], preferred_element_type=F32)
            o = o * inv_denoms[col][:, lanes]
            for pair in range(Q_PER_KV // 2):
                o_pair = jnp.concatenate([o[:, (2 * pair) * WINDOW:(2 * pair + 1) * WINDOW],
                                          o[:, (2 * pair + 1) * WINDOW:(2 * pair + 2) * WINDOW]], axis=0)
                out_lane = (kv_head * Q_PER_KV + 2 * pair) * HEAD_DIM
                o_ref[rows, out_lane:out_lane + V7X_LANES] = o_pair.T.astype(o_ref.dtype)
        return k_cols, vt_cur

    k_prev = [k_prev_ref[col] for col in range(n_cols)]
    vt_prev = vt_prev_ref[...]
    for blk in range(o_ref.shape[0] // WINDOW):
        allowed = band & ((key >= WINDOW) | (n > 0)) if blk == 0 else band
        k_prev, vt_prev = one_block(pl.ds(blk * WINDOW, WINDOW), k_prev, vt_prev, allowed)

    for col in range(n_cols):
        k_prev_ref[col] = k_prev[col]
    vt_prev_ref[...] = vt_prev


def _attention(proj, sinks, tables, gq_t, gk, wa, wb, wo, layer, batch, seq):
    m = proj.shape[0]
    rows = ATTN_BLOCKS_PER_STEP * WINDOW
    nb = seq // rows
    steps = batch * nb

    def slab(w):
        return pl.BlockSpec((None, w.shape[1] // steps, w.shape[2]),
                            lambda b, n: (layer, b * nb + n, 0))

    def slab_out(w):
        return pl.BlockSpec((w.shape[1] // steps, w.shape[2]), lambda b, n: (b * nb + n, 0))

    q_blk = PROJ_QKV_COL // ATTN_WIDTH
    kv_blk = (PROJ_QKV_COL + ATTN_WIDTH) // (2 * KV_WIDTH)
    cos, sin_signed, cos_t, sin_t = tables
    half_dim = HEAD_DIM // 2
    row_table = pl.BlockSpec((rows, V7X_LANES), lambda b, n: (n, 0))
    col_table = pl.BlockSpec((half_dim, rows), lambda b, n: (0, n))
    return pl.pallas_call(
        _attn_kernel,
        grid=(batch, nb),
        in_specs=[pl.BlockSpec(memory_space=pltpu.SMEM),
                  pl.BlockSpec((rows, ATTN_WIDTH), lambda b, n: (b * nb + n, q_blk)),
                  pl.BlockSpec((rows, 2 * KV_WIDTH), lambda b, n: (b * nb + n, kv_blk)),
                  row_table, row_table, col_table, col_table,
                  pl.BlockSpec((None, HEAD_DIM, WINDOW), lambda b, n: (layer, 0, 0)),
                  pl.BlockSpec((None, 1, V7X_LANES), lambda b, n: (layer, 0, 0)),
                  slab(wa), slab(wb), slab(wo)],
        out_specs=[pl.BlockSpec((rows, ATTN_WIDTH), lambda b, n: (b * nb + n, 0)),
                   slab_out(wa), slab_out(wb), slab_out(wo)],
        out_shape=[jax.ShapeDtypeStruct((m, ATTN_WIDTH), BF16)]
        + [jax.ShapeDtypeStruct(w.shape[1:], BF16) for w in (wa, wb, wo)],
        scratch_shapes=[pltpu.VMEM((KV_WIDTH // V7X_LANES, WINDOW, V7X_LANES), BF16),
                        pltpu.VMEM((KV_WIDTH, WINDOW), BF16)],
        compiler_params=_params(2),
        name="swa_attention",
    )(sinks, proj, proj, cos, sin_signed, cos_t, sin_t, gq_t, gk, wa, wb, wo)


def _sgu_kernel(u0_ref, u1_ref, v0_ref, v1_ref, w_ref, bt_ref, g_ref, b_ref, o_ref):
    r = lax.broadcasted_iota(jnp.int32, (CHUNK, CHUNK), 0)
    c = lax.broadcasted_iota(jnp.int32, (CHUNK, CHUNK), 1)
    causal = r >= c
    lane_mean = jnp.full((SGU_GROUP_DIM, SGU_GROUP_DIM), 1.0 / SGU_GROUP_DIM, BF16)
    groups_per_tile = TN // SGU_GROUP_DIM
    n_chunks = o_ref.shape[0] // CHUNK
    groups = range(SGU_GROUPS)

    def lanes(grp):
        return slice(grp * SGU_GROUP_DIM, (grp + 1) * SGU_GROUP_DIM)

    def tile_lanes(grp):
        return lanes(grp % groups_per_tile)

    vs = [(v0_ref, v1_ref)[grp // groups_per_tile][:, tile_lanes(grp)].astype(F32) for grp in groups]
    mus = [jnp.dot(v.astype(BF16), lane_mean, preferred_element_type=F32) for v in vs]
    dvs = [v - mu for v, mu in zip(vs, mus)]
    vars_ = [jnp.dot((dv * dv).astype(BF16), lane_mean, preferred_element_type=F32) for dv in dvs]
    ss = []
    for grp in groups:
        vn = (dvs[grp] * lax.rsqrt(vars_[grp] + EPS) * g_ref[:, lanes(grp)]
              + b_ref[:, lanes(grp)]).astype(BF16)
        w = jnp.where(causal, w_ref[grp], 0.0).astype(BF16)
        vn_chunks = [vn[ch * CHUNK:(ch + 1) * CHUNK, :] for ch in range(n_chunks)]
        ss.append(jnp.dot(w, jnp.concatenate(vn_chunks, axis=-1), preferred_element_type=F32))
    for grp in groups:
        s = ss[grp] + bt_ref[:, grp:grp + 1]
        u_ref = (u0_ref, u1_ref)[grp // groups_per_tile]
        for ch in range(n_chunks):
            u = u_ref[ch * CHUNK:(ch + 1) * CHUNK, tile_lanes(grp)].astype(F32)
            o_ref[ch * CHUNK:(ch + 1) * CHUNK, lanes(grp)] = (
                u * s[:, ch * CHUNK:(ch + 1) * CHUNK]).astype(o_ref.dtype)


def _sgu(proj, w_s, b_t, ln_g, ln_b, layer, rows=4 * CHUNK):
    m = proj.shape[0]
    u0 = PROJ_UV_TILE

    def tile(t):
        return pl.BlockSpec((rows, TN), lambda i: (i, t))

    return pl.pallas_call(
        _sgu_kernel,
        grid=(m // rows,),
        in_specs=[tile(u0), tile(u0 + 1), tile(u0 + 2), tile(u0 + 3),
                  pl.BlockSpec((None, SGU_GROUPS, CHUNK, CHUNK), lambda i: (layer, 0, 0, 0)),
                  pl.BlockSpec((None, CHUNK, SGU_GROUPS), lambda i: (layer, 0, 0)),
                  pl.BlockSpec((None, 1, SGU_WIDTH), lambda i: (layer, 0, 0)),
                  pl.BlockSpec((None, 1, SGU_WIDTH), lambda i: (layer, 0, 0))],
        out_specs=pl.BlockSpec((rows, SGU_WIDTH), lambda i: (i, 0)),
        out_shape=jax.ShapeDtypeStruct((m, SGU_WIDTH), BF16),
        compiler_params=_params(1),
        name="sgu",
    )(proj, proj, proj, proj, w_s, b_t, ln_g, ln_b)


def _rope_tables(seq):
    pos = jnp.arange(seq, dtype=F32)
    inv_freq = jnp.power(ROPE_THETA, -jnp.arange(0, HEAD_DIM, 2, dtype=F32) / HEAD_DIM)
    ang = pos[:, None] * inv_freq[None, :]
    cos, sin = jnp.cos(ang), jnp.sin(ang)
    reps = V7X_LANES // HEAD_DIM
    return (jnp.tile(cos, (1, 2 * reps)), jnp.tile(jnp.concatenate([-sin, sin], axis=-1), (1, reps)),
            cos.T, sin.T)


def kernel(x, mix_norm, w_in, q_norm, k_norm, sinks, sgu_ln_g, sgu_ln_b, w_spatial, b_spatial,
           w_attn_branch, w_sgu_branch, w_out, ffn_norm, w_gate, w_up, w_down):
    batch, seq, d = x.shape
    depth = w_in.shape[0]
    m = batch * seq
    assert d == D_MODEL and w_in.shape[-1] == (QKV_TILES + UV_TILES + GATE_TILES) * TN
    tables = _rope_tables(seq)
    gq_t = jnp.broadcast_to((q_norm * (HEAD_DIM ** -0.5 * LOG2_E))[:, :, None],
                            (depth, HEAD_DIM, WINDOW))
    gk = jnp.tile(k_norm, (1, V7X_LANES // HEAD_DIM)).reshape(depth, 1, V7X_LANES)
    b_t = jnp.swapaxes(b_spatial, 1, 2)
    mix_norm, ffn_norm, sgu_ln_g, sgu_ln_b = (
        p.reshape(depth, 1, -1) for p in (mix_norm, ffn_norm, sgu_ln_g, sgu_ln_b))

    xf = x.reshape(m, d)
    h = _rmsnorm(xf, mix_norm, 0)
    for l in range(depth):
        proj = _in_proj(h, w_in, l)
        a, wa_bf, wb_bf, wo_bf = _attention(proj, sinks[l], tables, gq_t, gk,
                                            w_attn_branch, w_sgu_branch, w_out, l, batch, seq)
        b = _sgu(proj, w_spatial, b_t, sgu_ln_g, sgu_ln_b, l)
        xf, h2 = _mix_out(a, b, proj, xf, wa_bf, wb_bf, wo_bf, ffn_norm, l)
        act, wd_bf = _swiglu(h2, w_gate, w_up, w_down, l)
        if l + 1 < depth:
            xf, h = _down_out(act, xf, wd_bf, mix_norm, l + 1)
        else:
            (xf,) = _down_out(act, xf, wd_bf, None, l)
    return xf.reshape(batch, seq, d)
```

```python
import functools

import jax
import jax.numpy as jnp
from jax import lax
from jax.experimental import pallas as pl
from jax.experimental.pallas import tpu as pltpu

F32 = jnp.float32
BF16 = jnp.bfloat16

HEAD_DIM = 64
N_Q_HEADS = 16
N_KV_HEADS = 4
Q_PER_KV = N_Q_HEADS // N_KV_HEADS
ATTN_WIDTH = N_Q_HEADS * HEAD_DIM
KV_WIDTH = N_KV_HEADS * HEAD_DIM
WINDOW = 128
ROPE_THETA = 10000.0
SGU_GROUPS = 8
SGU_GROUP_DIM = 128
SGU_WIDTH = SGU_GROUPS * SGU_GROUP_DIM
CHUNK = 128
EPS = 1e-6
MASK_VALUE = -1e30
LOG2_E = 1.4426950408889634

V7X_LANES = 128
V7X_VMEM_BYTES = 64 * 1024 * 1024
VMEM_LIMIT = V7X_VMEM_BYTES - 8 * 1024 * 1024

TN = 512
D_MODEL = 2048
QKV_TILES = (ATTN_WIDTH + 2 * KV_WIDTH) // TN
UV_TILES = 2 * SGU_WIDTH // TN
GATE_TILES = 2 * D_MODEL // TN
PROJ_QKV_COL = GATE_TILES * TN
PROJ_UV_TILE = GATE_TILES + QKV_TILES


def _params(n_axes):
    return pltpu.CompilerParams(dimension_semantics=("arbitrary",) * n_axes,
                                vmem_limit_bytes=VMEM_LIMIT)


def _rmsnorm_kernel(x_ref, g_ref, o_ref):
    x = x_ref[...]
    y = x * lax.rsqrt(jnp.mean(x * x, axis=-1, keepdims=True) + EPS)
    o_ref[...] = (y * g_ref[...]).astype(o_ref.dtype)


def _rmsnorm(x, gains, layer, tm=512):
    m, d = x.shape
    return pl.pallas_call(
        _rmsnorm_kernel,
        grid=(m // tm,),
        in_specs=[pl.BlockSpec((tm, d), lambda i: (i, 0)),
                  pl.BlockSpec((None, 1, d), lambda i: (layer, 0, 0))],
        out_specs=pl.BlockSpec((tm, d), lambda i: (i, 0)),
        out_shape=jax.ShapeDtypeStruct((m, d), BF16),
        compiler_params=_params(1),
        name="rmsnorm",
    )(x, gains)


SUB_ROWS = 512


def _row_blocks(o_ref):
    tm = o_ref.shape[0]
    sub = min(SUB_ROWS, tm // 2)
    return [pl.ds(r, sub) for r in range(0, tm, sub)]


def _cast_weights(w_refs, wbf_refs):
    @pl.when(pl.program_id(1) == 0)
    def _():
        for w_ref, wbf_ref in zip(w_refs, wbf_refs):
            wbf_ref[...] = w_ref[...].astype(BF16)


def _sigmoid(y):
    return 0.5 * jnp.tanh(0.5 * y) + 0.5


def _gelu_tanh(y):
    c = 0.7978845608028654
    half = 0.5 * y
    return half + half * jnp.tanh(y * (c + (c * 0.044715) * (y * y)))


def _in_proj_kernel(h_ref, w_ref, o_ref, wbf_ref):
    _cast_weights([w_ref], [wbf_ref])
    j = pl.program_id(0)

    def emit(act):
        for rows in _row_blocks(o_ref):
            y = jnp.dot(h_ref[rows, :], wbf_ref[...], preferred_element_type=F32)
            o_ref[rows, :] = (y if act is None else act(y)).astype(o_ref.dtype)

    pl.when(j < QKV_TILES)(lambda: emit(None))
    pl.when((j >= QKV_TILES) & (j < QKV_TILES + UV_TILES))(
        lambda: emit(_gelu_tanh))
    pl.when(j >= QKV_TILES + UV_TILES)(lambda: emit(_sigmoid))


def _in_proj(h, w, layer, tm=4096):
    m, k = h.shape
    n = w.shape[-1]
    return pl.pallas_call(
        _in_proj_kernel,
        grid=(n // TN, m // tm),
        in_specs=[pl.BlockSpec((tm, k), lambda j, i: (i, 0)),
                  pl.BlockSpec((None, k, TN), lambda j, i: (layer, 0, j))],
        out_specs=pl.BlockSpec(
            (tm, TN), lambda j, i: (i, jnp.where(j < QKV_TILES + UV_TILES, j + GATE_TILES,
                                                 j - (QKV_TILES + UV_TILES)))),
        out_shape=jax.ShapeDtypeStruct((m, n), BF16),
        scratch_shapes=[pltpu.VMEM((k, TN), BF16)],
        compiler_params=_params(2),
        name="in_proj",
    )(h, w)


def _swiglu_kernel(h_ref, wg_ref, wu_ref, wd_ref, o_ref, wd_bf_ref, wgbf_ref, wubf_ref):
    _cast_weights([wg_ref, wu_ref], [wgbf_ref, wubf_ref])
    wd_bf_ref[...] = wd_ref[...].astype(wd_bf_ref.dtype)
    for rows in _row_blocks(o_ref):
        h = h_ref[rows, :]
        g = jnp.dot(h, wgbf_ref[...], preferred_element_type=F32)
        u = jnp.dot(h, wubf_ref[...], preferred_element_type=F32)
        o_ref[rows, :] = (g * _sigmoid(g) * u).astype(o_ref.dtype)


def _swiglu(h, wg, wu, wd, layer, tm=2048):
    m, k = h.shape
    f = wg.shape[-1]
    mt = m // tm
    slab = wd.shape[1] // (f // TN * mt)
    w_spec = pl.BlockSpec((None, k, TN), lambda j, i: (layer, 0, j))
    return pl.pallas_call(
        _swiglu_kernel,
        grid=(f // TN, mt),
        in_specs=[pl.BlockSpec((tm, k), lambda j, i: (i, 0)), w_spec, w_spec,
                  pl.BlockSpec((None, slab, wd.shape[2]), lambda j, i: (layer, j * mt + i, 0))],
        out_specs=[pl.BlockSpec((tm, TN), lambda j, i: (i, j)),
                   pl.BlockSpec((slab, wd.shape[2]), lambda j, i: (j * mt + i, 0))],
        out_shape=[jax.ShapeDtypeStruct((m, f), BF16), jax.ShapeDtypeStruct(wd.shape[1:], BF16)],
        scratch_shapes=[pltpu.VMEM((k, TN), BF16), pltpu.VMEM((k, TN), BF16)],
        compiler_params=_params(2),
        name="swiglu",
    )(h, wg, wu, wd)


def _mix_out_kernel(a_ref, b_ref, ga_ref, gb_ref, x_ref, wa_ref, wb_ref, wo_ref, gain_ref,
                    o_ref, h_ref, merged_ref):
    tm, d = o_ref.shape
    tiles = [slice(t * TN, (t + 1) * TN) for t in range(d // TN)]
    halves = [pl.ds(r, tm // 2) for r in (0, tm // 2)]
    for rows in halves:
        for cols in tiles:
            ya = jnp.dot(a_ref[rows, :], wa_ref[:, cols], preferred_element_type=F32)
            yb = jnp.dot(b_ref[rows, :], wb_ref[:, cols], preferred_element_type=F32)
            merged_ref[rows, cols] = (ga_ref[rows, cols] * ya + gb_ref[rows, cols] * yb
                                      ).astype(merged_ref.dtype)
    for rows in halves:
        ssq = jnp.zeros((tm // 2, 1), F32)
        for cols in tiles:
            y = x_ref[rows, cols] + jnp.dot(merged_ref[rows, :], wo_ref[:, cols],
                                            preferred_element_type=F32)
            o_ref[rows, cols] = y
            ssq = ssq + jnp.sum(y * y, axis=-1, keepdims=True)
        scale = lax.rsqrt(ssq * (1.0 / d) + EPS)
        h_ref[rows, :] = (o_ref[rows, :] * scale * gain_ref[...]).astype(h_ref.dtype)


def _mix_out(a, b, proj, x, wa_bf, wb_bf, wo_bf, gains, layer, tm=512):
    m, k = a.shape
    d = x.shape[1]
    row_tile = pl.BlockSpec((tm, d), lambda i: (i, 0))

    def whole(w):
        return pl.BlockSpec(w.shape, lambda i: (0, 0))

    return pl.pallas_call(
        _mix_out_kernel,
        grid=(m // tm,),
        in_specs=[pl.BlockSpec((tm, k), lambda i: (i, 0)),
                  pl.BlockSpec((tm, k), lambda i: (i, 0)),
                  pl.BlockSpec((tm, d), lambda i: (i, 0)),
                  pl.BlockSpec((tm, d), lambda i: (i, 1)),
                  row_tile, whole(wa_bf), whole(wb_bf), whole(wo_bf),
                  pl.BlockSpec((None, 1, d), lambda i: (layer, 0, 0))],
        out_specs=[row_tile, row_tile],
        out_shape=[jax.ShapeDtypeStruct((m, d), F32), jax.ShapeDtypeStruct((m, d), BF16)],
        scratch_shapes=[pltpu.VMEM((tm, d), BF16)],
        compiler_params=_params(1),
        name="mix_out",
    )(a, b, proj, proj, x, wa_bf, wb_bf, wo_bf, gains)


def _down_out_kernel(act_ref, x_ref, w_ref, *rest):
    gain_ref, o_ref, h_ref = rest if len(rest) == 3 else (None, rest[0], None)
    tm, d = o_ref.shape
    tiles = [slice(t * TN, (t + 1) * TN) for t in range(d // TN)]
    for rows in (pl.ds(r, tm // 2) for r in (0, tm // 2)):
        ssq = jnp.zeros((tm // 2, 1), F32)
        for cols in tiles:
            y = x_ref[rows, cols] + jnp.dot(act_ref[rows, :], w_ref[:, cols],
                                            preferred_element_type=F32)
            o_ref[rows, cols] = y
            if h_ref is not None:
                ssq = ssq + jnp.sum(y * y, axis=-1, keepdims=True)
        if h_ref is not None:
            scale = lax.rsqrt(ssq * (1.0 / d) + EPS)
            h_ref[rows, :] = (o_ref[rows, :] * scale * gain_ref[...]).astype(h_ref.dtype)


def _down_out(act, x, w_bf, gains, layer, tm=512):
    m, k = act.shape
    d = x.shape[1]
    row_tile = pl.BlockSpec((tm, d), lambda i: (i, 0))
    in_specs = [pl.BlockSpec((tm, k), lambda i: (i, 0)), row_tile,
                pl.BlockSpec(w_bf.shape, lambda i: (0, 0))]
    operands = [act, x, w_bf]
    out_specs, out_shape = [row_tile], [jax.ShapeDtypeStruct((m, d), F32)]
    if gains is not None:
        in_specs.append(pl.BlockSpec((None, 1, d), lambda i: (layer, 0, 0)))
        operands.append(gains)
        out_specs.append(row_tile)
        out_shape.append(jax.ShapeDtypeStruct((m, d), BF16))
    return pl.pallas_call(
        _down_out_kernel,
        grid=(m // tm,),
        in_specs=in_specs,
        out_specs=out_specs,
        out_shape=out_shape,
        compiler_params=_params(1),
        name="down_out" if gains is not None else "down_out_last",
    )(*operands)


ATTN_BLOCKS_PER_STEP = 4


def _attn_kernel(sinks_ref, q_ref, kv_ref, cos_ref, sin_ref, cos_t_ref, sin_t_ref, gq_t_ref, gk_ref,
                 wa_ref, wb_ref, wo_ref, o_ref, wa_bf_ref, wb_bf_ref, wo_bf_ref,
                 k_prev_ref, vt_prev_ref):
    n = pl.program_id(1)
    half_dim = HEAD_DIM // 2

    for src, dst in ((wa_ref, wa_bf_ref), (wb_ref, wb_bf_ref), (wo_ref, wo_bf_ref)):
        dst[...] = src[...].astype(dst.dtype)

    @pl.when(n == 0)
    def _():
        k_prev_ref[...] = jnp.zeros_like(k_prev_ref)
        vt_prev_ref[...] = jnp.zeros_like(vt_prev_ref)

    lane = lax.broadcasted_iota(jnp.int32, (1, V7X_LANES), 1)
    even_quarter = ((lane // half_dim) % 2) == 0
    r = lax.broadcasted_iota(jnp.int32, (V7X_LANES, V7X_LANES), 0)
    c = lax.broadcasted_iota(jnp.int32, (V7X_LANES, V7X_LANES), 1)
    head_mean = jnp.where((r // HEAD_DIM) == (c // HEAD_DIM), 1.0 / HEAD_DIM, 0.0).astype(BF16)
    key = lax.broadcasted_iota(jnp.int32, (2 * WINDOW, WINDOW), 0)
    qry = lax.broadcasted_iota(jnp.int32, (2 * WINDOW, WINDOW), 1)
    diff = qry + WINDOW - key
    band = (diff >= 0) & (diff < WINDOW)
    gq_t = gq_t_ref[...]
    zeros = jnp.zeros((HEAD_DIM, WINDOW), BF16)
    n_cols = KV_WIDTH // V7X_LANES
    heads_per_col = N_Q_HEADS // n_cols
    group_w = Q_PER_KV * WINDOW
    sink_row = [jnp.full((1, WINDOW), sinks_ref[h] * LOG2_E, F32) for h in range(N_Q_HEADS)]

    def one_block(rows, k_prev, vt_prev, allowed):
        cos, sin = cos_ref[rows, :], sin_ref[rows, :]
        k_cols = []
        for col in range(n_cols):
            x = kv_ref[rows, col * V7X_LANES:(col + 1) * V7X_LANES].astype(F32)
            ms = jnp.dot((x * x).astype(BF16), head_mean, preferred_element_type=F32)
            y = x * lax.rsqrt(ms + EPS) * gk_ref[...]
            partner = jnp.where(even_quarter,
                                pltpu.roll(y, V7X_LANES - half_dim, 1),
                                pltpu.roll(y, half_dim, 1))
            k_cols.append((y * cos + partner * sin).astype(BF16))
        k_win = [jnp.concatenate([k_prev[col], k_cols[col]], axis=0) for col in range(n_cols)]
        vt_cur = kv_ref[rows, KV_WIDTH:].astype(F32).T.astype(BF16)
        vt_win = jnp.concatenate([vt_prev, vt_cur], axis=1)

        cos_t, sin_t = cos_t_ref[:, rows], sin_t_ref[:, rows]
        qt_pads = []
        for pair in range(N_Q_HEADS // 2):
            sl = slice(pair * V7X_LANES, (pair + 1) * V7X_LANES)
            qt_pair = q_ref[rows, sl].astype(F32).T
            for half in range(2):
                kv_head = (2 * pair + half) // Q_PER_KV
                x = qt_pair[half * HEAD_DIM:(half + 1) * HEAD_DIM, :]
                y = x * lax.rsqrt(jnp.mean(x * x, axis=0, keepdims=True) + EPS) * gq_t
                y1, y2 = y[:half_dim], y[half_dim:]
                qt = jnp.concatenate([y1 * cos_t - y2 * sin_t, y2 * cos_t + y1 * sin_t],
                                     axis=0).astype(BF16)
                qt_pads.append(jnp.concatenate([zeros, qt] if kv_head % 2 else [qt, zeros], axis=0))

        probs, inv_denoms = [], []
        for col in range(n_cols):
            heads = range(col * heads_per_col, (col + 1) * heads_per_col)
            s = jnp.dot(k_win[col], jnp.concatenate([qt_pads[h] for h in heads], axis=1),
                        preferred_element_type=F32)
            s = jnp.where(jnp.concatenate([allowed] * heads_per_col, axis=1), s, MASK_VALUE)
            sink = jnp.concatenate([sink_row[h] for h in heads], axis=1)
            m = jnp.maximum(jnp.max(s, axis=0, keepdims=True), sink)
            p = jnp.exp2(s - m)
            inv_denoms.append(1.0 / (jnp.sum(p, axis=0, keepdims=True) + jnp.exp2(sink - m)))
            probs.append(p.astype(BF16))

        for kv_head in range(N_KV_HEADS):
            col, part = divmod(kv_head, N_KV_HEADS // n_cols)
            lanes = slice(part * group_w, (part + 1) * group_w)
            vt = vt_win[kv_head * HEAD_DIM:(kv_head + 1) * HEAD_DIM, :]
            o = jnp.dot(vt, probs[col][:, lanes], preferred_element_type=F32)
            o = o * inv_denoms[col][:, lanes]
            for pair in range(Q_PER_KV // 2):
                o_pair = jnp.concatenate([o[:, (2 * pair) * WINDOW:(2 * pair + 1) * WINDOW],
                                          o[:, (2 * pair + 1) * WINDOW:(2 * pair + 2) * WINDOW]], axis=0)
                out_lane = (kv_head * Q_PER_KV + 2 * pair) * HEAD_DIM
                o_ref[rows, out_lane:out_lane + V7X_LANES] = o_pair.T.astype(o_ref.dtype)
        return k_cols, vt_cur

    k_prev = [k_prev_ref[col] for col in range(n_cols)]
    vt_prev = vt_prev_ref[...]
    for blk in range(o_ref.shape[0] // WINDOW):
        allowed = band & ((key >= WINDOW) | (n > 0)) if blk == 0 else band
        k_prev, vt_prev = one_block(pl.ds(blk * WINDOW, WINDOW), k_prev, vt_prev, allowed)

    for col in range(n_cols):
        k_prev_ref[col] = k_prev[col]
    vt_prev_ref[...] = vt_prev


def _attention(proj, sinks, tables, gq_t, gk, wa, wb, wo, layer, batch, seq):
    m = proj.shape[0]
    rows = ATTN_BLOCKS_PER_STEP * WINDOW
    nb = seq // rows
    steps = batch * nb

    def slab(w):
        return pl.BlockSpec((None, w.shape[1] // steps, w.shape[2]),
                            lambda b, n: (layer, b * nb + n, 0))

    def slab_out(w):
        return pl.BlockSpec((w.shape[1] // steps, w.shape[2]), lambda b, n: (b * nb + n, 0))

    q_blk = PROJ_QKV_COL // ATTN_WIDTH
    kv_blk = (PROJ_QKV_COL + ATTN_WIDTH) // (2 * KV_WIDTH)
    cos, sin_signed, cos_t, sin_t = tables
    half_dim = HEAD_DIM // 2
    row_table = pl.BlockSpec((rows, V7X_LANES), lambda b, n: (n, 0))
    col_table = pl.BlockSpec((half_dim, rows), lambda b, n: (0, n))
    return pl.pallas_call(
        _attn_kernel,
        grid=(batch, nb),
        in_specs=[pl.BlockSpec(memory_space=pltpu.SMEM),
                  pl.BlockSpec((rows, ATTN_WIDTH), lambda b, n: (b * nb + n, q_blk)),
                  pl.BlockSpec((rows, 2 * KV_WIDTH), lambda b, n: (b * nb + n, kv_blk)),
                  row_table, row_table, col_table, col_table,
                  pl.BlockSpec((None, HEAD_DIM, WINDOW), lambda b, n: (layer, 0, 0)),
                  pl.BlockSpec((None, 1, V7X_LANES), lambda b, n: (layer, 0, 0)),
                  slab(wa), slab(wb), slab(wo)],
        out_specs=[pl.BlockSpec((rows, ATTN_WIDTH), lambda b, n: (b * nb + n, 0)),
                   slab_out(wa), slab_out(wb), slab_out(wo)],
        out_shape=[jax.ShapeDtypeStruct((m, ATTN_WIDTH), BF16)]
        + [jax.ShapeDtypeStruct(w.shape[1:], BF16) for w in (wa, wb, wo)],
        scratch_shapes=[pltpu.VMEM((KV_WIDTH // V7X_LANES, WINDOW, V7X_LANES), BF16),
                        pltpu.VMEM((KV_WIDTH, WINDOW), BF16)],
        compiler_params=_params(2),
        name="swa_attention",
    )(sinks, proj, proj, cos, sin_signed, cos_t, sin_t, gq_t, gk, wa, wb, wo)


def _sgu_kernel(u0_ref, u1_ref, v0_ref, v1_ref, w_ref, bt_ref, g_ref, b_ref, o_ref):
    r = lax.broadcasted_iota(jnp.int32, (CHUNK, CHUNK), 0)
    c = lax.broadcasted_iota(jnp.int32, (CHUNK, CHUNK), 1)
    causal = r >= c
    lane_mean = jnp.full((SGU_GROUP_DIM, SGU_GROUP_DIM), 1.0 / SGU_GROUP_DIM, BF16)
    groups_per_tile = TN // SGU_GROUP_DIM
    n_chunks = o_ref.shape[0] // CHUNK
    groups = range(SGU_GROUPS)

    def lanes(grp):
        return slice(grp * SGU_GROUP_DIM, (grp + 1) * SGU_GROUP_DIM)

    def tile_lanes(grp):
        return lanes(grp % groups_per_tile)

    vs = [(v0_ref, v1_ref)[grp // groups_per_tile][:, tile_lanes(grp)].astype(F32) for grp in groups]
    mus = [jnp.dot(v.astype(BF16), lane_mean, preferred_element_type=F32) for v in vs]
    dvs = [v - mu for v, mu in zip(vs, mus)]
    vars_ = [jnp.dot((dv * dv).astype(BF16), lane_mean, preferred_element_type=F32) for dv in dvs]
    ss = []
    for grp in groups:
        vn = (dvs[grp] * lax.rsqrt(vars_[grp] + EPS) * g_ref[:, lanes(grp)]
              + b_ref[:, lanes(grp)]).astype(BF16)
        w = jnp.where(causal, w_ref[grp], 0.0).astype(BF16)
        vn_chunks = [vn[ch * CHUNK:(ch + 1) * CHUNK, :] for ch in range(n_chunks)]
        ss.append(jnp.dot(w, jnp.concatenate(vn_chunks, axis=-1), preferred_element_type=F32))
    for grp in groups:
        s = ss[grp] + bt_ref[:, grp:grp + 1]
        u_ref = (u0_ref, u1_ref)[grp // groups_per_tile]
        for ch in range(n_chunks):
            u = u_ref[ch * CHUNK:(ch + 1) * CHUNK, tile_lanes(grp)].astype(F32)
            o_ref[ch * CHUNK:(ch + 1) * CHUNK, lanes(grp)] = (
                u * s[:, ch * CHUNK:(ch + 1) * CHUNK]).astype(o_ref.dtype)


def _sgu(proj, w_s, b_t, ln_g, ln_b, layer, rows=4 * CHUNK):
    m = proj.shape[0]
    u0 = PROJ_UV_TILE

    def tile(t):
        return pl.BlockSpec((rows, TN), lambda i: (i, t))

    return pl.pallas_call(
        _sgu_kernel,
        grid=(m // rows,),
        in_specs=[tile(u0), tile(u0 + 1), tile(u0 + 2), tile(u0 + 3),
                  pl.BlockSpec((None, SGU_GROUPS, CHUNK, CHUNK), lambda i: (layer, 0, 0, 0)),
                  pl.BlockSpec((None, CHUNK, SGU_GROUPS), lambda i: (layer, 0, 0)),
                  pl.BlockSpec((None, 1, SGU_WIDTH), lambda i: (layer, 0, 0)),
                  pl.BlockSpec((None, 1, SGU_WIDTH), lambda i: (layer, 0, 0))],
        out_specs=pl.BlockSpec((rows, SGU_WIDTH), lambda i: (i, 0)),
        out_shape=jax.ShapeDtypeStruct((m, SGU_WIDTH), BF16),
        compiler_params=_params(1),
        name="sgu",
    )(proj, proj, proj, proj, w_s, b_t, ln_g, ln_b)


def _rope_tables(seq):
    pos = jnp.arange(seq, dtype=F32)
    inv_freq = jnp.power(ROPE_THETA, -jnp.arange(0, HEAD_DIM, 2, dtype=F32) / HEAD_DIM)
    ang = pos[:, None] * inv_freq[None, :]
    cos, sin = jnp.cos(ang), jnp.sin(ang)
    reps = V7X_LANES // HEAD_DIM
    return (jnp.tile(cos, (1, 2 * reps)), jnp.tile(jnp.concatenate([-sin, sin], axis=-1), (1, reps)),
            cos.T, sin.T)


def kernel(x, mix_norm, w_in, q_norm, k_norm, sinks, sgu_ln_g, sgu_ln_b, w_spatial, b_spatial,
           w_attn_branch, w_sgu_branch, w_out, ffn_norm, w_gate, w_up, w_down):
    batch, seq, d = x.shape
    depth = w_in.shape[0]
    m = batch * seq
    assert d == D_MODEL and w_in.shape[-1] == (QKV_TILES + UV_TILES + GATE_TILES) * TN
    tables = _rope_tables(seq)
    gq_t = jnp.broadcast_to((q_norm * (HEAD_DIM ** -0.5 * LOG2_E))[:, :, None],
                            (depth, HEAD_DIM, WINDOW))
    gk = jnp.tile(k_norm, (1, V7X_LANES // HEAD_DIM)).reshape(depth, 1, V7X_LANES)
    b_t = jnp.swapaxes(b_spatial, 1, 2)
    mix_norm, ffn_norm, sgu_ln_g, sgu_ln_b = (
        p.reshape(depth, 1, -1) for p in (mix_norm, ffn_norm, sgu_ln_g, sgu_ln_b))

    xf = x.reshape(m, d)
    h = _rmsnorm(xf, mix_norm, 0)
    for l in range(depth):
        proj = _in_proj(h, w_in, l)
        a, wa_bf, wb_bf, wo_bf = _attention(proj, sinks[l], tables, gq_t, gk,
                                            w_attn_branch, w_sgu_branch, w_out, l, batch, seq)
        b = _sgu(proj, w_spatial, b_t, sgu_ln_g, sgu_ln_b, l)
        xf, h2 = _mix_out(a, b, proj, xf, wa_bf, wb_bf, wo_bf, ffn_norm, l)
        act, wd_bf = _swiglu(h2, w_gate, w_up, w_down, l)
        if l + 1 < depth:
            xf, h = _down_out(act, xf, wd_bf, mix_norm, l + 1)
        else:
            (xf,) = _down_out(act, xf, wd_bf, None, l)
    return xf.reshape(batch, seq, d)
```

```python
import jax
import jax.numpy as jnp
from jax import lax
from jax.experimental import pallas as pl
from jax.experimental.pallas import tpu as pltpu

F32 = jnp.float32
BF16 = jnp.bfloat16

HEAD_DIM = 64
N_Q_HEADS = 16
N_KV_HEADS = 4
Q_PER_KV = N_Q_HEADS // N_KV_HEADS
ATTN_WIDTH = N_Q_HEADS * HEAD_DIM
KV_WIDTH = N_KV_HEADS * HEAD_DIM
WINDOW = 128
ROPE_THETA = 10000.0
SGU_GROUPS = 8
SGU_GROUP_DIM = 128
SGU_WIDTH = SGU_GROUPS * SGU_GROUP_DIM
CHUNK = 128
EPS = 1e-6
MASK_VALUE = -1e30
LOG2_E = 1.4426950408889634

V7X_LANES = 128
V7X_SUBLANES_BF16 = 16
V7X_VMEM_BYTES = 64 * 1024 * 1024
VMEM_LIMIT = V7X_VMEM_BYTES - 8 * 1024 * 1024

TN = 512
D_MODEL = 2048
QKV_TILES = (ATTN_WIDTH + 2 * KV_WIDTH) // TN
UV_TILES = 2 * SGU_WIDTH // TN
GATE_TILES = 2 * D_MODEL // TN
PROJ_QKV_COL = GATE_TILES * TN
PROJ_UV_TILE = GATE_TILES + QKV_TILES


def _params(n_axes):
    return pltpu.CompilerParams(dimension_semantics=("arbitrary",) * n_axes,
                                vmem_limit_bytes=VMEM_LIMIT)


def _rmsnorm_kernel(x_ref, g_ref, o_ref):
    x = x_ref[...]
    y = x * lax.rsqrt(jnp.mean(x * x, axis=-1, keepdims=True) + EPS)
    o_ref[...] = (y * g_ref[...]).astype(o_ref.dtype)


def _rmsnorm(x, gains, layer, tm=512):
    m, d = x.shape
    return pl.pallas_call(
        _rmsnorm_kernel,
        grid=(m // tm,),
        in_specs=[pl.BlockSpec((tm, d), lambda i: (i, 0)),
                  pl.BlockSpec((None, 1, d), lambda i: (layer, 0, 0))],
        out_specs=pl.BlockSpec((tm, d), lambda i: (i, 0)),
        out_shape=jax.ShapeDtypeStruct((m, d), BF16),
        compiler_params=_params(1),
        name="rmsnorm",
    )(x, gains)


SUB_ROWS = 512


def _row_blocks(o_ref):
    tm = o_ref.shape[0]
    sub = min(SUB_ROWS, tm // 2)
    return [pl.ds(r, sub) for r in range(0, tm, sub)]


def _cast_weights(w_refs, wbf_refs):
    @pl.when(pl.program_id(1) == 0)
    def _():
        for w_ref, wbf_ref in zip(w_refs, wbf_refs):
            wbf_ref[...] = w_ref[...].astype(BF16)


def _sigmoid(y):
    return 0.5 * jnp.tanh(0.5 * y) + 0.5


def _gelu_tanh(y):
    c = 0.7978845608028654
    half = 0.5 * y
    return half + half * jnp.tanh(y * (c + (c * 0.044715) * (y * y)))


def _in_proj_kernel(h_ref, w_ref, o_ref, wbf_ref):
    _cast_weights([w_ref], [wbf_ref])
    j = pl.program_id(0)

    def emit(act):
        for rows in _row_blocks(o_ref):
            y = jnp.dot(h_ref[rows, :], wbf_ref[...], preferred_element_type=F32)
            o_ref[rows, :] = (y if act is None else act(y)).astype(o_ref.dtype)

    pl.when(j < QKV_TILES)(lambda: emit(None))
    pl.when((j >= QKV_TILES) & (j < QKV_TILES + UV_TILES))(
        lambda: emit(_gelu_tanh))
    pl.when(j >= QKV_TILES + UV_TILES)(lambda: emit(_sigmoid))


def _in_proj(h, w, layer, tm=4096):
    m, k = h.shape
    n = w.shape[-1]
    return pl.pallas_call(
        _in_proj_kernel,
        grid=(n // TN, m // tm),
        in_specs=[pl.BlockSpec((tm, k), lambda j, i: (i, 0)),
                  pl.BlockSpec((None, k, TN), lambda j, i: (layer, 0, j))],
        out_specs=pl.BlockSpec(
            (tm, TN), lambda j, i: (i, jnp.where(j < QKV_TILES + UV_TILES, j + GATE_TILES,
                                                 j - (QKV_TILES + UV_TILES)))),
        out_shape=jax.ShapeDtypeStruct((m, n), BF16),
        scratch_shapes=[pltpu.VMEM((k, TN), BF16)],
        compiler_params=_params(2),
        name="in_proj",
    )(h, w)


def _swiglu_kernel(h_ref, wg_ref, wu_ref, wd_ref, o_ref, wd_bf_ref, wgbf_ref, wubf_ref):
    _cast_weights([wg_ref, wu_ref], [wgbf_ref, wubf_ref])
    wd_bf_ref[...] = wd_ref[...].astype(wd_bf_ref.dtype)
    for rows in _row_blocks(o_ref):
        h = h_ref[rows, :]
        g = jnp.dot(h, wgbf_ref[...], preferred_element_type=F32)
        u = jnp.dot(h, wubf_ref[...], preferred_element_type=F32)
        o_ref[rows, :] = (g * _sigmoid(g) * u).astype(o_ref.dtype)


def _swiglu(h, wg, wu, wd, layer, tm=2048):
    m, k = h.shape
    f = wg.shape[-1]
    mt = m // tm
    slab = wd.shape[1] // (f // TN * mt)
    w_spec = pl.BlockSpec((None, k, TN), lambda j, i: (layer, 0, j))
    return pl.pallas_call(
        _swiglu_kernel,
        grid=(f // TN, mt),
        in_specs=[pl.BlockSpec((tm, k), lambda j, i: (i, 0)), w_spec, w_spec,
                  pl.BlockSpec((None, slab, wd.shape[2]), lambda j, i: (layer, j * mt + i, 0))],
        out_specs=[pl.BlockSpec((tm, TN), lambda j, i: (i, j)),
                   pl.BlockSpec((slab, wd.shape[2]), lambda j, i: (j * mt + i, 0))],
        out_shape=[jax.ShapeDtypeStruct((m, f), BF16), jax.ShapeDtypeStruct(wd.shape[1:], BF16)],
        scratch_shapes=[pltpu.VMEM((k, TN), BF16), pltpu.VMEM((k, TN), BF16)],
        compiler_params=_params(2),
        name="swiglu",
    )(h, wg, wu, wd)


def _mix_out_kernel(a_ref, b_ref, ga_ref, gb_ref, x_ref, wa_ref, wb_ref, wo_ref, gain_ref,
                    o_ref, h_ref, merged_ref):
    tm, d = o_ref.shape
    tiles = [slice(t * TN, (t + 1) * TN) for t in range(d // TN)]
    halves = [pl.ds(r, tm // 2) for r in (0, tm // 2)]
    for rows in halves:
        for cols in tiles:
            ya = jnp.dot(a_ref[rows, :], wa_ref[:, cols], preferred_element_type=F32)
            yb = jnp.dot(b_ref[rows, :], wb_ref[:, cols], preferred_element_type=F32)
            merged_ref[rows, cols] = (ga_ref[rows, cols] * ya + gb_ref[rows, cols] * yb
                                      ).astype(merged_ref.dtype)
    for rows in halves:
        ssq = jnp.zeros((tm // 2, 1), F32)
        for cols in tiles:
            y = x_ref[rows, cols] + jnp.dot(merged_ref[rows, :], wo_ref[:, cols],
                                            preferred_element_type=F32)
            o_ref[rows, cols] = y
            ssq = ssq + jnp.sum(y * y, axis=-1, keepdims=True)
        scale = lax.rsqrt(ssq * (1.0 / d) + EPS)
        h_ref[rows, :] = (o_ref[rows, :] * scale * gain_ref[...]).astype(h_ref.dtype)


def _mix_out(a, b, proj, x, wa_bf, wb_bf, wo_bf, gains, layer, tm=512):
    m, k = a.shape
    d = x.shape[1]
    row_tile = pl.BlockSpec((tm, d), lambda i: (i, 0))

    def whole(w):
        return pl.BlockSpec(w.shape, lambda i: (0, 0))

    return pl.pallas_call(
        _mix_out_kernel,
        grid=(m // tm,),
        in_specs=[pl.BlockSpec((tm, k), lambda i: (i, 0)),
                  pl.BlockSpec((tm, k), lambda i: (i, 0)),
                  pl.BlockSpec((tm, d), lambda i: (i, 0)),
                  pl.BlockSpec((tm, d), lambda i: (i, 1)),
                  row_tile, whole(wa_bf), whole(wb_bf), whole(wo_bf),
                  pl.BlockSpec((None, 1, d), lambda i: (layer, 0, 0))],
        out_specs=[row_tile, row_tile],
        out_shape=[jax.ShapeDtypeStruct((m, d), F32), jax.ShapeDtypeStruct((m, d), BF16)],
        scratch_shapes=[pltpu.VMEM((tm, d), BF16)],
        compiler_params=_params(1),
        name="mix_out",
    )(a, b, proj, proj, x, wa_bf, wb_bf, wo_bf, gains)


def _down_out_kernel(act_ref, x_ref, w_ref, *rest):
    gain_ref, o_ref, h_ref = rest if len(rest) == 3 else (None, rest[0], None)
    tm, d = o_ref.shape
    tiles = [slice(t * TN, (t + 1) * TN) for t in range(d // TN)]
    for rows in (pl.ds(r, tm // 2) for r in (0, tm // 2)):
        ssq = jnp.zeros((tm // 2, 1), F32)
        for cols in tiles:
            y = x_ref[rows, cols] + jnp.dot(act_ref[rows, :], w_ref[:, cols],
                                            preferred_element_type=F32)
            o_ref[rows, cols] = y
            if h_ref is not None:
                ssq = ssq + jnp.sum(y * y, axis=-1, keepdims=True)
        if h_ref is not None:
            scale = lax.rsqrt(ssq * (1.0 / d) + EPS)
            h_ref[rows, :] = (o_ref[rows, :] * scale * gain_ref[...]).astype(h_ref.dtype)


def _down_out(act, x, w_bf, gains, layer, tm=512):
    m, k = act.shape
    d = x.shape[1]
    row_tile = pl.BlockSpec((tm, d), lambda i: (i, 0))
    in_specs = [pl.BlockSpec((tm, k), lambda i: (i, 0)), row_tile,
                pl.BlockSpec(w_bf.shape, lambda i: (0, 0))]
    operands = [act, x, w_bf]
    out_specs, out_shape = [row_tile], [jax.ShapeDtypeStruct((m, d), F32)]
    if gains is not None:
        in_specs.append(pl.BlockSpec((None, 1, d), lambda i: (layer, 0, 0)))
        operands.append(gains)
        out_specs.append(row_tile)
        out_shape.append(jax.ShapeDtypeStruct((m, d), BF16))
    return pl.pallas_call(
        _down_out_kernel,
        grid=(m // tm,),
        in_specs=in_specs,
        out_specs=out_specs,
        out_shape=out_shape,
        compiler_params=_params(1),
        name="down_out" if gains is not None else "down_out_last",
    )(*operands)


ATTN_BLOCKS_PER_STEP = 4


def _attn_kernel(sinks_ref, q_ref, kv_ref, cos_ref, sin_ref, cos_t_ref, sin_t_ref, gq_t_ref, gk_ref,
                 wa_ref, wb_ref, wo_ref, o_ref, wa_bf_ref, wb_bf_ref, wo_bf_ref,
                 k_prev_ref, vt_prev_ref):
    n = pl.program_id(1)
    half_dim = HEAD_DIM // 2

    for src, dst in ((wa_ref, wa_bf_ref), (wb_ref, wb_bf_ref), (wo_ref, wo_bf_ref)):
        dst[...] = src[...].astype(dst.dtype)

    @pl.when(n == 0)
    def _():
        k_prev_ref[...] = jnp.zeros_like(k_prev_ref)
        vt_prev_ref[...] = jnp.zeros_like(vt_prev_ref)

    lane = lax.broadcasted_iota(jnp.int32, (1, V7X_LANES), 1)
    even_quarter = ((lane // half_dim) % 2) == 0
    r = lax.broadcasted_iota(jnp.int32, (V7X_LANES, V7X_LANES), 0)
    c = lax.broadcasted_iota(jnp.int32, (V7X_LANES, V7X_LANES), 1)
    head_mean = jnp.where((r // HEAD_DIM) == (c // HEAD_DIM), 1.0 / HEAD_DIM, 0.0).astype(BF16)
    key = lax.broadcasted_iota(jnp.int32, (2 * WINDOW, WINDOW), 0)
    qry = lax.broadcasted_iota(jnp.int32, (2 * WINDOW, WINDOW), 1)
    diff = qry + WINDOW - key
    band = (diff >= 0) & (diff < WINDOW)
    gq_t = gq_t_ref[...]
    zeros = jnp.zeros((HEAD_DIM, WINDOW), BF16)
    n_cols = KV_WIDTH // V7X_LANES
    heads_per_col = N_Q_HEADS // n_cols
    group_w = Q_PER_KV * WINDOW
    sink_row = [jnp.full((1, WINDOW), sinks_ref[h] * LOG2_E, F32) for h in range(N_Q_HEADS)]

    def one_block(rows, k_prev, vt_prev, allowed):
        cos, sin = cos_ref[rows, :], sin_ref[rows, :]
        k_cols = []
        for col in range(n_cols):
            x = kv_ref[rows, col * V7X_LANES:(col + 1) * V7X_LANES].astype(F32)
            ms = jnp.dot((x * x).astype(BF16), head_mean, preferred_element_type=F32)
            y = x * lax.rsqrt(ms + EPS) * gk_ref[...]
            partner = jnp.where(even_quarter,
                                pltpu.roll(y, V7X_LANES - half_dim, 1),
                                pltpu.roll(y, half_dim, 1))
            k_cols.append((y * cos + partner * sin).astype(BF16))
        k_win = [jnp.concatenate([k_prev[col], k_cols[col]], axis=0) for col in range(n_cols)]
        vt_cur = kv_ref[rows, KV_WIDTH:].astype(F32).T.astype(BF16)
        vt_win = jnp.concatenate([vt_prev, vt_cur], axis=1)

        cos_t, sin_t = cos_t_ref[:, rows], sin_t_ref[:, rows]
        qt_pads = []
        for pair in range(N_Q_HEADS // 2):
            sl = slice(pair * V7X_LANES, (pair + 1) * V7X_LANES)
            qt_pair = q_ref[rows, sl].astype(F32).T
            for half in range(2):
                kv_head = (2 * pair + half) // Q_PER_KV
                x = qt_pair[half * HEAD_DIM:(half + 1) * HEAD_DIM, :]
                y = x * lax.rsqrt(jnp.mean(x * x, axis=0, keepdims=True) + EPS) * gq_t
                y1, y2 = y[:half_dim], y[half_dim:]
                qt = jnp.concatenate([y1 * cos_t - y2 * sin_t, y2 * cos_t + y1 * sin_t],
                                     axis=0).astype(BF16)
                qt_pads.append(jnp.concatenate([zeros, qt] if kv_head % 2 else [qt, zeros], axis=0))

        probs, sink_terms = [], []
        for col in range(n_cols):
            heads = range(col * heads_per_col, (col + 1) * heads_per_col)
            s = jnp.dot(k_win[col], jnp.concatenate([qt_pads[h] for h in heads], axis=1),
                        preferred_element_type=F32)
            s = jnp.where(jnp.concatenate([allowed] * heads_per_col, axis=1), s, MASK_VALUE)
            sink = jnp.concatenate([sink_row[h] for h in heads], axis=1)
            m = jnp.maximum(jnp.max(s, axis=0, keepdims=True), sink)
            probs.append(jnp.exp2(s - m).astype(BF16))
            sink_terms.append(jnp.exp2(sink - m))

        ones_rows = jnp.ones((V7X_SUBLANES_BF16, 2 * WINDOW), BF16)
        for kv_head in range(N_KV_HEADS):
            col, part = divmod(kv_head, N_KV_HEADS // n_cols)
            lanes = slice(part * group_w, (part + 1) * group_w)
            vt = jnp.concatenate([vt_win[kv_head * HEAD_DIM:(kv_head + 1) * HEAD_DIM, :], ones_rows],
                                 axis=0)
            o = jnp.dot(vt, probs[col][:, lanes], preferred_element_type=F32)
            denom = o[HEAD_DIM:HEAD_DIM + 1, :] + sink_terms[col][:, lanes]
            o = o[:HEAD_DIM, :] * (1.0 / denom)
            for pair in range(Q_PER_KV // 2):
                o_pair = jnp.concatenate([o[:, (2 * pair) * WINDOW:(2 * pair + 1) * WINDOW],
                                          o[:, (2 * pair + 1) * WINDOW:(2 * pair + 2) * WINDOW]], axis=0)
                out_lane = (kv_head * Q_PER_KV + 2 * pair) * HEAD_DIM
                o_ref[rows, out_lane:out_lane + V7X_LANES] = o_pair.T.astype(o_ref.dtype)
        return k_cols, vt_cur

    k_prev = [k_prev_ref[col] for col in range(n_cols)]
    vt_prev = vt_prev_ref[...]
    for blk in range(o_ref.shape[0] // WINDOW):
        allowed = band & ((key >= WINDOW) | (n > 0)) if blk == 0 else band
        k_prev, vt_prev = one_block(pl.ds(blk * WINDOW, WINDOW), k_prev, vt_prev, allowed)

    for col in range(n_cols):
        k_prev_ref[col] = k_prev[col]
    vt_prev_ref[...] = vt_prev


def _attention(proj, sinks, tables, gq_t, gk, wa, wb, wo, layer, batch, seq):
    m = proj.shape[0]
    rows = ATTN_BLOCKS_PER_STEP * WINDOW
    nb = seq // rows
    steps = batch * nb

    def slab(w):
        return pl.BlockSpec((None, w.shape[1] // steps, w.shape[2]),
                            lambda b, n: (layer, b * nb + n, 0))

    def slab_out(w):
        return pl.BlockSpec((w.shape[1] // steps, w.shape[2]), lambda b, n: (b * nb + n, 0))

    q_blk = PROJ_QKV_COL // ATTN_WIDTH
    kv_blk = (PROJ_QKV_COL + ATTN_WIDTH) // (2 * KV_WIDTH)
    cos, sin_signed, cos_t, sin_t = tables
    half_dim = HEAD_DIM // 2
    row_table = pl.BlockSpec((rows, V7X_LANES), lambda b, n: (n, 0))
    col_table = pl.BlockSpec((half_dim, rows), lambda b, n: (0, n))
    return pl.pallas_call(
        _attn_kernel,
        grid=(batch, nb),
        in_specs=[pl.BlockSpec(memory_space=pltpu.SMEM),
                  pl.BlockSpec((rows, ATTN_WIDTH), lambda b, n: (b * nb + n, q_blk)),
                  pl.BlockSpec((rows, 2 * KV_WIDTH), lambda b, n: (b * nb + n, kv_blk)),
                  row_table, row_table, col_table, col_table,
                  pl.BlockSpec((None, HEAD_DIM, WINDOW), lambda b, n: (layer, 0, 0)),
                  pl.BlockSpec((None, 1, V7X_LANES), lambda b, n: (layer, 0, 0)),
                  slab(wa), slab(wb), slab(wo)],
        out_specs=[pl.BlockSpec((rows, ATTN_WIDTH), lambda b, n: (b * nb + n, 0)),
                   slab_out(wa), slab_out(wb), slab_out(wo)],
        out_shape=[jax.ShapeDtypeStruct((m, ATTN_WIDTH), BF16)]
        + [jax.ShapeDtypeStruct(w.shape[1:], BF16) for w in (wa, wb, wo)],
        scratch_shapes=[pltpu.VMEM((KV_WIDTH // V7X_LANES, WINDOW, V7X_LANES), BF16),
                        pltpu.VMEM((KV_WIDTH, WINDOW), BF16)],
        compiler_params=_params(2),
        name="swa_attention",
    )(sinks, proj, proj, cos, sin_signed, cos_t, sin_t, gq_t, gk, wa, wb, wo)


def _sgu_kernel(u0_ref, u1_ref, v0_ref, v1_ref, w_ref, bt_ref, g_ref, b_ref, o_ref):
    r = lax.broadcasted_iota(jnp.int32, (CHUNK, CHUNK), 0)
    c = lax.broadcasted_iota(jnp.int32, (CHUNK, CHUNK), 1)
    causal = r >= c
    lane_mean = jnp.full((SGU_GROUP_DIM, SGU_GROUP_DIM), 1.0 / SGU_GROUP_DIM, BF16)
    groups_per_tile = TN // SGU_GROUP_DIM
    n_chunks = o_ref.shape[0] // CHUNK
    groups = range(SGU_GROUPS)

    def lanes(grp):
        return slice(grp * SGU_GROUP_DIM, (grp + 1) * SGU_GROUP_DIM)

    def tile_lanes(grp):
        return lanes(grp % groups_per_tile)

    vs = [(v0_ref, v1_ref)[grp // groups_per_tile][:, tile_lanes(grp)].astype(F32) for grp in groups]
    mus = [jnp.dot(v.astype(BF16), lane_mean, preferred_element_type=F32) for v in vs]
    dvs = [v - mu for v, mu in zip(vs, mus)]
    vars_ = [jnp.dot((dv * dv).astype(BF16), lane_mean, preferred_element_type=F32) for dv in dvs]
    ss = []
    for grp in groups:
        vn = (dvs[grp] * lax.rsqrt(vars_[grp] + EPS) * g_ref[:, lanes(grp)]
              + b_ref[:, lanes(grp)]).astype(BF16)
        w = jnp.where(causal, w_ref[grp], 0.0).astype(BF16)
        vn_chunks = [vn[ch * CHUNK:(ch + 1) * CHUNK, :] for ch in range(n_chunks)]
        ss.append(jnp.dot(w, jnp.concatenate(vn_chunks, axis=-1), preferred_element_type=F32))
    for grp in groups:
        s = ss[grp] + bt_ref[:, grp:grp + 1]
        u_ref = (u0_ref, u1_ref)[grp // groups_per_tile]
        for ch in range(n_chunks):
            u = u_ref[ch * CHUNK:(ch + 1) * CHUNK, tile_lanes(grp)].astype(F32)
            o_ref[ch * CHUNK:(ch + 1) * CHUNK, lanes(grp)] = (
                u * s[:, ch * CHUNK:(ch + 1) * CHUNK]).astype(o_ref.dtype)


def _sgu(proj, w_s, b_t, ln_g, ln_b, layer, rows=4 * CHUNK):
    m = proj.shape[0]
    u0 = PROJ_UV_TILE

    def tile(t):
        return pl.BlockSpec((rows, TN), lambda i: (i, t))

    return pl.pallas_call(
        _sgu_kernel,
        grid=(m // rows,),
        in_specs=[tile(u0), tile(u0 + 1), tile(u0 + 2), tile(u0 + 3),
                  pl.BlockSpec((None, SGU_GROUPS, CHUNK, CHUNK), lambda i: (layer, 0, 0, 0)),
                  pl.BlockSpec((None, CHUNK, SGU_GROUPS), lambda i: (layer, 0, 0)),
                  pl.BlockSpec((None, 1, SGU_WIDTH), lambda i: (layer, 0, 0)),
                  pl.BlockSpec((None, 1, SGU_WIDTH), lambda i: (layer, 0, 0))],
        out_specs=pl.BlockSpec((rows, SGU_WIDTH), lambda i: (i, 0)),
        out_shape=jax.ShapeDtypeStruct((m, SGU_WIDTH), BF16),
        compiler_params=_params(1),
        name="sgu",
    )(proj, proj, proj, proj, w_s, b_t, ln_g, ln_b)


def _rope_tables(seq):
    pos = jnp.arange(seq, dtype=F32)
    inv_freq = jnp.power(ROPE_THETA, -jnp.arange(0, HEAD_DIM, 2, dtype=F32) / HEAD_DIM)
    ang = pos[:, None] * inv_freq[None, :]
    cos, sin = jnp.cos(ang), jnp.sin(ang)
    reps = V7X_LANES // HEAD_DIM
    return (jnp.tile(cos, (1, 2 * reps)), jnp.tile(jnp.concatenate([-sin, sin], axis=-1), (1, reps)),
            cos.T, sin.T)


def kernel(x, mix_norm, w_in, q_norm, k_norm, sinks, sgu_ln_g, sgu_ln_b, w_spatial, b_spatial,
           w_attn_branch, w_sgu_branch, w_out, ffn_norm, w_gate, w_up, w_down):
    batch, seq, d = x.shape
    depth = w_in.shape[0]
    m = batch * seq
    assert d == D_MODEL and w_in.shape[-1] == (QKV_TILES + UV_TILES + GATE_TILES) * TN
    tables = _rope_tables(seq)
    gq_t = jnp.broadcast_to((q_norm * (HEAD_DIM ** -0.5 * LOG2_E))[:, :, None],
                            (depth, HEAD_DIM, WINDOW))
    gk = jnp.tile(k_norm, (1, V7X_LANES // HEAD_DIM)).reshape(depth, 1, V7X_LANES)
    b_t = jnp.swapaxes(b_spatial, 1, 2)
    mix_norm, ffn_norm, sgu_ln_g, sgu_ln_b = (
        p.reshape(depth, 1, -1) for p in (mix_norm, ffn_norm, sgu_ln_g, sgu_ln_b))

    xf = x.reshape(m, d)
    h = _rmsnorm(xf, mix_norm, 0)
    for l in range(depth):
        proj = _in_proj(h, w_in, l)
        a, wa_bf, wb_bf, wo_bf = _attention(proj, sinks[l], tables, gq_t, gk,
                                            w_attn_branch, w_sgu_branch, w_out, l, batch, seq)
        b = _sgu(proj, w_spatial, b_t, sgu_ln_g, sgu_ln_b, l)
        xf, h2 = _mix_out(a, b, proj, xf, wa_bf, wb_bf, wo_bf, ffn_norm, l)
        act, wd_bf = _swiglu(h2, w_gate, w_up, w_down, l)
        if l + 1 < depth:
            xf, h = _down_out(act, xf, wd_bf, mix_norm, l + 1)
        else:
            (xf,) = _down_out(act, xf, wd_bf, None, l)
    return xf.reshape(batch, seq, d)
```

```python
import jax
import jax.numpy as jnp
from jax import lax
from jax.experimental import pallas as pl
from jax.experimental.pallas import tpu as pltpu

F32 = jnp.float32
BF16 = jnp.bfloat16

HEAD_DIM = 64
N_Q_HEADS = 16
N_KV_HEADS = 4
Q_PER_KV = N_Q_HEADS // N_KV_HEADS
ATTN_WIDTH = N_Q_HEADS * HEAD_DIM
KV_WIDTH = N_KV_HEADS * HEAD_DIM
WINDOW = 128
ROPE_THETA = 10000.0
SGU_GROUPS = 8
SGU_GROUP_DIM = 128
SGU_WIDTH = SGU_GROUPS * SGU_GROUP_DIM
CHUNK = 128
EPS = 1e-6
MASK_VALUE = -1e30
LOG2_E = 1.4426950408889634

V7X_LANES = 128
V7X_SUBLANES_BF16 = 16
V7X_VMEM_BYTES = 64 * 1024 * 1024
VMEM_LIMIT = V7X_VMEM_BYTES - 8 * 1024 * 1024

TN = 512
D_MODEL = 2048
QKV_TILES = (ATTN_WIDTH + 2 * KV_WIDTH) // TN
UV_TILES = 2 * SGU_WIDTH // TN
GATE_TILES = 2 * D_MODEL // TN
PROJ_QKV_COL = GATE_TILES * TN
PROJ_UV_TILE = GATE_TILES + QKV_TILES


def _params(n_axes):
    return pltpu.CompilerParams(dimension_semantics=("arbitrary",) * n_axes,
                                vmem_limit_bytes=VMEM_LIMIT)


def _rmsnorm_kernel(x_ref, g_ref, o_ref):
    x = x_ref[...]
    y = x * lax.rsqrt(jnp.mean(x * x, axis=-1, keepdims=True) + EPS)
    o_ref[...] = (y * g_ref[...]).astype(o_ref.dtype)


def _rmsnorm(x, gains, layer, tm=512):
    m, d = x.shape
    return pl.pallas_call(
        _rmsnorm_kernel,
        grid=(m // tm,),
        in_specs=[pl.BlockSpec((tm, d), lambda i: (i, 0)),
                  pl.BlockSpec((None, 1, d), lambda i: (layer, 0, 0))],
        out_specs=pl.BlockSpec((tm, d), lambda i: (i, 0)),
        out_shape=jax.ShapeDtypeStruct((m, d), BF16),
        compiler_params=_params(1),
        name="rmsnorm",
    )(x, gains)


SUB_ROWS = 512


def _row_blocks(o_ref):
    tm = o_ref.shape[0]
    sub = min(SUB_ROWS, tm // 2)
    return [pl.ds(r, sub) for r in range(0, tm, sub)]


def _cast_weights(w_refs, wbf_refs):
    @pl.when(pl.program_id(1) == 0)
    def _():
        for w_ref, wbf_ref in zip(w_refs, wbf_refs):
            wbf_ref[...] = w_ref[...].astype(BF16)


def _sigmoid(y):
    return 0.5 * jnp.tanh(0.5 * y) + 0.5


def _gelu_tanh(y):
    c = 0.7978845608028654
    half = 0.5 * y
    return half + half * jnp.tanh(y * (c + (c * 0.044715) * (y * y)))


def _in_proj_kernel(h_ref, w_ref, o_ref, wbf_ref):
    _cast_weights([w_ref], [wbf_ref])
    j = pl.program_id(0)
    per_tile = TN // o_ref.shape[1]
    qkv_end, uv_end = QKV_TILES * per_tile, (QKV_TILES + UV_TILES) * per_tile

    def emit(act):
        for rows in _row_blocks(o_ref):
            y = jnp.dot(h_ref[rows, :], wbf_ref[...], preferred_element_type=F32)
            o_ref[rows, :] = (y if act is None else act(y)).astype(o_ref.dtype)

    pl.when(j < qkv_end)(lambda: emit(None))
    pl.when((j >= qkv_end) & (j < uv_end))(lambda: emit(_gelu_tanh))
    pl.when(j >= uv_end)(lambda: emit(_sigmoid))


def _in_proj(h, w, layer, tn=256):
    m, k = h.shape
    n = w.shape[-1]
    per_tile = TN // tn
    head_tiles, gate_tiles = (QKV_TILES + UV_TILES) * per_tile, GATE_TILES * per_tile
    return pl.pallas_call(
        _in_proj_kernel,
        grid=(n // tn, 1),
        in_specs=[pl.BlockSpec((m, k), lambda j, i: (0, 0)),
                  pl.BlockSpec((None, k, tn), lambda j, i: (layer, 0, j))],
        out_specs=pl.BlockSpec(
            (m, tn), lambda j, i: (0, jnp.where(j < head_tiles, j + gate_tiles, j - head_tiles))),
        out_shape=jax.ShapeDtypeStruct((m, n), BF16),
        scratch_shapes=[pltpu.VMEM((k, tn), BF16)],
        compiler_params=_params(2),
        name="in_proj",
    )(h, w)


def _swiglu_kernel(h_ref, wg_ref, wu_ref, wd_ref, o_ref, wd_bf_ref, wgbf_ref, wubf_ref):
    _cast_weights([wg_ref, wu_ref], [wgbf_ref, wubf_ref])
    wd_bf_ref[...] = wd_ref[...].astype(wd_bf_ref.dtype)
    for rows in _row_blocks(o_ref):
        h = h_ref[rows, :]
        g = jnp.dot(h, wgbf_ref[...], preferred_element_type=F32)
        u = jnp.dot(h, wubf_ref[...], preferred_element_type=F32)
        o_ref[rows, :] = (g * _sigmoid(g) * u).astype(o_ref.dtype)


def _swiglu(h, wg, wu, wd, layer, tm=2048):
    m, k = h.shape
    f = wg.shape[-1]
    mt = m // tm
    slab = wd.shape[1] // (f // TN * mt)
    w_spec = pl.BlockSpec((None, k, TN), lambda j, i: (layer, 0, j))
    return pl.pallas_call(
        _swiglu_kernel,
        grid=(f // TN, mt),
        in_specs=[pl.BlockSpec((tm, k), lambda j, i: (i, 0)), w_spec, w_spec,
                  pl.BlockSpec((None, slab, wd.shape[2]), lambda j, i: (layer, j * mt + i, 0))],
        out_specs=[pl.BlockSpec((tm, TN), lambda j, i: (i, j)),
                   pl.BlockSpec((slab, wd.shape[2]), lambda j, i: (j * mt + i, 0))],
        out_shape=[jax.ShapeDtypeStruct((m, f), BF16), jax.ShapeDtypeStruct(wd.shape[1:], BF16)],
        scratch_shapes=[pltpu.VMEM((k, TN), BF16), pltpu.VMEM((k, TN), BF16)],
        compiler_params=_params(2),
        name="swiglu",
    )(h, wg, wu, wd)


def _mix_out_kernel(a_ref, b_ref, ga_ref, gb_ref, x_ref, wa_ref, wb_ref, wo_ref, gain_ref,
                    o_ref, h_ref, merged_ref):
    tm, d = o_ref.shape
    tiles = [slice(t * TN, (t + 1) * TN) for t in range(d // TN)]
    halves = [pl.ds(r, tm // 2) for r in (0, tm // 2)]
    for rows in halves:
        for cols in tiles:
            ya = jnp.dot(a_ref[rows, :], wa_ref[:, cols], preferred_element_type=F32)
            yb = jnp.dot(b_ref[rows, :], wb_ref[:, cols], preferred_element_type=F32)
            merged_ref[rows, cols] = (ga_ref[rows, cols] * ya + gb_ref[rows, cols] * yb
                                      ).astype(merged_ref.dtype)
    for rows in halves:
        ssq = jnp.zeros((tm // 2, 1), F32)
        for cols in tiles:
            y = x_ref[rows, cols] + jnp.dot(merged_ref[rows, :], wo_ref[:, cols],
                                            preferred_element_type=F32)
            o_ref[rows, cols] = y
            ssq = ssq + jnp.sum(y * y, axis=-1, keepdims=True)
        scale = lax.rsqrt(ssq * (1.0 / d) + EPS)
        h_ref[rows, :] = (o_ref[rows, :] * scale * gain_ref[...]).astype(h_ref.dtype)


def _mix_out(a, b, proj, x, wa_bf, wb_bf, wo_bf, gains, layer, tm=512):
    m, k = a.shape
    d = x.shape[1]
    row_tile = pl.BlockSpec((tm, d), lambda i: (i, 0))

    def whole(w):
        return pl.BlockSpec(w.shape, lambda i: (0, 0))

    return pl.pallas_call(
        _mix_out_kernel,
        grid=(m // tm,),
        in_specs=[pl.BlockSpec((tm, k), lambda i: (i, 0)),
                  pl.BlockSpec((tm, k), lambda i: (i, 0)),
                  pl.BlockSpec((tm, d), lambda i: (i, 0)),
                  pl.BlockSpec((tm, d), lambda i: (i, 1)),
                  row_tile, whole(wa_bf), whole(wb_bf), whole(wo_bf),
                  pl.BlockSpec((None, 1, d), lambda i: (layer, 0, 0))],
        out_specs=[row_tile, row_tile],
        out_shape=[jax.ShapeDtypeStruct((m, d), F32), jax.ShapeDtypeStruct((m, d), BF16)],
        scratch_shapes=[pltpu.VMEM((tm, d), BF16)],
        compiler_params=_params(1),
        name="mix_out",
    )(a, b, proj, proj, x, wa_bf, wb_bf, wo_bf, gains)


def _down_out_kernel(act_ref, x_ref, w_ref, *rest):
    gain_ref, o_ref, h_ref = rest if len(rest) == 3 else (None, rest[0], None)
    tm, d = o_ref.shape
    tiles = [slice(t * TN, (t + 1) * TN) for t in range(d // TN)]
    for rows in (pl.ds(r, tm // 2) for r in (0, tm // 2)):
        ssq = jnp.zeros((tm // 2, 1), F32)
        for cols in tiles:
            y = x_ref[rows, cols] + jnp.dot(act_ref[rows, :], w_ref[:, cols],
                                            preferred_element_type=F32)
            o_ref[rows, cols] = y
            if h_ref is not None:
                ssq = ssq + jnp.sum(y * y, axis=-1, keepdims=True)
        if h_ref is not None:
            scale = lax.rsqrt(ssq * (1.0 / d) + EPS)
            h_ref[rows, :] = (o_ref[rows, :] * scale * gain_ref[...]).astype(h_ref.dtype)


def _down_out(act, x, w_bf, gains, layer, tm=512):
    m, k = act.shape
    d = x.shape[1]
    row_tile = pl.BlockSpec((tm, d), lambda i: (i, 0))
    in_specs = [pl.BlockSpec((tm, k), lambda i: (i, 0)), row_tile,
                pl.BlockSpec(w_bf.shape, lambda i: (0, 0))]
    operands = [act, x, w_bf]
    out_specs, out_shape = [row_tile], [jax.ShapeDtypeStruct((m, d), F32)]
    if gains is not None:
        in_specs.append(pl.BlockSpec((None, 1, d), lambda i: (layer, 0, 0)))
        operands.append(gains)
        out_specs.append(row_tile)
        out_shape.append(jax.ShapeDtypeStruct((m, d), BF16))
    return pl.pallas_call(
        _down_out_kernel,
        grid=(m // tm,),
        in_specs=in_specs,
        out_specs=out_specs,
        out_shape=out_shape,
        compiler_params=_params(1),
        name="down_out" if gains is not None else "down_out_last",
    )(*operands)


ATTN_BLOCKS_PER_STEP = 4


def _attn_kernel(sinks_ref, q_ref, kv_ref, cos_ref, sin_ref, cos_t_ref, sin_t_ref, gq_t_ref, gk_ref,
                 wa_ref, wb_ref, wo_ref, o_ref, wa_bf_ref, wb_bf_ref, wo_bf_ref,
                 k_prev_ref, vt_prev_ref):
    n = pl.program_id(1)
    half_dim = HEAD_DIM // 2

    for src, dst in ((wa_ref, wa_bf_ref), (wb_ref, wb_bf_ref), (wo_ref, wo_bf_ref)):
        dst[...] = src[...].astype(dst.dtype)

    @pl.when(n == 0)
    def _():
        k_prev_ref[...] = jnp.zeros_like(k_prev_ref)
        vt_prev_ref[...] = jnp.zeros_like(vt_prev_ref)

    lane = lax.broadcasted_iota(jnp.int32, (1, V7X_LANES), 1)
    even_quarter = ((lane // half_dim) % 2) == 0
    r = lax.broadcasted_iota(jnp.int32, (V7X_LANES, V7X_LANES), 0)
    c = lax.broadcasted_iota(jnp.int32, (V7X_LANES, V7X_LANES), 1)
    head_mean = jnp.where((r // HEAD_DIM) == (c // HEAD_DIM), 1.0 / HEAD_DIM, 0.0).astype(BF16)
    key = lax.broadcasted_iota(jnp.int32, (2 * WINDOW, WINDOW), 0)
    qry = lax.broadcasted_iota(jnp.int32, (2 * WINDOW, WINDOW), 1)
    diff = qry + WINDOW - key
    band = (diff >= 0) & (diff < WINDOW)
    gq_t = gq_t_ref[...]
    zeros = jnp.zeros((HEAD_DIM, WINDOW), BF16)
    n_cols = KV_WIDTH // V7X_LANES
    heads_per_col = N_Q_HEADS // n_cols
    group_w = Q_PER_KV * WINDOW
    sink_row = [jnp.full((1, WINDOW), sinks_ref[h] * LOG2_E, F32) for h in range(N_Q_HEADS)]

    def one_block(rows, k_prev, vt_prev, allowed):
        cos, sin = cos_ref[rows, :], sin_ref[rows, :]
        k_cols = []
        for col in range(n_cols):
            x = kv_ref[rows, col * V7X_LANES:(col + 1) * V7X_LANES].astype(F32)
            ms = jnp.dot((x * x).astype(BF16), head_mean, preferred_element_type=F32)
            y = x * lax.rsqrt(ms + EPS) * gk_ref[...]
            partner = jnp.where(even_quarter,
                                pltpu.roll(y, V7X_LANES - half_dim, 1),
                                pltpu.roll(y, half_dim, 1))
            k_cols.append((y * cos + partner * sin).astype(BF16))
        k_win = [jnp.concatenate([k_prev[col], k_cols[col]], axis=0) for col in range(n_cols)]
        vt_cur = kv_ref[rows, KV_WIDTH:].astype(F32).T.astype(BF16)
        vt_win = jnp.concatenate([vt_prev, vt_cur], axis=1)

        cos_t, sin_t = cos_t_ref[:, rows], sin_t_ref[:, rows]
        qt_pads = []
        for pair in range(N_Q_HEADS // 2):
            sl = slice(pair * V7X_LANES, (pair + 1) * V7X_LANES)
            qt_pair = q_ref[rows, sl].astype(F32).T
            for half in range(2):
                kv_head = (2 * pair + half) // Q_PER_KV
                x = qt_pair[half * HEAD_DIM:(half + 1) * HEAD_DIM, :]
                y = x * lax.rsqrt(jnp.mean(x * x, axis=0, keepdims=True) + EPS) * gq_t
                y1, y2 = y[:half_dim], y[half_dim:]
                qt = jnp.concatenate([y1 * cos_t - y2 * sin_t, y2 * cos_t + y1 * sin_t],
                                     axis=0).astype(BF16)
                qt_pads.append(jnp.concatenate([zeros, qt] if kv_head % 2 else [qt, zeros], axis=0))

        probs, sink_terms = [], []
        for col in range(n_cols):
            heads = range(col * heads_per_col, (col + 1) * heads_per_col)
            s = jnp.dot(k_win[col], jnp.concatenate([qt_pads[h] for h in heads], axis=1),
                        preferred_element_type=F32)
            s = jnp.where(jnp.concatenate([allowed] * heads_per_col, axis=1), s, MASK_VALUE)
            sink = jnp.concatenate([sink_row[h] for h in heads], axis=1)
            m = jnp.maximum(jnp.max(s, axis=0, keepdims=True), sink)
            probs.append(jnp.exp2(s - m).astype(BF16))
            sink_terms.append(jnp.exp2(sink - m))

        ones_rows = jnp.ones((V7X_SUBLANES_BF16, 2 * WINDOW), BF16)
        for kv_head in range(N_KV_HEADS):
            col, part = divmod(kv_head, N_KV_HEADS // n_cols)
            lanes = slice(part * group_w, (part + 1) * group_w)
            vt = jnp.concatenate([vt_win[kv_head * HEAD_DIM:(kv_head + 1) * HEAD_DIM, :], ones_rows],
                                 axis=0)
            o = jnp.dot(vt, probs[col][:, lanes], preferred_element_type=F32)
            denom = o[HEAD_DIM:HEAD_DIM + 1, :] + sink_terms[col][:, lanes]
            o = o[:HEAD_DIM, :] * (1.0 / denom)
            for pair in range(Q_PER_KV // 2):
                o_pair = jnp.concatenate([o[:, (2 * pair) * WINDOW:(2 * pair + 1) * WINDOW],
                                          o[:, (2 * pair + 1) * WINDOW:(2 * pair + 2) * WINDOW]], axis=0)
                out_lane = (kv_head * Q_PER_KV + 2 * pair) * HEAD_DIM
                o_ref[rows, out_lane:out_lane + V7X_LANES] = o_pair.T.astype(o_ref.dtype)
        return k_cols, vt_cur

    k_prev = [k_prev_ref[col] for col in range(n_cols)]
    vt_prev = vt_prev_ref[...]
    for blk in range(o_ref.shape[0] // WINDOW):
        allowed = band & ((key >= WINDOW) | (n > 0)) if blk == 0 else band
        k_prev, vt_prev = one_block(pl.ds(blk * WINDOW, WINDOW), k_prev, vt_prev, allowed)

    for col in range(n_cols):
        k_prev_ref[col] = k_prev[col]
    vt_prev_ref[...] = vt_prev


def _attention(proj, sinks, tables, gq_t, gk, wa, wb, wo, layer, batch, seq):
    m = proj.shape[0]
    rows = ATTN_BLOCKS_PER_STEP * WINDOW
    nb = seq // rows
    steps = batch * nb

    def slab(w):
        return pl.BlockSpec((None, w.shape[1] // steps, w.shape[2]),
                            lambda b, n: (layer, b * nb + n, 0))

    def slab_out(w):
        return pl.BlockSpec((w.shape[1] // steps, w.shape[2]), lambda b, n: (b * nb + n, 0))

    q_blk = PROJ_QKV_COL // ATTN_WIDTH
    kv_blk = (PROJ_QKV_COL + ATTN_WIDTH) // (2 * KV_WIDTH)
    cos, sin_signed, cos_t, sin_t = tables
    half_dim = HEAD_DIM // 2
    row_table = pl.BlockSpec((rows, V7X_LANES), lambda b, n: (n, 0))
    col_table = pl.BlockSpec((half_dim, rows), lambda b, n: (0, n))
    return pl.pallas_call(
        _attn_kernel,
        grid=(batch, nb),
        in_specs=[pl.BlockSpec(memory_space=pltpu.SMEM),
                  pl.BlockSpec((rows, ATTN_WIDTH), lambda b, n: (b * nb + n, q_blk)),
                  pl.BlockSpec((rows, 2 * KV_WIDTH), lambda b, n: (b * nb + n, kv_blk)),
                  row_table, row_table, col_table, col_table,
                  pl.BlockSpec((None, HEAD_DIM, WINDOW), lambda b, n: (layer, 0, 0)),
                  pl.BlockSpec((None, 1, V7X_LANES), lambda b, n: (layer, 0, 0)),
                  slab(wa), slab(wb), slab(wo)],
        out_specs=[pl.BlockSpec((rows, ATTN_WIDTH), lambda b, n: (b * nb + n, 0)),
                   slab_out(wa), slab_out(wb), slab_out(wo)],
        out_shape=[jax.ShapeDtypeStruct((m, ATTN_WIDTH), BF16)]
        + [jax.ShapeDtypeStruct(w.shape[1:], BF16) for w in (wa, wb, wo)],
        scratch_shapes=[pltpu.VMEM((KV_WIDTH // V7X_LANES, WINDOW, V7X_LANES), BF16),
                        pltpu.VMEM((KV_WIDTH, WINDOW), BF16)],
        compiler_params=_params(2),
        name="swa_attention",
    )(sinks, proj, proj, cos, sin_signed, cos_t, sin_t, gq_t, gk, wa, wb, wo)


def _sgu_kernel(u0_ref, u1_ref, v0_ref, v1_ref, w_ref, bt_ref, g_ref, b_ref, o_ref):
    r = lax.broadcasted_iota(jnp.int32, (CHUNK, CHUNK), 0)
    c = lax.broadcasted_iota(jnp.int32, (CHUNK, CHUNK), 1)
    causal = r >= c
    lane_mean = jnp.full((SGU_GROUP_DIM, SGU_GROUP_DIM), 1.0 / SGU_GROUP_DIM, BF16)
    groups_per_tile = TN // SGU_GROUP_DIM
    n_chunks = o_ref.shape[0] // CHUNK
    groups = range(SGU_GROUPS)

    def lanes(grp):
        return slice(grp * SGU_GROUP_DIM, (grp + 1) * SGU_GROUP_DIM)

    def tile_lanes(grp):
        return lanes(grp % groups_per_tile)

    vs = [(v0_ref, v1_ref)[grp // groups_per_tile][:, tile_lanes(grp)].astype(F32) for grp in groups]
    mus = [jnp.dot(v.astype(BF16), lane_mean, preferred_element_type=F32) for v in vs]
    dvs = [v - mu for v, mu in zip(vs, mus)]
    vars_ = [jnp.dot((dv * dv).astype(BF16), lane_mean, preferred_element_type=F32) for dv in dvs]
    ss = []
    for grp in groups:
        vn = (dvs[grp] * lax.rsqrt(vars_[grp] + EPS) * g_ref[:, lanes(grp)]
              + b_ref[:, lanes(grp)]).astype(BF16)
        w = jnp.where(causal, w_ref[grp], 0.0).astype(BF16)
        vn_chunks = [vn[ch * CHUNK:(ch + 1) * CHUNK, :] for ch in range(n_chunks)]
        ss.append(jnp.dot(w, jnp.concatenate(vn_chunks, axis=-1), preferred_element_type=F32))
    for grp in groups:
        s = ss[grp] + bt_ref[:, grp:grp + 1]
        u_ref = (u0_ref, u1_ref)[grp // groups_per_tile]
        for ch in range(n_chunks):
            u = u_ref[ch * CHUNK:(ch + 1) * CHUNK, tile_lanes(grp)].astype(F32)
            o_ref[ch * CHUNK:(ch + 1) * CHUNK, lanes(grp)] = (
                u * s[:, ch * CHUNK:(ch + 1) * CHUNK]).astype(o_ref.dtype)


def _sgu(proj, w_s, b_t, ln_g, ln_b, layer, rows=4 * CHUNK):
    m = proj.shape[0]
    u0 = PROJ_UV_TILE

    def tile(t):
        return pl.BlockSpec((rows, TN), lambda i: (i, t))

    return pl.pallas_call(
        _sgu_kernel,
        grid=(m // rows,),
        in_specs=[tile(u0), tile(u0 + 1), tile(u0 + 2), tile(u0 + 3),
                  pl.BlockSpec((None, SGU_GROUPS, CHUNK, CHUNK), lambda i: (layer, 0, 0, 0)),
                  pl.BlockSpec((None, CHUNK, SGU_GROUPS), lambda i: (layer, 0, 0)),
                  pl.BlockSpec((None, 1, SGU_WIDTH), lambda i: (layer, 0, 0)),
                  pl.BlockSpec((None, 1, SGU_WIDTH), lambda i: (layer, 0, 0))],
        out_specs=pl.BlockSpec((rows, SGU_WIDTH), lambda i: (i, 0)),
        out_shape=jax.ShapeDtypeStruct((m, SGU_WIDTH), BF16),
        compiler_params=_params(1),
        name="sgu",
    )(proj, proj, proj, proj, w_s, b_t, ln_g, ln_b)


def _rope_tables(seq):
    pos = jnp.arange(seq, dtype=F32)
    inv_freq = jnp.power(ROPE_THETA, -jnp.arange(0, HEAD_DIM, 2, dtype=F32) / HEAD_DIM)
    ang = pos[:, None] * inv_freq[None, :]
    cos, sin = jnp.cos(ang), jnp.sin(ang)
    reps = V7X_LANES // HEAD_DIM
    return (jnp.tile(cos, (1, 2 * reps)), jnp.tile(jnp.concatenate([-sin, sin], axis=-1), (1, reps)),
            cos.T, sin.T)


def kernel(x, mix_norm, w_in, q_norm, k_norm, sinks, sgu_ln_g, sgu_ln_b, w_spatial, b_spatial,
           w_attn_branch, w_sgu_branch, w_out, ffn_norm, w_gate, w_up, w_down):
    batch, seq, d = x.shape
    depth = w_in.shape[0]
    m = batch * seq
    assert d == D_MODEL and w_in.shape[-1] == (QKV_TILES + UV_TILES + GATE_TILES) * TN
    tables = _rope_tables(seq)
    gq_t = jnp.broadcast_to((q_norm * (HEAD_DIM ** -0.5 * LOG2_E))[:, :, None],
                            (depth, HEAD_DIM, WINDOW))
    gk = jnp.tile(k_norm, (1, V7X_LANES // HEAD_DIM)).reshape(depth, 1, V7X_LANES)
    b_t = jnp.swapaxes(b_spatial, 1, 2)
    mix_norm, ffn_norm, sgu_ln_g, sgu_ln_b = (
        p.reshape(depth, 1, -1) for p in (mix_norm, ffn_norm, sgu_ln_g, sgu_ln_b))

    xf = x.reshape(m, d)
    h = _rmsnorm(xf, mix_norm, 0)
    for l in range(depth):
        proj = _in_proj(h, w_in, l)
        a, wa_bf, wb_bf, wo_bf = _attention(proj, sinks[l], tables, gq_t, gk,
                                            w_attn_branch, w_sgu_branch, w_out, l, batch, seq)
        b = _sgu(proj, w_spatial, b_t, sgu_ln_g, sgu_ln_b, l)
        xf, h2 = _mix_out(a, b, proj, xf, wa_bf, wb_bf, wo_bf, ffn_norm, l)
        act, wd_bf = _swiglu(h2, w_gate, w_up, w_down, l)
        if l + 1 < depth:
            xf, h = _down_out(act, xf, wd_bf, mix_norm, l + 1)
        else:
            (xf,) = _down_out(act, xf, wd_bf, None, l)
    return xf.reshape(batch, seq, d)
```

```python
import jax
import jax.numpy as jnp
from jax import lax
from jax.experimental import pallas as pl
from jax.experimental.pallas import tpu as pltpu

F32 = jnp.float32
BF16 = jnp.bfloat16

HEAD_DIM = 64
N_Q_HEADS = 16
N_KV_HEADS = 4
Q_PER_KV = N_Q_HEADS // N_KV_HEADS
ATTN_WIDTH = N_Q_HEADS * HEAD_DIM
KV_WIDTH = N_KV_HEADS * HEAD_DIM
WINDOW = 128
ROPE_THETA = 10000.0
SGU_GROUPS = 8
SGU_GROUP_DIM = 128
SGU_WIDTH = SGU_GROUPS * SGU_GROUP_DIM
CHUNK = 128
EPS = 1e-6
MASK_VALUE = -1e30
LOG2_E = 1.4426950408889634

V7X_LANES = 128
V7X_SUBLANES_BF16 = 16
V7X_VMEM_BYTES = 64 * 1024 * 1024
VMEM_LIMIT = V7X_VMEM_BYTES - 8 * 1024 * 1024

TN = 512
D_MODEL = 2048
QKV_TILES = (ATTN_WIDTH + 2 * KV_WIDTH) // TN
UV_TILES = 2 * SGU_WIDTH // TN
GATE_TILES = 2 * D_MODEL // TN
PROJ_QKV_COL = GATE_TILES * TN
PROJ_UV_TILE = GATE_TILES + QKV_TILES


def _params(n_axes):
    return pltpu.CompilerParams(dimension_semantics=("arbitrary",) * n_axes,
                                vmem_limit_bytes=VMEM_LIMIT)


def _rmsnorm_kernel(x_ref, g_ref, o_ref):
    x = x_ref[...]
    y = x * lax.rsqrt(jnp.mean(x * x, axis=-1, keepdims=True) + EPS)
    o_ref[...] = (y * g_ref[...]).astype(o_ref.dtype)


def _rmsnorm(x, gains, layer, tm=512):
    m, d = x.shape
    return pl.pallas_call(
        _rmsnorm_kernel,
        grid=(m // tm,),
        in_specs=[pl.BlockSpec((tm, d), lambda i: (i, 0)),
                  pl.BlockSpec((None, 1, d), lambda i: (layer, 0, 0))],
        out_specs=pl.BlockSpec((tm, d), lambda i: (i, 0)),
        out_shape=jax.ShapeDtypeStruct((m, d), BF16),
        compiler_params=_params(1),
        name="rmsnorm",
    )(x, gains)


SUB_ROWS = 512


def _row_blocks(o_ref):
    tm = o_ref.shape[0]
    sub = min(SUB_ROWS, tm // 2)
    return [pl.ds(r, sub) for r in range(0, tm, sub)]


def _cast_weights(w_refs, wbf_refs):
    @pl.when(pl.program_id(1) == 0)
    def _():
        for w_ref, wbf_ref in zip(w_refs, wbf_refs):
            wbf_ref[...] = w_ref[...].astype(BF16)


def _sigmoid(y):
    return 0.5 * jnp.tanh(0.5 * y) + 0.5


def _gelu_tanh(y):
    c = 0.7978845608028654
    half = 0.5 * y
    return half + half * jnp.tanh(y * (c + (c * 0.044715) * (y * y)))


def _in_proj_kernel(h_ref, w_ref, o_ref, wbf_ref):
    _cast_weights([w_ref], [wbf_ref])
    j = pl.program_id(0)

    def emit(act):
        for rows in _row_blocks(o_ref):
            y = jnp.dot(h_ref[rows, :], wbf_ref[...], preferred_element_type=F32)
            o_ref[rows, :] = (y if act is None else act(y)).astype(o_ref.dtype)

    pl.when(j < QKV_TILES)(lambda: emit(None))
    pl.when((j >= QKV_TILES) & (j < QKV_TILES + UV_TILES))(
        lambda: emit(_gelu_tanh))
    pl.when(j >= QKV_TILES + UV_TILES)(lambda: emit(_sigmoid))


def _in_proj(h, w, layer, tm=4096):
    m, k = h.shape
    n = w.shape[-1]
    return pl.pallas_call(
        _in_proj_kernel,
        grid=(n // TN, m // tm),
        in_specs=[pl.BlockSpec((tm, k), lambda j, i: (i, 0)),
                  pl.BlockSpec((None, k, TN), lambda j, i: (layer, 0, j))],
        out_specs=pl.BlockSpec(
            (tm, TN), lambda j, i: (i, jnp.where(j < QKV_TILES + UV_TILES, j + GATE_TILES,
                                                 j - (QKV_TILES + UV_TILES)))),
        out_shape=jax.ShapeDtypeStruct((m, n), BF16),
        scratch_shapes=[pltpu.VMEM((k, TN), BF16)],
        compiler_params=_params(2),
        name="in_proj",
    )(h, w)


def _swiglu_kernel(h_ref, wg_ref, wu_ref, wd_ref, o_ref, wd_bf_ref, wgbf_ref, wubf_ref):
    _cast_weights([wg_ref, wu_ref], [wgbf_ref, wubf_ref])
    wd_bf_ref[...] = wd_ref[...].astype(wd_bf_ref.dtype)
    for rows in _row_blocks(o_ref):
        h = h_ref[rows, :]
        g = jnp.dot(h, wgbf_ref[...], preferred_element_type=F32)
        u = jnp.dot(h, wubf_ref[...], preferred_element_type=F32)
        o_ref[rows, :] = (g * _sigmoid(g) * u).astype(o_ref.dtype)


def _swiglu(h, wg, wu, wd, layer, tm=2048):
    m, k = h.shape
    f = wg.shape[-1]
    mt = m // tm
    slab = wd.shape[1] // (f // TN * mt)
    w_spec = pl.BlockSpec((None, k, TN), lambda j, i: (layer, 0, j))
    return pl.pallas_call(
        _swiglu_kernel,
        grid=(f // TN, mt),
        in_specs=[pl.BlockSpec((tm, k), lambda j, i: (i, 0)), w_spec, w_spec,
                  pl.BlockSpec((None, slab, wd.shape[2]), lambda j, i: (layer, j * mt + i, 0))],
        out_specs=[pl.BlockSpec((tm, TN), lambda j, i: (i, j)),
                   pl.BlockSpec((slab, wd.shape[2]), lambda j, i: (j * mt + i, 0))],
        out_shape=[jax.ShapeDtypeStruct((m, f), BF16), jax.ShapeDtypeStruct(wd.shape[1:], BF16)],
        scratch_shapes=[pltpu.VMEM((k, TN), BF16), pltpu.VMEM((k, TN), BF16)],
        compiler_params=_params(2),
        name="swiglu",
    )(h, wg, wu, wd)


def _mix_out_kernel(a_ref, b_ref, ga_ref, gb_ref, x_ref, wa_ref, wb_ref, wo_ref, gain_ref,
                    o_ref, h_ref, merged_ref):
    tm, d = o_ref.shape
    tiles = [slice(t * TN, (t + 1) * TN) for t in range(d // TN)]
    halves = [pl.ds(r, tm // 2) for r in (0, tm // 2)]
    for rows in halves:
        for cols in tiles:
            ya = jnp.dot(a_ref[rows, :], wa_ref[:, cols], preferred_element_type=F32)
            yb = jnp.dot(b_ref[rows, :], wb_ref[:, cols], preferred_element_type=F32)
            merged_ref[rows, cols] = (ga_ref[rows, cols] * ya + gb_ref[rows, cols] * yb
                                      ).astype(merged_ref.dtype)
    for rows in halves:
        ssq = jnp.zeros((tm // 2, 1), F32)
        for cols in tiles:
            y = x_ref[rows, cols] + jnp.dot(merged_ref[rows, :], wo_ref[:, cols],
                                            preferred_element_type=F32)
            o_ref[rows, cols] = y
            ssq = ssq + jnp.sum(y * y, axis=-1, keepdims=True)
        scale = lax.rsqrt(ssq * (1.0 / d) + EPS)
        h_ref[rows, :] = (o_ref[rows, :] * scale * gain_ref[...]).astype(h_ref.dtype)


def _mix_out(a, b, proj, x, wa_bf, wb_bf, wo_bf, gains, layer, tm=512):
    m, k = a.shape
    d = x.shape[1]
    row_tile = pl.BlockSpec((tm, d), lambda i: (i, 0))

    def whole(w):
        return pl.BlockSpec(w.shape, lambda i: (0, 0))

    return pl.pallas_call(
        _mix_out_kernel,
        grid=(m // tm,),
        in_specs=[pl.BlockSpec((tm, k), lambda i: (i, 0)),
                  pl.BlockSpec((tm, k), lambda i: (i, 0)),
                  pl.BlockSpec((tm, d), lambda i: (i, 0)),
                  pl.BlockSpec((tm, d), lambda i: (i, 1)),
                  row_tile, whole(wa_bf), whole(wb_bf), whole(wo_bf),
                  pl.BlockSpec((None, 1, d), lambda i: (layer, 0, 0))],
        out_specs=[row_tile, row_tile],
        out_shape=[jax.ShapeDtypeStruct((m, d), F32), jax.ShapeDtypeStruct((m, d), BF16)],
        scratch_shapes=[pltpu.VMEM((tm, d), BF16)],
        compiler_params=_params(1),
        name="mix_out",
    )(a, b, proj, proj, x, wa_bf, wb_bf, wo_bf, gains)


def _down_out_kernel(act_ref, x_ref, w_ref, *rest):
    gain_ref, o_ref, h_ref = rest if len(rest) == 3 else (None, rest[0], None)
    tm, d = o_ref.shape
    tiles = [slice(t * TN, (t + 1) * TN) for t in range(d // TN)]
    for rows in (pl.ds(r, tm // 2) for r in (0, tm // 2)):
        ssq = jnp.zeros((tm // 2, 1), F32)
        for cols in tiles:
            y = x_ref[rows, cols] + jnp.dot(act_ref[rows, :], w_ref[:, cols],
                                            preferred_element_type=F32)
            o_ref[rows, cols] = y
            if h_ref is not None:
                ssq = ssq + jnp.sum(y * y, axis=-1, keepdims=True)
        if h_ref is not None:
            scale = lax.rsqrt(ssq * (1.0 / d) + EPS)
            h_ref[rows, :] = (o_ref[rows, :] * scale * gain_ref[...]).astype(h_ref.dtype)


def _down_out(act, x, w_bf, gains, layer, tm=512):
    m, k = act.shape
    d = x.shape[1]
    row_tile = pl.BlockSpec((tm, d), lambda i: (i, 0))
    in_specs = [pl.BlockSpec((tm, k), lambda i: (i, 0)), row_tile,
                pl.BlockSpec(w_bf.shape, lambda i: (0, 0))]
    operands = [act, x, w_bf]
    out_specs, out_shape = [row_tile], [jax.ShapeDtypeStruct((m, d), F32)]
    if gains is not None:
        in_specs.append(pl.BlockSpec((None, 1, d), lambda i: (layer, 0, 0)))
        operands.append(gains)
        out_specs.append(row_tile)
        out_shape.append(jax.ShapeDtypeStruct((m, d), BF16))
    return pl.pallas_call(
        _down_out_kernel,
        grid=(m // tm,),
        in_specs=in_specs,
        out_specs=out_specs,
        out_shape=out_shape,
        compiler_params=_params(1),
        name="down_out" if gains is not None else "down_out_last",
    )(*operands)


ATTN_BLOCKS_PER_STEP = 8


def _attn_kernel(sinks_ref, q_ref, kv_ref, cos_ref, sin_ref, cos_t_ref, sin_t_ref, gq_t_ref, gk_ref,
                 wa_ref, wb_ref, wo_ref, o_ref, wa_bf_ref, wb_bf_ref, wo_bf_ref,
                 k_prev_ref, vt_prev_ref):
    n = pl.program_id(1)
    half_dim = HEAD_DIM // 2

    for src, dst in ((wa_ref, wa_bf_ref), (wb_ref, wb_bf_ref), (wo_ref, wo_bf_ref)):
        dst[...] = src[...].astype(dst.dtype)

    @pl.when(n == 0)
    def _():
        k_prev_ref[...] = jnp.zeros_like(k_prev_ref)
        vt_prev_ref[...] = jnp.zeros_like(vt_prev_ref)

    lane = lax.broadcasted_iota(jnp.int32, (1, V7X_LANES), 1)
    even_quarter = ((lane // half_dim) % 2) == 0
    r = lax.broadcasted_iota(jnp.int32, (V7X_LANES, V7X_LANES), 0)
    c = lax.broadcasted_iota(jnp.int32, (V7X_LANES, V7X_LANES), 1)
    head_mean = jnp.where((r // HEAD_DIM) == (c // HEAD_DIM), 1.0 / HEAD_DIM, 0.0).astype(BF16)
    key = lax.broadcasted_iota(jnp.int32, (2 * WINDOW, WINDOW), 0)
    qry = lax.broadcasted_iota(jnp.int32, (2 * WINDOW, WINDOW), 1)
    diff = qry + WINDOW - key
    band = (diff >= 0) & (diff < WINDOW)
    gq_t = gq_t_ref[...]
    zeros = jnp.zeros((HEAD_DIM, WINDOW), BF16)
    n_cols = KV_WIDTH // V7X_LANES
    heads_per_col = N_Q_HEADS // n_cols
    group_w = Q_PER_KV * WINDOW
    sink_row = [jnp.full((1, WINDOW), sinks_ref[h] * LOG2_E, F32) for h in range(N_Q_HEADS)]

    def one_block(rows, k_prev, vt_prev, allowed):
        cos, sin = cos_ref[rows, :], sin_ref[rows, :]
        k_cols = []
        for col in range(n_cols):
            x = kv_ref[rows, col * V7X_LANES:(col + 1) * V7X_LANES].astype(F32)
            ms = jnp.dot((x * x).astype(BF16), head_mean, preferred_element_type=F32)
            y = x * lax.rsqrt(ms + EPS) * gk_ref[...]
            partner = jnp.where(even_quarter,
                                pltpu.roll(y, V7X_LANES - half_dim, 1),
                                pltpu.roll(y, half_dim, 1))
            k_cols.append((y * cos + partner * sin).astype(BF16))
        k_win = [jnp.concatenate([k_prev[col], k_cols[col]], axis=0) for col in range(n_cols)]
        vt_cur = kv_ref[rows, KV_WIDTH:].astype(F32).T.astype(BF16)
        vt_win = jnp.concatenate([vt_prev, vt_cur], axis=1)

        cos_t, sin_t = cos_t_ref[:, rows], sin_t_ref[:, rows]
        qt_pads = []
        for pair in range(N_Q_HEADS // 2):
            sl = slice(pair * V7X_LANES, (pair + 1) * V7X_LANES)
            qt_pair = q_ref[rows, sl].astype(F32).T
            for half in range(2):
                kv_head = (2 * pair + half) // Q_PER_KV
                x = qt_pair[half * HEAD_DIM:(half + 1) * HEAD_DIM, :]
                y = x * lax.rsqrt(jnp.mean(x * x, axis=0, keepdims=True) + EPS) * gq_t
                y1, y2 = y[:half_dim], y[half_dim:]
                qt = jnp.concatenate([y1 * cos_t - y2 * sin_t, y2 * cos_t + y1 * sin_t],
                                     axis=0).astype(BF16)
                qt_pads.append(jnp.concatenate([zeros, qt] if kv_head % 2 else [qt, zeros], axis=0))

        probs, sink_terms = [], []
        for col in range(n_cols):
            heads = range(col * heads_per_col, (col + 1) * heads_per_col)
            s = jnp.dot(k_win[col], jnp.concatenate([qt_pads[h] for h in heads], axis=1),
                        preferred_element_type=F32)
            s = jnp.where(jnp.concatenate([allowed] * heads_per_col, axis=1), s, MASK_VALUE)
            sink = jnp.concatenate([sink_row[h] for h in heads], axis=1)
            m = jnp.maximum(jnp.max(s, axis=0, keepdims=True), sink)
            probs.append(jnp.exp2(s - m).astype(BF16))
            sink_terms.append(jnp.exp2(sink - m))

        ones_rows = jnp.ones((V7X_SUBLANES_BF16, 2 * WINDOW), BF16)
        for kv_head in range(N_KV_HEADS):
            col, part = divmod(kv_head, N_KV_HEADS // n_cols)
            lanes = slice(part * group_w, (part + 1) * group_w)
            vt = jnp.concatenate([vt_win[kv_head * HEAD_DIM:(kv_head + 1) * HEAD_DIM, :], ones_rows],
                                 axis=0)
            o = jnp.dot(vt, probs[col][:, lanes], preferred_element_type=F32)
            denom = o[HEAD_DIM:HEAD_DIM + 1, :] + sink_terms[col][:, lanes]
            o = o[:HEAD_DIM, :] * (1.0 / denom)
            for pair in range(Q_PER_KV // 2):
                o_pair = jnp.concatenate([o[:, (2 * pair) * WINDOW:(2 * pair + 1) * WINDOW],
                                          o[:, (2 * pair + 1) * WINDOW:(2 * pair + 2) * WINDOW]], axis=0)
                out_lane = (kv_head * Q_PER_KV + 2 * pair) * HEAD_DIM
                o_ref[rows, out_lane:out_lane + V7X_LANES] = o_pair.T.astype(o_ref.dtype)
        return k_cols, vt_cur

    k_prev = [k_prev_ref[col] for col in range(n_cols)]
    vt_prev = vt_prev_ref[...]
    for blk in range(o_ref.shape[0] // WINDOW):
        allowed = band & ((key >= WINDOW) | (n > 0)) if blk == 0 else band
        k_prev, vt_prev = one_block(pl.ds(blk * WINDOW, WINDOW), k_prev, vt_prev, allowed)

    for col in range(n_cols):
        k_prev_ref[col] = k_prev[col]
    vt_prev_ref[...] = vt_prev


def _attention(proj, sinks, tables, gq_t, gk, wa, wb, wo, layer, batch, seq):
    m = proj.shape[0]
    rows = ATTN_BLOCKS_PER_STEP * WINDOW
    nb = seq // rows
    steps = batch * nb

    def slab(w):
        return pl.BlockSpec((None, w.shape[1] // steps, w.shape[2]),
                            lambda b, n: (layer, b * nb + n, 0))

    def slab_out(w):
        return pl.BlockSpec((w.shape[1] // steps, w.shape[2]), lambda b, n: (b * nb + n, 0))

    q_blk = PROJ_QKV_COL // ATTN_WIDTH
    kv_blk = (PROJ_QKV_COL + ATTN_WIDTH) // (2 * KV_WIDTH)
    cos, sin_signed, cos_t, sin_t = tables
    half_dim = HEAD_DIM // 2
    row_table = pl.BlockSpec((rows, V7X_LANES), lambda b, n: (n, 0))
    col_table = pl.BlockSpec((half_dim, rows), lambda b, n: (0, n))
    return pl.pallas_call(
        _attn_kernel,
        grid=(batch, nb),
        in_specs=[pl.BlockSpec(memory_space=pltpu.SMEM),
                  pl.BlockSpec((rows, ATTN_WIDTH), lambda b, n: (b * nb + n, q_blk)),
                  pl.BlockSpec((rows, 2 * KV_WIDTH), lambda b, n: (b * nb + n, kv_blk)),
                  row_table, row_table, col_table, col_table,
                  pl.BlockSpec((None, HEAD_DIM, WINDOW), lambda b, n: (layer, 0, 0)),
                  pl.BlockSpec((None, 1, V7X_LANES), lambda b, n: (layer, 0, 0)),
                  slab(wa), slab(wb), slab(wo)],
        out_specs=[pl.BlockSpec((rows, ATTN_WIDTH), lambda b, n: (b * nb + n, 0)),
                   slab_out(wa), slab_out(wb), slab_out(wo)],
        out_shape=[jax.ShapeDtypeStruct((m, ATTN_WIDTH), BF16)]
        + [jax.ShapeDtypeStruct(w.shape[1:], BF16) for w in (wa, wb, wo)],
        scratch_shapes=[pltpu.VMEM((KV_WIDTH // V7X_LANES, WINDOW, V7X_LANES), BF16),
                        pltpu.VMEM((KV_WIDTH, WINDOW), BF16)],
        compiler_params=_params(2),
        name="swa_attention",
    )(sinks, proj, proj, cos, sin_signed, cos_t, sin_t, gq_t, gk, wa, wb, wo)


def _sgu_kernel(u0_ref, u1_ref, v0_ref, v1_ref, w_ref, bt_ref, g_ref, b_ref, o_ref):
    r = lax.broadcasted_iota(jnp.int32, (CHUNK, CHUNK), 0)
    c = lax.broadcasted_iota(jnp.int32, (CHUNK, CHUNK), 1)
    causal = r >= c
    lane_mean = jnp.full((SGU_GROUP_DIM, SGU_GROUP_DIM), 1.0 / SGU_GROUP_DIM, BF16)
    groups_per_tile = TN // SGU_GROUP_DIM
    n_chunks = o_ref.shape[0] // CHUNK
    groups = range(SGU_GROUPS)

    def lanes(grp):
        return slice(grp * SGU_GROUP_DIM, (grp + 1) * SGU_GROUP_DIM)

    def tile_lanes(grp):
        return lanes(grp % groups_per_tile)

    vs = [(v0_ref, v1_ref)[grp // groups_per_tile][:, tile_lanes(grp)].astype(F32) for grp in groups]
    mus = [jnp.dot(v.astype(BF16), lane_mean, preferred_element_type=F32) for v in vs]
    dvs = [v - mu for v, mu in zip(vs, mus)]
    vars_ = [jnp.dot((dv * dv).astype(BF16), lane_mean, preferred_element_type=F32) for dv in dvs]
    ss = []
    for grp in groups:
        vn = (dvs[grp] * lax.rsqrt(vars_[grp] + EPS) * g_ref[:, lanes(grp)]
              + b_ref[:, lanes(grp)]).astype(BF16)
        w = jnp.where(causal, w_ref[grp], 0.0).astype(BF16)
        vn_chunks = [vn[ch * CHUNK:(ch + 1) * CHUNK, :] for ch in range(n_chunks)]
        ss.append(jnp.dot(w, jnp.concatenate(vn_chunks, axis=-1), preferred_element_type=F32))
    for grp in groups:
        s = ss[grp] + bt_ref[:, grp:grp + 1]
        u_ref = (u0_ref, u1_ref)[grp // groups_per_tile]
        for ch in range(n_chunks):
            u = u_ref[ch * CHUNK:(ch + 1) * CHUNK, tile_lanes(grp)].astype(F32)
            o_ref[ch * CHUNK:(ch + 1) * CHUNK, lanes(grp)] = (
                u * s[:, ch * CHUNK:(ch + 1) * CHUNK]).astype(o_ref.dtype)


def _sgu(proj, w_s, b_t, ln_g, ln_b, layer, rows=8 * CHUNK):
    m = proj.shape[0]
    u0 = PROJ_UV_TILE

    def tile(t):
        return pl.BlockSpec((rows, TN), lambda i: (i, t))

    return pl.pallas_call(
        _sgu_kernel,
        grid=(m // rows,),
        in_specs=[tile(u0), tile(u0 + 1), tile(u0 + 2), tile(u0 + 3),
                  pl.BlockSpec((None, SGU_GROUPS, CHUNK, CHUNK), lambda i: (layer, 0, 0, 0)),
                  pl.BlockSpec((None, CHUNK, SGU_GROUPS), lambda i: (layer, 0, 0)),
                  pl.BlockSpec((None, 1, SGU_WIDTH), lambda i: (layer, 0, 0)),
                  pl.BlockSpec((None, 1, SGU_WIDTH), lambda i: (layer, 0, 0))],
        out_specs=pl.BlockSpec((rows, SGU_WIDTH), lambda i: (i, 0)),
        out_shape=jax.ShapeDtypeStruct((m, SGU_WIDTH), BF16),
        compiler_params=_params(1),
        name="sgu",
    )(proj, proj, proj, proj, w_s, b_t, ln_g, ln_b)


def _rope_tables(seq):
    pos = jnp.arange(seq, dtype=F32)
    inv_freq = jnp.power(ROPE_THETA, -jnp.arange(0, HEAD_DIM, 2, dtype=F32) / HEAD_DIM)
    ang = pos[:, None] * inv_freq[None, :]
    cos, sin = jnp.cos(ang), jnp.sin(ang)
    reps = V7X_LANES // HEAD_DIM
    return (jnp.tile(cos, (1, 2 * reps)), jnp.tile(jnp.concatenate([-sin, sin], axis=-1), (1, reps)),
            cos.T, sin.T)


def kernel(x, mix_norm, w_in, q_norm, k_norm, sinks, sgu_ln_g, sgu_ln_b, w_spatial, b_spatial,
           w_attn_branch, w_sgu_branch, w_out, ffn_norm, w_gate, w_up, w_down):
    batch, seq, d = x.shape
    depth = w_in.shape[0]
    m = batch * seq
    assert d == D_MODEL and w_in.shape[-1] == (QKV_TILES + UV_TILES + GATE_TILES) * TN
    tables = _rope_tables(seq)
    gq_t = jnp.broadcast_to((q_norm * (HEAD_DIM ** -0.5 * LOG2_E))[:, :, None],
                            (depth, HEAD_DIM, WINDOW))
    gk = jnp.tile(k_norm, (1, V7X_LANES // HEAD_DIM)).reshape(depth, 1, V7X_LANES)
    b_t = jnp.swapaxes(b_spatial, 1, 2)
    mix_norm, ffn_norm, sgu_ln_g, sgu_ln_b = (
        p.reshape(depth, 1, -1) for p in (mix_norm, ffn_norm, sgu_ln_g, sgu_ln_b))

    xf = x.reshape(m, d)
    h = _rmsnorm(xf, mix_norm, 0)
    for l in range(depth):
        proj = _in_proj(h, w_in, l)
        a, wa_bf, wb_bf, wo_bf = _attention(proj, sinks[l], tables, gq_t, gk,
                                            w_attn_branch, w_sgu_branch, w_out, l, batch, seq)
        b = _sgu(proj, w_spatial, b_t, sgu_ln_g, sgu_ln_b, l)
        xf, h2 = _mix_out(a, b, proj, xf, wa_bf, wb_bf, wo_bf, ffn_norm, l)
        act, wd_bf = _swiglu(h2, w_gate, w_up, w_down, l)
        if l + 1 < depth:
            xf, h = _down_out(act, xf, wd_bf, mix_norm, l + 1)
        else:
            (xf,) = _down_out(act, xf, wd_bf, None, l)
    return xf.reshape(batch, seq, d)
```

```python
import functools

import jax
import jax.numpy as jnp
from jax import lax
from jax.experimental import pallas as pl
from jax.experimental.pallas import tpu as pltpu

F32 = jnp.float32
BF16 = jnp.bfloat16

HEAD_DIM = 64
N_Q_HEADS = 16
N_KV_HEADS = 4
Q_PER_KV = N_Q_HEADS // N_KV_HEADS
ATTN_WIDTH = N_Q_HEADS * HEAD_DIM
KV_WIDTH = N_KV_HEADS * HEAD_DIM
WINDOW = 128
ROPE_THETA = 10000.0
SGU_GROUPS = 8
SGU_GROUP_DIM = 128
SGU_WIDTH = SGU_GROUPS * SGU_GROUP_DIM
CHUNK = 128
EPS = 1e-6
MASK_VALUE = -1e30
LOG2_E = 1.4426950408889634

V7X_LANES = 128
V7X_SUBLANES_BF16 = 16
V7X_VMEM_BYTES = 64 * 1024 * 1024
VMEM_LIMIT = V7X_VMEM_BYTES - 8 * 1024 * 1024

TN = 512
D_MODEL = 2048
QKV_TILES = (ATTN_WIDTH + 2 * KV_WIDTH) // TN
UV_TILES = 2 * SGU_WIDTH // TN
GATE_TILES = 2 * D_MODEL // TN
PROJ_QKV_COL = GATE_TILES * TN
PROJ_UV_TILE = GATE_TILES + QKV_TILES


def _params(n_axes):
    return pltpu.CompilerParams(dimension_semantics=("arbitrary",) * n_axes,
                                vmem_limit_bytes=VMEM_LIMIT)


def _layer_row(ref, layer):
    return ref[layer:layer + 1, :]


def _whole(p):
    return pl.BlockSpec(p.shape, lambda *_: (0,) * p.ndim)


def _rmsnorm_kernel(x_ref, g_ref, o_ref, *, layer):
    x = x_ref[...]
    y = x * lax.rsqrt(jnp.mean(x * x, axis=-1, keepdims=True) + EPS)
    o_ref[...] = (y * _layer_row(g_ref, layer)).astype(o_ref.dtype)


def _rmsnorm(x, gains, layer, tm=512):
    m, d = x.shape
    return pl.pallas_call(
        functools.partial(_rmsnorm_kernel, layer=layer),
        grid=(m // tm,),
        in_specs=[pl.BlockSpec((tm, d), lambda i: (i, 0)), _whole(gains)],
        out_specs=pl.BlockSpec((tm, d), lambda i: (i, 0)),
        out_shape=jax.ShapeDtypeStruct((m, d), BF16),
        compiler_params=_params(1),
        name="rmsnorm",
    )(x, gains)


SUB_ROWS = 512


def _row_blocks(o_ref):
    tm = o_ref.shape[0]
    sub = min(SUB_ROWS, tm // 2)
    return [pl.ds(r, sub) for r in range(0, tm, sub)]


def _cast_weights(w_refs, wbf_refs):
    @pl.when(pl.program_id(1) == 0)
    def _():
        for w_ref, wbf_ref in zip(w_refs, wbf_refs):
            wbf_ref[...] = w_ref[...].astype(BF16)


def _sigmoid(y):
    return 0.5 * jnp.tanh(0.5 * y) + 0.5


def _gelu_tanh(y):
    c = 0.7978845608028654
    half = 0.5 * y
    return half + half * jnp.tanh(y * (c + (c * 0.044715) * (y * y)))


def _in_proj_kernel(h_ref, w_ref, o_ref, wbf_ref):
    _cast_weights([w_ref], [wbf_ref])
    j = pl.program_id(0)

    def emit(act):
        for rows in _row_blocks(o_ref):
            y = jnp.dot(h_ref[rows, :], wbf_ref[...], preferred_element_type=F32)
            o_ref[rows, :] = (y if act is None else act(y)).astype(o_ref.dtype)

    pl.when(j < QKV_TILES)(lambda: emit(None))
    pl.when((j >= QKV_TILES) & (j < QKV_TILES + UV_TILES))(
        lambda: emit(_gelu_tanh))
    pl.when(j >= QKV_TILES + UV_TILES)(lambda: emit(_sigmoid))


def _in_proj(h, w, layer, tm=4096):
    m, k = h.shape
    n = w.shape[-1]
    return pl.pallas_call(
        _in_proj_kernel,
        grid=(n // TN, m // tm),
        in_specs=[pl.BlockSpec((tm, k), lambda j, i: (i, 0)),
                  pl.BlockSpec((None, k, TN), lambda j, i: (layer, 0, j))],
        out_specs=pl.BlockSpec(
            (tm, TN), lambda j, i: (i, jnp.where(j < QKV_TILES + UV_TILES, j + GATE_TILES,
                                                 j - (QKV_TILES + UV_TILES)))),
        out_shape=jax.ShapeDtypeStruct((m, n), BF16),
        scratch_shapes=[pltpu.VMEM((k, TN), BF16)],
        compiler_params=_params(2),
        name="in_proj",
    )(h, w)


def _swiglu_kernel(h_ref, wg_ref, wu_ref, wd_ref, o_ref, wd_bf_ref, wgbf_ref, wubf_ref):
    _cast_weights([wg_ref, wu_ref], [wgbf_ref, wubf_ref])
    wd_bf_ref[...] = wd_ref[...].astype(wd_bf_ref.dtype)
    for rows in _row_blocks(o_ref):
        h = h_ref[rows, :]
        g = jnp.dot(h, wgbf_ref[...], preferred_element_type=F32)
        u = jnp.dot(h, wubf_ref[...], preferred_element_type=F32)
        o_ref[rows, :] = (g * _sigmoid(g) * u).astype(o_ref.dtype)


def _swiglu(h, wg, wu, wd, layer, tm=2048):
    m, k = h.shape
    f = wg.shape[-1]
    mt = m // tm
    slab = wd.shape[1] // (f // TN * mt)
    w_spec = pl.BlockSpec((None, k, TN), lambda j, i: (layer, 0, j))
    return pl.pallas_call(
        _swiglu_kernel,
        grid=(f // TN, mt),
        in_specs=[pl.BlockSpec((tm, k), lambda j, i: (i, 0)), w_spec, w_spec,
                  pl.BlockSpec((None, slab, wd.shape[2]), lambda j, i: (layer, j * mt + i, 0))],
        out_specs=[pl.BlockSpec((tm, TN), lambda j, i: (i, j)),
                   pl.BlockSpec((slab, wd.shape[2]), lambda j, i: (j * mt + i, 0))],
        out_shape=[jax.ShapeDtypeStruct((m, f), BF16), jax.ShapeDtypeStruct(wd.shape[1:], BF16)],
        scratch_shapes=[pltpu.VMEM((k, TN), BF16), pltpu.VMEM((k, TN), BF16)],
        compiler_params=_params(2),
        name="swiglu",
    )(h, wg, wu, wd)


def _mix_out_kernel(a_ref, b_ref, ga_ref, gb_ref, x_ref, wa_ref, wb_ref, wo_ref, gain_ref,
                    o_ref, h_ref, merged_ref, *, layer):
    tm, d = o_ref.shape
    tiles = [slice(t * TN, (t + 1) * TN) for t in range(d // TN)]
    halves = [pl.ds(r, tm // 2) for r in (0, tm // 2)]
    for rows in halves:
        for cols in tiles:
            ya = jnp.dot(a_ref[rows, :], wa_ref[:, cols], preferred_element_type=F32)
            yb = jnp.dot(b_ref[rows, :], wb_ref[:, cols], preferred_element_type=F32)
            merged_ref[rows, cols] = (ga_ref[rows, cols] * ya + gb_ref[rows, cols] * yb
                                      ).astype(merged_ref.dtype)
    for rows in halves:
        ssq = jnp.zeros((tm // 2, 1), F32)
        for cols in tiles:
            y = x_ref[rows, cols] + jnp.dot(merged_ref[rows, :], wo_ref[:, cols],
                                            preferred_element_type=F32)
            o_ref[rows, cols] = y
            ssq = ssq + jnp.sum(y * y, axis=-1, keepdims=True)
        scale = lax.rsqrt(ssq * (1.0 / d) + EPS)
        h_ref[rows, :] = (o_ref[rows, :] * scale * _layer_row(gain_ref, layer)).astype(h_ref.dtype)


def _mix_out(a, b, proj, x, wa_bf, wb_bf, wo_bf, gains, layer, tm=512):
    m, k = a.shape
    d = x.shape[1]
    row_tile = pl.BlockSpec((tm, d), lambda i: (i, 0))
    return pl.pallas_call(
        functools.partial(_mix_out_kernel, layer=layer),
        grid=(m // tm,),
        in_specs=[pl.BlockSpec((tm, k), lambda i: (i, 0)),
                  pl.BlockSpec((tm, k), lambda i: (i, 0)),
                  pl.BlockSpec((tm, d), lambda i: (i, 0)),
                  pl.BlockSpec((tm, d), lambda i: (i, 1)),
                  row_tile, _whole(wa_bf), _whole(wb_bf), _whole(wo_bf), _whole(gains)],
        out_specs=[row_tile, row_tile],
        out_shape=[jax.ShapeDtypeStruct((m, d), F32), jax.ShapeDtypeStruct((m, d), BF16)],
        scratch_shapes=[pltpu.VMEM((tm, d), BF16)],
        compiler_params=_params(1),
        name="mix_out",
    )(a, b, proj, proj, x, wa_bf, wb_bf, wo_bf, gains)


def _down_out_kernel(act_ref, x_ref, w_ref, *rest, layer):
    gain_ref, o_ref, h_ref = rest if len(rest) == 3 else (None, rest[0], None)
    tm, d = o_ref.shape
    tiles = [slice(t * TN, (t + 1) * TN) for t in range(d // TN)]
    for rows in (pl.ds(r, tm // 2) for r in (0, tm // 2)):
        ssq = jnp.zeros((tm // 2, 1), F32)
        for cols in tiles:
            y = x_ref[rows, cols] + jnp.dot(act_ref[rows, :], w_ref[:, cols],
                                            preferred_element_type=F32)
            o_ref[rows, cols] = y
            if h_ref is not None:
                ssq = ssq + jnp.sum(y * y, axis=-1, keepdims=True)
        if h_ref is not None:
            scale = lax.rsqrt(ssq * (1.0 / d) + EPS)
            h_ref[rows, :] = (o_ref[rows, :] * scale * _layer_row(gain_ref, layer)).astype(h_ref.dtype)


def _down_out(act, x, w_bf, gains, layer, tm=512):
    m, k = act.shape
    d = x.shape[1]
    row_tile = pl.BlockSpec((tm, d), lambda i: (i, 0))
    in_specs = [pl.BlockSpec((tm, k), lambda i: (i, 0)), row_tile, _whole(w_bf)]
    operands = [act, x, w_bf]
    out_specs, out_shape = [row_tile], [jax.ShapeDtypeStruct((m, d), F32)]
    if gains is not None:
        in_specs.append(_whole(gains))
        operands.append(gains)
        out_specs.append(row_tile)
        out_shape.append(jax.ShapeDtypeStruct((m, d), BF16))
    return pl.pallas_call(
        functools.partial(_down_out_kernel, layer=layer),
        grid=(m // tm,),
        in_specs=in_specs,
        out_specs=out_specs,
        out_shape=out_shape,
        compiler_params=_params(1),
        name="down_out" if gains is not None else "down_out_last",
    )(*operands)


ATTN_BLOCKS_PER_STEP = 8


def _attn_kernel(sinks_ref, q_ref, kv_ref, cos_ref, sin_ref, cos_t_ref, sin_t_ref, gq_t_ref, gk_ref,
                 wa_ref, wb_ref, wo_ref, o_ref, wa_bf_ref, wb_bf_ref, wo_bf_ref,
                 k_prev_ref, vt_prev_ref):
    n = pl.program_id(1)
    half_dim = HEAD_DIM // 2

    for src, dst in ((wa_ref, wa_bf_ref), (wb_ref, wb_bf_ref), (wo_ref, wo_bf_ref)):
        dst[...] = src[...].astype(dst.dtype)

    @pl.when(n == 0)
    def _():
        k_prev_ref[...] = jnp.zeros_like(k_prev_ref)
        vt_prev_ref[...] = jnp.zeros_like(vt_prev_ref)

    lane = lax.broadcasted_iota(jnp.int32, (1, V7X_LANES), 1)
    even_quarter = ((lane // half_dim) % 2) == 0
    r = lax.broadcasted_iota(jnp.int32, (V7X_LANES, V7X_LANES), 0)
    c = lax.broadcasted_iota(jnp.int32, (V7X_LANES, V7X_LANES), 1)
    head_mean = jnp.where((r // HEAD_DIM) == (c // HEAD_DIM), 1.0 / HEAD_DIM, 0.0).astype(BF16)
    key = lax.broadcasted_iota(jnp.int32, (2 * WINDOW, WINDOW), 0)
    qry = lax.broadcasted_iota(jnp.int32, (2 * WINDOW, WINDOW), 1)
    diff = qry + WINDOW - key
    band = (diff >= 0) & (diff < WINDOW)
    gq_t = gq_t_ref[...]
    zeros = jnp.zeros((HEAD_DIM, WINDOW), BF16)
    n_cols = KV_WIDTH // V7X_LANES
    heads_per_col = N_Q_HEADS // n_cols
    group_w = Q_PER_KV * WINDOW
    sink_row = [jnp.full((1, WINDOW), sinks_ref[h] * LOG2_E, F32) for h in range(N_Q_HEADS)]

    def one_block(rows, k_prev, vt_prev, allowed):
        cos, sin = cos_ref[rows, :], sin_ref[rows, :]
        k_cols = []
        for col in range(n_cols):
            x = kv_ref[rows, col * V7X_LANES:(col + 1) * V7X_LANES].astype(F32)
            ms = jnp.dot((x * x).astype(BF16), head_mean, preferred_element_type=F32)
            y = x * lax.rsqrt(ms + EPS) * gk_ref[...]
            partner = jnp.where(even_quarter,
                                pltpu.roll(y, V7X_LANES - half_dim, 1),
                                pltpu.roll(y, half_dim, 1))
            k_cols.append((y * cos + partner * sin).astype(BF16))
        k_win = [jnp.concatenate([k_prev[col], k_cols[col]], axis=0) for col in range(n_cols)]
        vt_cur = kv_ref[rows, KV_WIDTH:].astype(F32).T.astype(BF16)
        vt_win = jnp.concatenate([vt_prev, vt_cur], axis=1)

        cos_t, sin_t = cos_t_ref[:, rows], sin_t_ref[:, rows]
        qt_pads = []
        for pair in range(N_Q_HEADS // 2):
            sl = slice(pair * V7X_LANES, (pair + 1) * V7X_LANES)
            qt_pair = q_ref[rows, sl].astype(F32).T
            for half in range(2):
                kv_head = (2 * pair + half) // Q_PER_KV
                x = qt_pair[half * HEAD_DIM:(half + 1) * HEAD_DIM, :]
                y = x * lax.rsqrt(jnp.mean(x * x, axis=0, keepdims=True) + EPS) * gq_t
                y1, y2 = y[:half_dim], y[half_dim:]
                qt = jnp.concatenate([y1 * cos_t - y2 * sin_t, y2 * cos_t + y1 * sin_t],
                                     axis=0).astype(BF16)
                qt_pads.append(jnp.concatenate([zeros, qt] if kv_head % 2 else [qt, zeros], axis=0))

        probs, sink_terms = [], []
        for col in range(n_cols):
            heads = range(col * heads_per_col, (col + 1) * heads_per_col)
            s = jnp.dot(k_win[col], jnp.concatenate([qt_pads[h] for h in heads], axis=1),
                        preferred_element_type=F32)
            s = jnp.where(jnp.concatenate([allowed] * heads_per_col, axis=1), s, MASK_VALUE)
            sink = jnp.concatenate([sink_row[h] for h in heads], axis=1)
            m = jnp.maximum(jnp.max(s, axis=0, keepdims=True), sink)
            probs.append(jnp.exp2(s - m).astype(BF16))
            sink_terms.append(jnp.exp2(sink - m))

        ones_rows = jnp.ones((V7X_SUBLANES_BF16, 2 * WINDOW), BF16)
        for kv_head in range(N_KV_HEADS):
            col, part = divmod(kv_head, N_KV_HEADS // n_cols)
            lanes = slice(part * group_w, (part + 1) * group_w)
            vt = jnp.concatenate([vt_win[kv_head * HEAD_DIM:(kv_head + 1) * HEAD_DIM, :], ones_rows],
                                 axis=0)
            o = jnp.dot(vt, probs[col][:, lanes], preferred_element_type=F32)
            denom = o[HEAD_DIM:HEAD_DIM + 1, :] + sink_terms[col][:, lanes]
            o = o[:HEAD_DIM, :] * (1.0 / denom)
            for pair in range(Q_PER_KV // 2):
                o_pair = jnp.concatenate([o[:, (2 * pair) * WINDOW:(2 * pair + 1) * WINDOW],
                                          o[:, (2 * pair + 1) * WINDOW:(2 * pair + 2) * WINDOW]], axis=0)
                out_lane = (kv_head * Q_PER_KV + 2 * pair) * HEAD_DIM
                o_ref[rows, out_lane:out_lane + V7X_LANES] = o_pair.T.astype(o_ref.dtype)
        return k_cols, vt_cur

    k_prev = [k_prev_ref[col] for col in range(n_cols)]
    vt_prev = vt_prev_ref[...]
    for blk in range(o_ref.shape[0] // WINDOW):
        allowed = band & ((key >= WINDOW) | (n > 0)) if blk == 0 else band
        k_prev, vt_prev = one_block(pl.ds(blk * WINDOW, WINDOW), k_prev, vt_prev, allowed)

    for col in range(n_cols):
        k_prev_ref[col] = k_prev[col]
    vt_prev_ref[...] = vt_prev


def _attention(proj, sinks, tables, gq_t, gk, wa, wb, wo, layer, batch, seq):
    m = proj.shape[0]
    rows = ATTN_BLOCKS_PER_STEP * WINDOW
    nb = seq // rows
    steps = batch * nb

    def slab(w):
        return pl.BlockSpec((None, w.shape[1] // steps, w.shape[2]),
                            lambda b, n: (layer, b * nb + n, 0))

    def slab_out(w):
        return pl.BlockSpec((w.shape[1] // steps, w.shape[2]), lambda b, n: (b * nb + n, 0))

    q_blk = PROJ_QKV_COL // ATTN_WIDTH
    kv_blk = (PROJ_QKV_COL + ATTN_WIDTH) // (2 * KV_WIDTH)
    cos, sin_signed, cos_t, sin_t = tables
    half_dim = HEAD_DIM // 2
    row_table = pl.BlockSpec((rows, V7X_LANES), lambda b, n: (n, 0))
    col_table = pl.BlockSpec((half_dim, rows), lambda b, n: (0, n))
    return pl.pallas_call(
        _attn_kernel,
        grid=(batch, nb),
        in_specs=[pl.BlockSpec(memory_space=pltpu.SMEM),
                  pl.BlockSpec((rows, ATTN_WIDTH), lambda b, n: (b * nb + n, q_blk)),
                  pl.BlockSpec((rows, 2 * KV_WIDTH), lambda b, n: (b * nb + n, kv_blk)),
                  row_table, row_table, col_table, col_table,
                  pl.BlockSpec((None, HEAD_DIM, WINDOW), lambda b, n: (layer, 0, 0)),
                  pl.BlockSpec((None, 1, V7X_LANES), lambda b, n: (layer, 0, 0)),
                  slab(wa), slab(wb), slab(wo)],
        out_specs=[pl.BlockSpec((rows, ATTN_WIDTH), lambda b, n: (b * nb + n, 0)),
                   slab_out(wa), slab_out(wb), slab_out(wo)],
        out_shape=[jax.ShapeDtypeStruct((m, ATTN_WIDTH), BF16)]
        + [jax.ShapeDtypeStruct(w.shape[1:], BF16) for w in (wa, wb, wo)],
        scratch_shapes=[pltpu.VMEM((KV_WIDTH // V7X_LANES, WINDOW, V7X_LANES), BF16),
                        pltpu.VMEM((KV_WIDTH, WINDOW), BF16)],
        compiler_params=_params(2),
        name="swa_attention",
    )(sinks, proj, proj, cos, sin_signed, cos_t, sin_t, gq_t, gk, wa, wb, wo)


def _sgu_kernel(u0_ref, u1_ref, v0_ref, v1_ref, w_ref, bt_ref, g_ref, b_ref, o_ref, *, layer):
    r = lax.broadcasted_iota(jnp.int32, (CHUNK, CHUNK), 0)
    c = lax.broadcasted_iota(jnp.int32, (CHUNK, CHUNK), 1)
    causal = r >= c
    lane_mean = jnp.full((SGU_GROUP_DIM, SGU_GROUP_DIM), 1.0 / SGU_GROUP_DIM, BF16)
    groups_per_tile = TN // SGU_GROUP_DIM
    n_chunks = o_ref.shape[0] // CHUNK
    groups = range(SGU_GROUPS)

    def lanes(grp):
        return slice(grp * SGU_GROUP_DIM, (grp + 1) * SGU_GROUP_DIM)

    def tile_lanes(grp):
        return lanes(grp % groups_per_tile)

    vs = [(v0_ref, v1_ref)[grp // groups_per_tile][:, tile_lanes(grp)].astype(F32) for grp in groups]
    mus = [jnp.dot(v.astype(BF16), lane_mean, preferred_element_type=F32) for v in vs]
    dvs = [v - mu for v, mu in zip(vs, mus)]
    vars_ = [jnp.dot((dv * dv).astype(BF16), lane_mean, preferred_element_type=F32) for dv in dvs]
    ss = []
    for grp in groups:
        vn = (dvs[grp] * lax.rsqrt(vars_[grp] + EPS) * g_ref[layer:layer + 1, lanes(grp)]
              + b_ref[layer:layer + 1, lanes(grp)]).astype(BF16)
        w = jnp.where(causal, w_ref[grp], 0.0).astype(BF16)
        vn_chunks = [vn[ch * CHUNK:(ch + 1) * CHUNK, :] for ch in range(n_chunks)]
        ss.append(jnp.dot(w, jnp.concatenate(vn_chunks, axis=-1), preferred_element_type=F32))
    for grp in groups:
        s = ss[grp] + bt_ref[:, grp:grp + 1]
        u_ref = (u0_ref, u1_ref)[grp // groups_per_tile]
        for ch in range(n_chunks):
            u = u_ref[ch * CHUNK:(ch + 1) * CHUNK, tile_lanes(grp)].astype(F32)
            o_ref[ch * CHUNK:(ch + 1) * CHUNK, lanes(grp)] = (
                u * s[:, ch * CHUNK:(ch + 1) * CHUNK]).astype(o_ref.dtype)


def _sgu(proj, w_s, b_t, ln_g, ln_b, layer, rows=8 * CHUNK):
    m = proj.shape[0]
    u0 = PROJ_UV_TILE

    def tile(t):
        return pl.BlockSpec((rows, TN), lambda i: (i, t))

    return pl.pallas_call(
        functools.partial(_sgu_kernel, layer=layer),
        grid=(m // rows,),
        in_specs=[tile(u0), tile(u0 + 1), tile(u0 + 2), tile(u0 + 3),
                  pl.BlockSpec((None, SGU_GROUPS, CHUNK, CHUNK), lambda i: (layer, 0, 0, 0)),
                  pl.BlockSpec((None, CHUNK, SGU_GROUPS), lambda i: (layer, 0, 0)),
                  _whole(ln_g), _whole(ln_b)],
        out_specs=pl.BlockSpec((rows, SGU_WIDTH), lambda i: (i, 0)),
        out_shape=jax.ShapeDtypeStruct((m, SGU_WIDTH), BF16),
        compiler_params=_params(1),
        name="sgu",
    )(proj, proj, proj, proj, w_s, b_t, ln_g, ln_b)


def _rope_tables(seq):
    pos = jnp.arange(seq, dtype=F32)
    inv_freq = jnp.power(ROPE_THETA, -jnp.arange(0, HEAD_DIM, 2, dtype=F32) / HEAD_DIM)
    ang = pos[:, None] * inv_freq[None, :]
    cos, sin = jnp.cos(ang), jnp.sin(ang)
    reps = V7X_LANES // HEAD_DIM
    return (jnp.tile(cos, (1, 2 * reps)), jnp.tile(jnp.concatenate([-sin, sin], axis=-1), (1, reps)),
            cos.T, sin.T)


def kernel(x, mix_norm, w_in, q_norm, k_norm, sinks, sgu_ln_g, sgu_ln_b, w_spatial, b_spatial,
           w_attn_branch, w_sgu_branch, w_out, ffn_norm, w_gate, w_up, w_down):
    batch, seq, d = x.shape
    depth = w_in.shape[0]
    m = batch * seq
    assert d == D_MODEL and w_in.shape[-1] == (QKV_TILES + UV_TILES + GATE_TILES) * TN
    tables = _rope_tables(seq)
    gq_t = jnp.broadcast_to((q_norm * (HEAD_DIM ** -0.5 * LOG2_E))[:, :, None],
                            (depth, HEAD_DIM, WINDOW))
    gk = jnp.tile(k_norm, (1, V7X_LANES // HEAD_DIM)).reshape(depth, 1, V7X_LANES)
    b_t = jnp.swapaxes(b_spatial, 1, 2)

    xf = x.reshape(m, d)
    h = _rmsnorm(xf, mix_norm, 0)
    for l in range(depth):
        proj = _in_proj(h, w_in, l)
        a, wa_bf, wb_bf, wo_bf = _attention(proj, sinks[l], tables, gq_t, gk,
                                            w_attn_branch, w_sgu_branch, w_out, l, batch, seq)
        b = _sgu(proj, w_spatial, b_t, sgu_ln_g, sgu_ln_b, l)
        xf, h2 = _mix_out(a, b, proj, xf, wa_bf, wb_bf, wo_bf, ffn_norm, l)
        act, wd_bf = _swiglu(h2, w_gate, w_up, w_down, l)
        if l + 1 < depth:
            xf, h = _down_out(act, xf, wd_bf, mix_norm, l + 1)
        else:
            (xf,) = _down_out(act, xf, wd_bf, None, l)
    return xf.reshape(batch, seq, d)
```

```python
import functools

import jax
import jax.numpy as jnp
from jax import lax
from jax.experimental import pallas as pl
from jax.experimental.pallas import tpu as pltpu

F32 = jnp.float32
BF16 = jnp.bfloat16

HEAD_DIM = 64
N_Q_HEADS = 16
N_KV_HEADS = 4
Q_PER_KV = N_Q_HEADS // N_KV_HEADS
ATTN_WIDTH = N_Q_HEADS * HEAD_DIM
KV_WIDTH = N_KV_HEADS * HEAD_DIM
WINDOW = 128
ROPE_THETA = 10000.0
SGU_GROUPS = 8
SGU_GROUP_DIM = 128
SGU_WIDTH = SGU_GROUPS * SGU_GROUP_DIM
CHUNK = 128
EPS = 1e-6
MASK_VALUE = -1e30
LOG2_E = 1.4426950408889634

V7X_LANES = 128
V7X_SUBLANES_BF16 = 16
V7X_VMEM_BYTES = 64 * 1024 * 1024
VMEM_LIMIT = V7X_VMEM_BYTES - 8 * 1024 * 1024

TN = 512
D_MODEL = 2048
QKV_TILES = (ATTN_WIDTH + 2 * KV_WIDTH) // TN
UV_TILES = 2 * SGU_WIDTH // TN
GATE_TILES = 2 * D_MODEL // TN
PROJ_QKV_COL = GATE_TILES * TN
PROJ_UV_TILE = GATE_TILES + QKV_TILES


def _params(n_axes):
    return pltpu.CompilerParams(dimension_semantics=("arbitrary",) * n_axes,
                                vmem_limit_bytes=VMEM_LIMIT)


def _layer_row(ref, layer):
    return ref[layer:layer + 1, :]


def _whole(p):
    return pl.BlockSpec(p.shape, lambda *_: (0,) * p.ndim)


def _rmsnorm_kernel(x_ref, g_ref, o_ref, *, layer):
    x = x_ref[...]
    y = x * lax.rsqrt(jnp.mean(x * x, axis=-1, keepdims=True) + EPS)
    o_ref[...] = (y * _layer_row(g_ref, layer)).astype(o_ref.dtype)


def _rmsnorm(x, gains, layer, tm=512):
    m, d = x.shape
    return pl.pallas_call(
        functools.partial(_rmsnorm_kernel, layer=layer),
        grid=(m // tm,),
        in_specs=[pl.BlockSpec((tm, d), lambda i: (i, 0)), _whole(gains)],
        out_specs=pl.BlockSpec((tm, d), lambda i: (i, 0)),
        out_shape=jax.ShapeDtypeStruct((m, d), BF16),
        compiler_params=_params(1),
        name="rmsnorm",
    )(x, gains)


SUB_ROWS = 512


def _row_blocks(o_ref):
    tm = o_ref.shape[0]
    sub = min(SUB_ROWS, tm // 2)
    return [pl.ds(r, sub) for r in range(0, tm, sub)]


def _cast_weights(w_refs, wbf_refs):
    @pl.when(pl.program_id(1) == 0)
    def _():
        for w_ref, wbf_ref in zip(w_refs, wbf_refs):
            wbf_ref[...] = w_ref[...].astype(BF16)


def _sigmoid(y):
    return 0.5 * jnp.tanh(0.5 * y) + 0.5


def _gelu_tanh(y):
    c = 0.7978845608028654
    half = 0.5 * y
    return half + half * jnp.tanh(y * (c + (c * 0.044715) * (y * y)))


def _in_proj_kernel(h_ref, w_ref, o_ref, wbf_ref):
    _cast_weights([w_ref], [wbf_ref])
    j = pl.program_id(0)

    def emit(act):
        for rows in _row_blocks(o_ref):
            y = jnp.dot(h_ref[rows, :], wbf_ref[...], preferred_element_type=F32)
            o_ref[rows, :] = (y if act is None else act(y)).astype(o_ref.dtype)

    pl.when(j < QKV_TILES)(lambda: emit(None))
    pl.when((j >= QKV_TILES) & (j < QKV_TILES + UV_TILES))(
        lambda: emit(_gelu_tanh))
    pl.when(j >= QKV_TILES + UV_TILES)(lambda: emit(_sigmoid))


def _in_proj(h, w, layer, tm=4096):
    m, k = h.shape
    n = w.shape[-1]
    return pl.pallas_call(
        _in_proj_kernel,
        grid=(n // TN, m // tm),
        in_specs=[pl.BlockSpec((tm, k), lambda j, i: (i, 0)),
                  pl.BlockSpec((None, k, TN), lambda j, i: (layer, 0, j))],
        out_specs=pl.BlockSpec(
            (tm, TN), lambda j, i: (i, jnp.where(j < QKV_TILES + UV_TILES, j + GATE_TILES,
                                                 j - (QKV_TILES + UV_TILES)))),
        out_shape=jax.ShapeDtypeStruct((m, n), BF16),
        scratch_shapes=[pltpu.VMEM((k, TN), BF16)],
        compiler_params=_params(2),
        name="in_proj",
    )(h, w)


def _swiglu_kernel(h_ref, wg_ref, wu_ref, wd_ref, o_ref, wd_bf_ref, wgbf_ref, wubf_ref):
    _cast_weights([wg_ref, wu_ref], [wgbf_ref, wubf_ref])
    wd_bf_ref[...] = wd_ref[...].astype(wd_bf_ref.dtype)
    for rows in _row_blocks(o_ref):
        h = h_ref[rows, :]
        g = jnp.dot(h, wgbf_ref[...], preferred_element_type=F32)
        u = jnp.dot(h, wubf_ref[...], preferred_element_type=F32)
        o_ref[rows, :] = (g * _sigmoid(g) * u).astype(o_ref.dtype)


def _swiglu(h, wg, wu, wd, layer, tm=2048):
    m, k = h.shape
    f = wg.shape[-1]
    mt = m // tm
    slab = wd.shape[1] // (f // TN * mt)
    w_spec = pl.BlockSpec((None, k, TN), lambda j, i: (layer, 0, j))
    return pl.pallas_call(
        _swiglu_kernel,
        grid=(f // TN, mt),
        in_specs=[pl.BlockSpec((tm, k), lambda j, i: (i, 0)), w_spec, w_spec,
                  pl.BlockSpec((None, slab, wd.shape[2]), lambda j, i: (layer, j * mt + i, 0))],
        out_specs=[pl.BlockSpec((tm, TN), lambda j, i: (i, j)),
                   pl.BlockSpec((slab, wd.shape[2]), lambda j, i: (j * mt + i, 0))],
        out_shape=[jax.ShapeDtypeStruct((m, f), BF16), jax.ShapeDtypeStruct(wd.shape[1:], BF16)],
        scratch_shapes=[pltpu.VMEM((k, TN), BF16), pltpu.VMEM((k, TN), BF16)],
        compiler_params=_params(2),
        name="swiglu",
    )(h, wg, wu, wd)


def _mix_out_kernel(a_ref, b_ref, ga_ref, gb_ref, x_ref, wa_ref, wb_ref, wo_ref, gain_ref,
                    o_ref, h_ref, merged_ref, *, layer):
    tm, d = o_ref.shape
    tiles = [slice(t * TN, (t + 1) * TN) for t in range(d // TN)]
    halves = [pl.ds(r, tm // 2) for r in (0, tm // 2)]
    for rows in halves:
        for cols in tiles:
            ya = jnp.dot(a_ref[rows, :], wa_ref[:, cols], preferred_element_type=F32)
            yb = jnp.dot(b_ref[rows, :], wb_ref[:, cols], preferred_element_type=F32)
            merged_ref[rows, cols] = (ga_ref[rows, cols] * ya + gb_ref[rows, cols] * yb
                                      ).astype(merged_ref.dtype)
    for rows in halves:
        ssq = jnp.zeros((tm // 2, 1), F32)
        for cols in tiles:
            y = x_ref[rows, cols] + jnp.dot(merged_ref[rows, :], wo_ref[:, cols],
                                            preferred_element_type=F32)
            o_ref[rows, cols] = y
            ssq = ssq + jnp.sum(y * y, axis=-1, keepdims=True)
        scale = lax.rsqrt(ssq * (1.0 / d) + EPS)
        h_ref[rows, :] = (o_ref[rows, :] * scale * _layer_row(gain_ref, layer)).astype(h_ref.dtype)


def _mix_out(a, b, proj, x, wa_bf, wb_bf, wo_bf, gains, layer, tm=512):
    m, k = a.shape
    d = x.shape[1]
    row_tile = pl.BlockSpec((tm, d), lambda i: (i, 0))
    return pl.pallas_call(
        functools.partial(_mix_out_kernel, layer=layer),
        grid=(m // tm,),
        in_specs=[pl.BlockSpec((tm, k), lambda i: (i, 0)),
                  pl.BlockSpec((tm, k), lambda i: (i, 0)),
                  pl.BlockSpec((tm, d), lambda i: (i, 0)),
                  pl.BlockSpec((tm, d), lambda i: (i, 1)),
                  row_tile, _whole(wa_bf), _whole(wb_bf), _whole(wo_bf), _whole(gains)],
        out_specs=[row_tile, row_tile],
        out_shape=[jax.ShapeDtypeStruct((m, d), F32), jax.ShapeDtypeStruct((m, d), BF16)],
        scratch_shapes=[pltpu.VMEM((tm, d), BF16)],
        compiler_params=_params(1),
        name="mix_out",
    )(a, b, proj, proj, x, wa_bf, wb_bf, wo_bf, gains)


def _down_out_kernel(act_ref, x_ref, w_ref, *rest, layer):
    gain_ref, o_ref, h_ref = rest if len(rest) == 3 else (None, rest[0], None)
    tm, d = o_ref.shape
    tiles = [slice(t * TN, (t + 1) * TN) for t in range(d // TN)]
    for rows in (pl.ds(r, tm // 2) for r in (0, tm // 2)):
        ssq = jnp.zeros((tm // 2, 1), F32)
        for cols in tiles:
            y = x_ref[rows, cols] + jnp.dot(act_ref[rows, :], w_ref[:, cols],
                                            preferred_element_type=F32)
            o_ref[rows, cols] = y
            if h_ref is not None:
                ssq = ssq + jnp.sum(y * y, axis=-1, keepdims=True)
        if h_ref is not None:
            scale = lax.rsqrt(ssq * (1.0 / d) + EPS)
            h_ref[rows, :] = (o_ref[rows, :] * scale * _layer_row(gain_ref, layer)).astype(h_ref.dtype)


def _down_out(act, x, w_bf, gains, layer, tm=512):
    m, k = act.shape
    d = x.shape[1]
    row_tile = pl.BlockSpec((tm, d), lambda i: (i, 0))
    in_specs = [pl.BlockSpec((tm, k), lambda i: (i, 0)), row_tile, _whole(w_bf)]
    operands = [act, x, w_bf]
    out_specs, out_shape = [row_tile], [jax.ShapeDtypeStruct((m, d), F32)]
    if gains is not None:
        in_specs.append(_whole(gains))
        operands.append(gains)
        out_specs.append(row_tile)
        out_shape.append(jax.ShapeDtypeStruct((m, d), BF16))
    return pl.pallas_call(
        functools.partial(_down_out_kernel, layer=layer),
        grid=(m // tm,),
        in_specs=in_specs,
        out_specs=out_specs,
        out_shape=out_shape,
        compiler_params=_params(1),
        name="down_out" if gains is not None else "down_out_last",
    )(*operands)


ATTN_BLOCKS_PER_STEP = 8


def _attn_kernel(sinks_ref, q_ref, kv_ref, cos_ref, sin_ref, cos_t_ref, sin_t_ref, gq_t_ref, gk_ref,
                 wa_ref, wb_ref, wo_ref, o_ref, wa_bf_ref, wb_bf_ref, wo_bf_ref,
                 k_prev_ref, vt_prev_ref, *, layer):
    n = pl.program_id(1)
    half_dim = HEAD_DIM // 2

    for src, dst in ((wa_ref, wa_bf_ref), (wb_ref, wb_bf_ref), (wo_ref, wo_bf_ref)):
        dst[...] = src[...].astype(dst.dtype)

    @pl.when(n == 0)
    def _():
        k_prev_ref[...] = jnp.zeros_like(k_prev_ref)
        vt_prev_ref[...] = jnp.zeros_like(vt_prev_ref)

    lane = lax.broadcasted_iota(jnp.int32, (1, V7X_LANES), 1)
    even_quarter = ((lane // half_dim) % 2) == 0
    r = lax.broadcasted_iota(jnp.int32, (V7X_LANES, V7X_LANES), 0)
    c = lax.broadcasted_iota(jnp.int32, (V7X_LANES, V7X_LANES), 1)
    head_mean = jnp.where((r // HEAD_DIM) == (c // HEAD_DIM), 1.0 / HEAD_DIM, 0.0).astype(BF16)
    key = lax.broadcasted_iota(jnp.int32, (2 * WINDOW, WINDOW), 0)
    qry = lax.broadcasted_iota(jnp.int32, (2 * WINDOW, WINDOW), 1)
    diff = qry + WINDOW - key
    band = (diff >= 0) & (diff < WINDOW)
    gq_t = gq_t_ref[...]
    zeros = jnp.zeros((HEAD_DIM, WINDOW), BF16)
    n_cols = KV_WIDTH // V7X_LANES
    heads_per_col = N_Q_HEADS // n_cols
    group_w = Q_PER_KV * WINDOW
    sink_row = [jnp.full((1, WINDOW), sinks_ref[layer, h] * LOG2_E, F32) for h in range(N_Q_HEADS)]

    def one_block(rows, k_prev, vt_prev, mask_bias):
        cos, sin = cos_ref[rows, :], sin_ref[rows, :]
        k_cols = []
        for col in range(n_cols):
            x = kv_ref[rows, col * V7X_LANES:(col + 1) * V7X_LANES].astype(F32)
            ms = jnp.dot((x * x).astype(BF16), head_mean, preferred_element_type=F32)
            y = x * lax.rsqrt(ms + EPS) * gk_ref[...]
            partner = jnp.where(even_quarter,
                                pltpu.roll(y, V7X_LANES - half_dim, 1),
                                pltpu.roll(y, half_dim, 1))
            k_cols.append((y * cos + partner * sin).astype(BF16))
        k_win = [jnp.concatenate([k_prev[col], k_cols[col]], axis=0) for col in range(n_cols)]
        vt_cur = kv_ref[rows, KV_WIDTH:].astype(F32).T.astype(BF16)
        vt_win = jnp.concatenate([vt_prev, vt_cur], axis=1)

        cos_t, sin_t = cos_t_ref[:, rows], sin_t_ref[:, rows]
        qt_pads = []
        for pair in range(N_Q_HEADS // 2):
            sl = slice(pair * V7X_LANES, (pair + 1) * V7X_LANES)
            qt_pair = q_ref[rows, sl].astype(F32).T
            for half in range(2):
                kv_head = (2 * pair + half) // Q_PER_KV
                x = qt_pair[half * HEAD_DIM:(half + 1) * HEAD_DIM, :]
                y = x * lax.rsqrt(jnp.mean(x * x, axis=0, keepdims=True) + EPS) * gq_t
                y1, y2 = y[:half_dim], y[half_dim:]
                qt = jnp.concatenate([y1 * cos_t - y2 * sin_t, y2 * cos_t + y1 * sin_t],
                                     axis=0).astype(BF16)
                qt_pads.append(jnp.concatenate([zeros, qt] if kv_head % 2 else [qt, zeros], axis=0))

        probs, sink_terms = [], []
        for col in range(n_cols):
            heads = range(col * heads_per_col, (col + 1) * heads_per_col)
            s = jnp.dot(k_win[col], jnp.concatenate([qt_pads[h] for h in heads], axis=1),
                        preferred_element_type=F32)
            s = s + jnp.concatenate([mask_bias] * heads_per_col, axis=1)
            sink = jnp.concatenate([sink_row[h] for h in heads], axis=1)
            m = jnp.maximum(jnp.max(s, axis=0, keepdims=True), sink)
            probs.append(jnp.exp2(s - m).astype(BF16))
            sink_terms.append(jnp.exp2(sink - m))

        ones_rows = jnp.ones((V7X_SUBLANES_BF16, 2 * WINDOW), BF16)
        for kv_head in range(N_KV_HEADS):
            col, part = divmod(kv_head, N_KV_HEADS // n_cols)
            lanes = slice(part * group_w, (part + 1) * group_w)
            vt = jnp.concatenate([vt_win[kv_head * HEAD_DIM:(kv_head + 1) * HEAD_DIM, :], ones_rows],
                                 axis=0)
            o = jnp.dot(vt, probs[col][:, lanes], preferred_element_type=F32)
            denom = o[HEAD_DIM:HEAD_DIM + 1, :] + sink_terms[col][:, lanes]
            o = o[:HEAD_DIM, :] * (1.0 / denom)
            for pair in range(Q_PER_KV // 2):
                o_pair = jnp.concatenate([o[:, (2 * pair) * WINDOW:(2 * pair + 1) * WINDOW],
                                          o[:, (2 * pair + 1) * WINDOW:(2 * pair + 2) * WINDOW]], axis=0)
                out_lane = (kv_head * Q_PER_KV + 2 * pair) * HEAD_DIM
                o_ref[rows, out_lane:out_lane + V7X_LANES] = o_pair.T.astype(o_ref.dtype)
        return k_cols, vt_cur

    k_prev = [k_prev_ref[col] for col in range(n_cols)]
    vt_prev = vt_prev_ref[...]
    for blk in range(o_ref.shape[0] // WINDOW):
        allowed = band & ((key >= WINDOW) | (n > 0)) if blk == 0 else band
        mask_bias = jnp.where(allowed, 0.0, MASK_VALUE)
        k_prev, vt_prev = one_block(pl.ds(blk * WINDOW, WINDOW), k_prev, vt_prev, mask_bias)

    for col in range(n_cols):
        k_prev_ref[col] = k_prev[col]
    vt_prev_ref[...] = vt_prev


def _attention(proj, sinks, tables, gq_t, gk, wa, wb, wo, layer, batch, seq):
    m = proj.shape[0]
    rows = ATTN_BLOCKS_PER_STEP * WINDOW
    nb = seq // rows
    steps = batch * nb

    def slab(w):
        return pl.BlockSpec((None, w.shape[1] // steps, w.shape[2]),
                            lambda b, n: (layer, b * nb + n, 0))

    def slab_out(w):
        return pl.BlockSpec((w.shape[1] // steps, w.shape[2]), lambda b, n: (b * nb + n, 0))

    q_blk = PROJ_QKV_COL // ATTN_WIDTH
    kv_blk = (PROJ_QKV_COL + ATTN_WIDTH) // (2 * KV_WIDTH)
    cos, sin_signed, cos_t, sin_t = tables
    half_dim = HEAD_DIM // 2
    row_table = pl.BlockSpec((rows, V7X_LANES), lambda b, n: (n, 0))
    col_table = pl.BlockSpec((half_dim, rows), lambda b, n: (0, n))
    return pl.pallas_call(
        functools.partial(_attn_kernel, layer=layer),
        grid=(batch, nb),
        in_specs=[pl.BlockSpec(memory_space=pltpu.SMEM),
                  pl.BlockSpec((rows, ATTN_WIDTH), lambda b, n: (b * nb + n, q_blk)),
                  pl.BlockSpec((rows, 2 * KV_WIDTH), lambda b, n: (b * nb + n, kv_blk)),
                  row_table, row_table, col_table, col_table,
                  pl.BlockSpec((None, HEAD_DIM, WINDOW), lambda b, n: (layer, 0, 0)),
                  pl.BlockSpec((None, 1, V7X_LANES), lambda b, n: (layer, 0, 0)),
                  slab(wa), slab(wb), slab(wo)],
        out_specs=[pl.BlockSpec((rows, ATTN_WIDTH), lambda b, n: (b * nb + n, 0)),
                   slab_out(wa), slab_out(wb), slab_out(wo)],
        out_shape=[jax.ShapeDtypeStruct((m, ATTN_WIDTH), BF16)]
        + [jax.ShapeDtypeStruct(w.shape[1:], BF16) for w in (wa, wb, wo)],
        scratch_shapes=[pltpu.VMEM((KV_WIDTH // V7X_LANES, WINDOW, V7X_LANES), BF16),
                        pltpu.VMEM((KV_WIDTH, WINDOW), BF16)],
        compiler_params=_params(2),
        name="swa_attention",
    )(sinks, proj, proj, cos, sin_signed, cos_t, sin_t, gq_t, gk, wa, wb, wo)


def _sgu_kernel(u0_ref, u1_ref, v0_ref, v1_ref, w_ref, bt_ref, g_ref, b_ref, o_ref, *, layer):
    r = lax.broadcasted_iota(jnp.int32, (CHUNK, CHUNK), 0)
    c = lax.broadcasted_iota(jnp.int32, (CHUNK, CHUNK), 1)
    causal = r >= c
    lane_mean = jnp.full((SGU_GROUP_DIM, SGU_GROUP_DIM), 1.0 / SGU_GROUP_DIM, BF16)
    groups_per_tile = TN // SGU_GROUP_DIM
    n_chunks = o_ref.shape[0] // CHUNK
    groups = range(SGU_GROUPS)

    def lanes(grp):
        return slice(grp * SGU_GROUP_DIM, (grp + 1) * SGU_GROUP_DIM)

    def tile_lanes(grp):
        return lanes(grp % groups_per_tile)

    vs = [(v0_ref, v1_ref)[grp // groups_per_tile][:, tile_lanes(grp)].astype(F32) for grp in groups]
    mus = [jnp.dot(v.astype(BF16), lane_mean, preferred_element_type=F32) for v in vs]
    dvs = [v - mu for v, mu in zip(vs, mus)]
    vars_ = [jnp.dot((dv * dv).astype(BF16), lane_mean, preferred_element_type=F32) for dv in dvs]
    ss = []
    for grp in groups:
        vn = (dvs[grp] * lax.rsqrt(vars_[grp] + EPS) * g_ref[layer:layer + 1, lanes(grp)]
              + b_ref[layer:layer + 1, lanes(grp)]).astype(BF16)
        w = jnp.where(causal, w_ref[grp], 0.0).astype(BF16)
        vn_chunks = [vn[ch * CHUNK:(ch + 1) * CHUNK, :] for ch in range(n_chunks)]
        ss.append(jnp.dot(w, jnp.concatenate(vn_chunks, axis=-1), preferred_element_type=F32))
    for grp in groups:
        s = ss[grp] + bt_ref[:, grp:grp + 1]
        u_ref = (u0_ref, u1_ref)[grp // groups_per_tile]
        for ch in range(n_chunks):
            u = u_ref[ch * CHUNK:(ch + 1) * CHUNK, tile_lanes(grp)].astype(F32)
            o_ref[ch * CHUNK:(ch + 1) * CHUNK, lanes(grp)] = (
                u * s[:, ch * CHUNK:(ch + 1) * CHUNK]).astype(o_ref.dtype)


def _sgu(proj, w_s, b_t, ln_g, ln_b, layer, rows=8 * CHUNK):
    m = proj.shape[0]
    u0 = PROJ_UV_TILE

    def tile(t):
        return pl.BlockSpec((rows, TN), lambda i: (i, t))

    return pl.pallas_call(
        functools.partial(_sgu_kernel, layer=layer),
        grid=(m // rows,),
        in_specs=[tile(u0), tile(u0 + 1), tile(u0 + 2), tile(u0 + 3),
                  pl.BlockSpec((None, SGU_GROUPS, CHUNK, CHUNK), lambda i: (layer, 0, 0, 0)),
                  pl.BlockSpec((None, CHUNK, SGU_GROUPS), lambda i: (layer, 0, 0)),
                  _whole(ln_g), _whole(ln_b)],
        out_specs=pl.BlockSpec((rows, SGU_WIDTH), lambda i: (i, 0)),
        out_shape=jax.ShapeDtypeStruct((m, SGU_WIDTH), BF16),
        compiler_params=_params(1),
        name="sgu",
    )(proj, proj, proj, proj, w_s, b_t, ln_g, ln_b)


def _rope_tables(seq):
    pos = jnp.arange(seq, dtype=F32)
    inv_freq = jnp.power(ROPE_THETA, -jnp.arange(0, HEAD_DIM, 2, dtype=F32) / HEAD_DIM)
    ang = pos[:, None] * inv_freq[None, :]
    cos, sin = jnp.cos(ang), jnp.sin(ang)
    reps = V7X_LANES // HEAD_DIM
    ang_t = inv_freq[:, None] * pos[None, :]
    return (jnp.tile(cos, (1, 2 * reps)), jnp.tile(jnp.concatenate([-sin, sin], axis=-1), (1, reps)),
            jnp.cos(ang_t), jnp.sin(ang_t))


def kernel(x, mix_norm, w_in, q_norm, k_norm, sinks, sgu_ln_g, sgu_ln_b, w_spatial, b_spatial,
           w_attn_branch, w_sgu_branch, w_out, ffn_norm, w_gate, w_up, w_down):
    batch, seq, d = x.shape
    depth = w_in.shape[0]
    m = batch * seq
    assert d == D_MODEL and w_in.shape[-1] == (QKV_TILES + UV_TILES + GATE_TILES) * TN
    tables = _rope_tables(seq)
    gq_t = jnp.broadcast_to((q_norm * (HEAD_DIM ** -0.5 * LOG2_E))[:, :, None],
                            (depth, HEAD_DIM, WINDOW))
    gk = jnp.tile(k_norm, (1, V7X_LANES // HEAD_DIM)).reshape(depth, 1, V7X_LANES)
    b_t = jnp.swapaxes(b_spatial, 1, 2)

    xf = x.reshape(m, d)
    h = _rmsnorm(xf, mix_norm, 0)
    for l in range(depth):
        proj = _in_proj(h, w_in, l)
        a, wa_bf, wb_bf, wo_bf = _attention(proj, sinks, tables, gq_t, gk,
                                            w_attn_branch, w_sgu_branch, w_out, l, batch, seq)
        b = _sgu(proj, w_spatial, b_t, sgu_ln_g, sgu_ln_b, l)
        xf, h2 = _mix_out(a, b, proj, xf, wa_bf, wb_bf, wo_bf, ffn_norm, l)
        act, wd_bf = _swiglu(h2, w_gate, w_up, w_down, l)
        if l + 1 < depth:
            xf, h = _down_out(act, xf, wd_bf, mix_norm, l + 1)
        else:
            (xf,) = _down_out(act, xf, wd_bf, None, l)
    return xf.reshape(batch, seq, d)
```

```python
import functools

import jax
import jax.numpy as jnp
from jax import lax
from jax.experimental import pallas as pl
from jax.experimental.pallas import tpu as pltpu

F32 = jnp.float32
BF16 = jnp.bfloat16

HEAD_DIM = 64
N_Q_HEADS = 16
N_KV_HEADS = 4
Q_PER_KV = N_Q_HEADS // N_KV_HEADS
ATTN_WIDTH = N_Q_HEADS * HEAD_DIM
KV_WIDTH = N_KV_HEADS * HEAD_DIM
WINDOW = 128
ROPE_THETA = 10000.0
SGU_GROUPS = 8
SGU_GROUP_DIM = 128
SGU_WIDTH = SGU_GROUPS * SGU_GROUP_DIM
CHUNK = 128
EPS = 1e-6
MASK_VALUE = -1e30
LOG2_E = 1.4426950408889634

V7X_LANES = 128
V7X_SUBLANES_BF16 = 16
V7X_VMEM_BYTES = 64 * 1024 * 1024
VMEM_LIMIT = V7X_VMEM_BYTES - 8 * 1024 * 1024

TN = 512
D_MODEL = 2048
QKV_TILES = (ATTN_WIDTH + 2 * KV_WIDTH) // TN
UV_TILES = 2 * SGU_WIDTH // TN
GATE_TILES = 2 * D_MODEL // TN
PROJ_QKV_COL = GATE_TILES * TN
PROJ_UV_TILE = GATE_TILES + QKV_TILES


def _params(n_axes):
    return pltpu.CompilerParams(dimension_semantics=("arbitrary",) * n_axes,
                                vmem_limit_bytes=VMEM_LIMIT)


def _layer_row(ref, layer):
    return ref[layer:layer + 1, :]


def _whole(p):
    return pl.BlockSpec(p.shape, lambda *_: (0,) * p.ndim)


def _rmsnorm_kernel(x_ref, g_ref, o_ref, *, layer):
    x = x_ref[...]
    y = x * lax.rsqrt(jnp.mean(x * x, axis=-1, keepdims=True) + EPS)
    o_ref[...] = (y * _layer_row(g_ref, layer)).astype(o_ref.dtype)


def _rmsnorm(x, gains, layer, tm=512):
    m, d = x.shape
    return pl.pallas_call(
        functools.partial(_rmsnorm_kernel, layer=layer),
        grid=(m // tm,),
        in_specs=[pl.BlockSpec((tm, d), lambda i: (i, 0)), _whole(gains)],
        out_specs=pl.BlockSpec((tm, d), lambda i: (i, 0)),
        out_shape=jax.ShapeDtypeStruct((m, d), BF16),
        compiler_params=_params(1),
        name="rmsnorm",
    )(x, gains)


SUB_ROWS = 512


def _row_blocks(o_ref):
    tm = o_ref.shape[0]
    sub = min(SUB_ROWS, tm // 2)
    return [pl.ds(r, sub) for r in range(0, tm, sub)]


def _cast_weights(w_refs, wbf_refs):
    @pl.when(pl.program_id(1) == 0)
    def _():
        for w_ref, wbf_ref in zip(w_refs, wbf_refs):
            wbf_ref[...] = w_ref[...].astype(BF16)


def _sigmoid(y):
    return 0.5 * jnp.tanh(0.5 * y) + 0.5


def _gelu_tanh(y):
    c = 0.7978845608028654
    half = 0.5 * y
    return half + half * jnp.tanh(y * (c + (c * 0.044715) * (y * y)))


def _in_proj_kernel(h_ref, w_ref, o_ref, wbf_ref):
    _cast_weights([w_ref], [wbf_ref])
    j = pl.program_id(0)

    def emit(act):
        for rows in _row_blocks(o_ref):
            y = jnp.dot(h_ref[rows, :], wbf_ref[...], preferred_element_type=F32)
            o_ref[rows, :] = (y if act is None else act(y)).astype(o_ref.dtype)

    pl.when(j < QKV_TILES)(lambda: emit(None))
    pl.when((j >= QKV_TILES) & (j < QKV_TILES + UV_TILES))(
        lambda: emit(_gelu_tanh))
    pl.when(j >= QKV_TILES + UV_TILES)(lambda: emit(_sigmoid))


def _in_proj(h, w, layer, tm=4096):
    m, k = h.shape
    n = w.shape[-1]
    return pl.pallas_call(
        _in_proj_kernel,
        grid=(n // TN, m // tm),
        in_specs=[pl.BlockSpec((tm, k), lambda j, i: (i, 0)),
                  pl.BlockSpec((None, k, TN), lambda j, i: (layer, 0, j))],
        out_specs=pl.BlockSpec(
            (tm, TN), lambda j, i: (i, jnp.where(j < QKV_TILES + UV_TILES, j + GATE_TILES,
                                                 j - (QKV_TILES + UV_TILES)))),
        out_shape=jax.ShapeDtypeStruct((m, n), BF16),
        scratch_shapes=[pltpu.VMEM((k, TN), BF16)],
        compiler_params=_params(2),
        name="in_proj",
    )(h, w)


def _swiglu_kernel(h_ref, wg_ref, wu_ref, wd_ref, o_ref, wd_bf_ref, wgbf_ref, wubf_ref):
    _cast_weights([wg_ref, wu_ref], [wgbf_ref, wubf_ref])
    wd_bf_ref[...] = wd_ref[...].astype(wd_bf_ref.dtype)
    for rows in _row_blocks(o_ref):
        h = h_ref[rows, :]
        g = jnp.dot(h, wgbf_ref[...], preferred_element_type=F32)
        u = jnp.dot(h, wubf_ref[...], preferred_element_type=F32)
        o_ref[rows, :] = (g * _sigmoid(g) * u).astype(o_ref.dtype)


def _swiglu(h, wg, wu, wd, layer, tm=2048):
    m, k = h.shape
    f = wg.shape[-1]
    mt = m // tm
    slab = wd.shape[1] // (f // TN * mt)
    w_spec = pl.BlockSpec((None, k, TN), lambda j, i: (layer, 0, j))
    return pl.pallas_call(
        _swiglu_kernel,
        grid=(f // TN, mt),
        in_specs=[pl.BlockSpec((tm, k), lambda j, i: (i, 0)), w_spec, w_spec,
                  pl.BlockSpec((None, slab, wd.shape[2]), lambda j, i: (layer, j * mt + i, 0))],
        out_specs=[pl.BlockSpec((tm, TN), lambda j, i: (i, j)),
                   pl.BlockSpec((slab, wd.shape[2]), lambda j, i: (j * mt + i, 0))],
        out_shape=[jax.ShapeDtypeStruct((m, f), BF16), jax.ShapeDtypeStruct(wd.shape[1:], BF16)],
        scratch_shapes=[pltpu.VMEM((k, TN), BF16), pltpu.VMEM((k, TN), BF16)],
        compiler_params=_params(2),
        name="swiglu",
    )(h, wg, wu, wd)


def _mix_out_kernel(a_ref, b_ref, ga_ref, gb_ref, x_ref, wa_ref, wb_ref, wo_ref, gain_ref,
                    o_ref, h_ref, merged_ref, *, layer):
    tm, d = o_ref.shape
    tiles = [slice(t * TN, (t + 1) * TN) for t in range(d // TN)]
    halves = [pl.ds(r, tm // 2) for r in (0, tm // 2)]
    for rows in halves:
        for cols in tiles:
            ya = jnp.dot(a_ref[rows, :], wa_ref[:, cols], preferred_element_type=F32)
            yb = jnp.dot(b_ref[rows, :], wb_ref[:, cols], preferred_element_type=F32)
            merged_ref[rows, cols] = (ga_ref[rows, cols] * ya + gb_ref[rows, cols] * yb
                                      ).astype(merged_ref.dtype)
    for rows in halves:
        ssq = jnp.zeros((tm // 2, 1), F32)
        for cols in tiles:
            y = x_ref[rows, cols] + jnp.dot(merged_ref[rows, :], wo_ref[:, cols],
                                            preferred_element_type=F32)
            o_ref[rows, cols] = y
            ssq = ssq + jnp.sum(y * y, axis=-1, keepdims=True)
        scale = lax.rsqrt(ssq * (1.0 / d) + EPS)
        h_ref[rows, :] = (o_ref[rows, :] * scale * _layer_row(gain_ref, layer)).astype(h_ref.dtype)


def _mix_out(a, b, proj, x, wa_bf, wb_bf, wo_bf, gains, layer, tm=512):
    m, k = a.shape
    d = x.shape[1]
    row_tile = pl.BlockSpec((tm, d), lambda i: (i, 0))
    return pl.pallas_call(
        functools.partial(_mix_out_kernel, layer=layer),
        grid=(m // tm,),
        in_specs=[pl.BlockSpec((tm, k), lambda i: (i, 0)),
                  pl.BlockSpec((tm, k), lambda i: (i, 0)),
                  pl.BlockSpec((tm, d), lambda i: (i, 0)),
                  pl.BlockSpec((tm, d), lambda i: (i, 1)),
                  row_tile, _whole(wa_bf), _whole(wb_bf), _whole(wo_bf), _whole(gains)],
        out_specs=[row_tile, row_tile],
        out_shape=[jax.ShapeDtypeStruct((m, d), F32), jax.ShapeDtypeStruct((m, d), BF16)],
        scratch_shapes=[pltpu.VMEM((tm, d), BF16)],
        compiler_params=_params(1),
        name="mix_out",
    )(a, b, proj, proj, x, wa_bf, wb_bf, wo_bf, gains)


def _down_out_kernel(act_ref, x_ref, w_hbm_ref, *rest, layer):
    *io_refs, w_ref, sem = rest
    gain_ref, o_ref, h_ref = io_refs if len(io_refs) == 3 else (None, io_refs[0], None)
    tm, d = o_ref.shape
    tiles = [slice(t * TN, (t + 1) * TN) for t in range(d // TN)]
    copies = [pltpu.make_async_copy(w_hbm_ref.at[:, cols], w_ref.at[:, cols], sem.at[t])
              for t, cols in enumerate(tiles)]

    def body(first_step):
        if first_step:
            for copy in copies:
                copy.start()
        for half, rows in enumerate(pl.ds(r, tm // 2) for r in (0, tm // 2)):
            ssq = jnp.zeros((tm // 2, 1), F32)
            for t, cols in enumerate(tiles):
                if first_step and half == 0:
                    copies[t].wait()
                y = x_ref[rows, cols] + jnp.dot(act_ref[rows, :], w_ref[:, cols],
                                                preferred_element_type=F32)
                o_ref[rows, cols] = y
                if h_ref is not None:
                    ssq = ssq + jnp.sum(y * y, axis=-1, keepdims=True)
            if h_ref is not None:
                scale = lax.rsqrt(ssq * (1.0 / d) + EPS)
                h_ref[rows, :] = (o_ref[rows, :] * scale
                                  * _layer_row(gain_ref, layer)).astype(h_ref.dtype)

    first = pl.program_id(0) == 0
    pl.when(first)(lambda: body(True))
    pl.when(jnp.logical_not(first))(lambda: body(False))


def _down_out(act, x, w_bf, gains, layer, tm=512):
    m, k = act.shape
    d = x.shape[1]
    row_tile = pl.BlockSpec((tm, d), lambda i: (i, 0))
    in_specs = [pl.BlockSpec((tm, k), lambda i: (i, 0)), row_tile,
                pl.BlockSpec(memory_space=pl.ANY)]
    operands = [act, x, w_bf]
    out_specs, out_shape = [row_tile], [jax.ShapeDtypeStruct((m, d), F32)]
    if gains is not None:
        in_specs.append(_whole(gains))
        operands.append(gains)
        out_specs.append(row_tile)
        out_shape.append(jax.ShapeDtypeStruct((m, d), BF16))
    return pl.pallas_call(
        functools.partial(_down_out_kernel, layer=layer),
        grid=(m // tm,),
        in_specs=in_specs,
        out_specs=out_specs,
        out_shape=out_shape,
        scratch_shapes=[pltpu.VMEM(w_bf.shape, BF16), pltpu.SemaphoreType.DMA((d // TN,))],
        compiler_params=_params(1),
        name="down_out" if gains is not None else "down_out_last",
    )(*operands)


ATTN_BLOCKS_PER_STEP = 8


def _attn_kernel(sinks_ref, q_ref, kv_ref, cos_ref, sin_ref, cos_t_ref, sin_t_ref, gq_t_ref, gk_ref,
                 wa_ref, wb_ref, wo_ref, o_ref, wa_bf_ref, wb_bf_ref, wo_bf_ref,
                 k_prev_ref, vt_prev_ref):
    n = pl.program_id(1)
    half_dim = HEAD_DIM // 2

    for src, dst in ((wa_ref, wa_bf_ref), (wb_ref, wb_bf_ref), (wo_ref, wo_bf_ref)):
        dst[...] = src[...].astype(dst.dtype)

    @pl.when(n == 0)
    def _():
        k_prev_ref[...] = jnp.zeros_like(k_prev_ref)
        vt_prev_ref[...] = jnp.zeros_like(vt_prev_ref)

    lane = lax.broadcasted_iota(jnp.int32, (1, V7X_LANES), 1)
    even_quarter = ((lane // half_dim) % 2) == 0
    r = lax.broadcasted_iota(jnp.int32, (V7X_LANES, V7X_LANES), 0)
    c = lax.broadcasted_iota(jnp.int32, (V7X_LANES, V7X_LANES), 1)
    head_mean = jnp.where((r // HEAD_DIM) == (c // HEAD_DIM), 1.0 / HEAD_DIM, 0.0).astype(BF16)
    key = lax.broadcasted_iota(jnp.int32, (2 * WINDOW, WINDOW), 0)
    qry = lax.broadcasted_iota(jnp.int32, (2 * WINDOW, WINDOW), 1)
    diff = qry + WINDOW - key
    band = (diff >= 0) & (diff < WINDOW)
    gq_t = gq_t_ref[...]
    zeros = jnp.zeros((HEAD_DIM, WINDOW), BF16)
    n_cols = KV_WIDTH // V7X_LANES
    heads_per_col = N_Q_HEADS // n_cols
    group_w = Q_PER_KV * WINDOW
    sink_row = [jnp.full((1, WINDOW), sinks_ref[h] * LOG2_E, F32) for h in range(N_Q_HEADS)]

    def one_block(rows, k_prev, vt_prev, allowed):
        cos, sin = cos_ref[rows, :], sin_ref[rows, :]
        k_cols = []
        for col in range(n_cols):
            x = kv_ref[rows, col * V7X_LANES:(col + 1) * V7X_LANES].astype(F32)
            ms = jnp.dot((x * x).astype(BF16), head_mean, preferred_element_type=F32)
            y = x * lax.rsqrt(ms + EPS) * gk_ref[...]
            partner = jnp.where(even_quarter,
                                pltpu.roll(y, V7X_LANES - half_dim, 1),
                                pltpu.roll(y, half_dim, 1))
            k_cols.append((y * cos + partner * sin).astype(BF16))
        k_win = [jnp.concatenate([k_prev[col], k_cols[col]], axis=0) for col in range(n_cols)]
        vt_cur = kv_ref[rows, KV_WIDTH:].astype(F32).T.astype(BF16)
        vt_win = jnp.concatenate([vt_prev, vt_cur], axis=1)

        cos_t, sin_t = cos_t_ref[:, rows], sin_t_ref[:, rows]
        qt_pads = []
        for pair in range(N_Q_HEADS // 2):
            sl = slice(pair * V7X_LANES, (pair + 1) * V7X_LANES)
            qt_pair = q_ref[rows, sl].astype(F32).T
            for half in range(2):
                kv_head = (2 * pair + half) // Q_PER_KV
                x = qt_pair[half * HEAD_DIM:(half + 1) * HEAD_DIM, :]
                y = x * lax.rsqrt(jnp.mean(x * x, axis=0, keepdims=True) + EPS) * gq_t
                y1, y2 = y[:half_dim], y[half_dim:]
                qt = jnp.concatenate([y1 * cos_t - y2 * sin_t, y2 * cos_t + y1 * sin_t],
                                     axis=0).astype(BF16)
                qt_pads.append(jnp.concatenate([zeros, qt] if kv_head % 2 else [qt, zeros], axis=0))

        probs, sink_terms = [], []
        for col in range(n_cols):
            heads = range(col * heads_per_col, (col + 1) * heads_per_col)
            s = jnp.dot(k_win[col], jnp.concatenate([qt_pads[h] for h in heads], axis=1),
                        preferred_element_type=F32)
            s = jnp.where(jnp.concatenate([allowed] * heads_per_col, axis=1), s, MASK_VALUE)
            sink = jnp.concatenate([sink_row[h] for h in heads], axis=1)
            m = jnp.maximum(jnp.max(s, axis=0, keepdims=True), sink)
            probs.append(jnp.exp2(s - m).astype(BF16))
            sink_terms.append(jnp.exp2(sink - m))

        ones_rows = jnp.ones((V7X_SUBLANES_BF16, 2 * WINDOW), BF16)
        for kv_head in range(N_KV_HEADS):
            col, part = divmod(kv_head, N_KV_HEADS // n_cols)
            lanes = slice(part * group_w, (part + 1) * group_w)
            vt = jnp.concatenate([vt_win[kv_head * HEAD_DIM:(kv_head + 1) * HEAD_DIM, :], ones_rows],
                                 axis=0)
            o = jnp.dot(vt, probs[col][:, lanes], preferred_element_type=F32)
            denom = o[HEAD_DIM:HEAD_DIM + 1, :] + sink_terms[col][:, lanes]
            o = o[:HEAD_DIM, :] * (1.0 / denom)
            for pair in range(Q_PER_KV // 2):
                o_pair = jnp.concatenate([o[:, (2 * pair) * WINDOW:(2 * pair + 1) * WINDOW],
                                          o[:, (2 * pair + 1) * WINDOW:(2 * pair + 2) * WINDOW]], axis=0)
                out_lane = (kv_head * Q_PER_KV + 2 * pair) * HEAD_DIM
                o_ref[rows, out_lane:out_lane + V7X_LANES] = o_pair.T.astype(o_ref.dtype)
        return k_cols, vt_cur

    k_prev = [k_prev_ref[col] for col in range(n_cols)]
    vt_prev = vt_prev_ref[...]
    for blk in range(o_ref.shape[0] // WINDOW):
        allowed = band & ((key >= WINDOW) | (n > 0)) if blk == 0 else band
        k_prev, vt_prev = one_block(pl.ds(blk * WINDOW, WINDOW), k_prev, vt_prev, allowed)

    for col in range(n_cols):
        k_prev_ref[col] = k_prev[col]
    vt_prev_ref[...] = vt_prev


def _attention(proj, sinks, tables, gq_t, gk, wa, wb, wo, layer, batch, seq):
    m = proj.shape[0]
    rows = ATTN_BLOCKS_PER_STEP * WINDOW
    nb = seq // rows
    steps = batch * nb

    def slab(w):
        return pl.BlockSpec((None, w.shape[1] // steps, w.shape[2]),
                            lambda b, n: (layer, b * nb + n, 0))

    def slab_out(w):
        return pl.BlockSpec((w.shape[1] // steps, w.shape[2]), lambda b, n: (b * nb + n, 0))

    q_blk = PROJ_QKV_COL // ATTN_WIDTH
    kv_blk = (PROJ_QKV_COL + ATTN_WIDTH) // (2 * KV_WIDTH)
    cos, sin_signed, cos_t, sin_t = tables
    half_dim = HEAD_DIM // 2
    row_table = pl.BlockSpec((rows, V7X_LANES), lambda b, n: (n, 0))
    col_table = pl.BlockSpec((half_dim, rows), lambda b, n: (0, n))
    return pl.pallas_call(
        _attn_kernel,
        grid=(batch, nb),
        in_specs=[pl.BlockSpec(memory_space=pltpu.SMEM),
                  pl.BlockSpec((rows, ATTN_WIDTH), lambda b, n: (b * nb + n, q_blk)),
                  pl.BlockSpec((rows, 2 * KV_WIDTH), lambda b, n: (b * nb + n, kv_blk)),
                  row_table, row_table, col_table, col_table,
                  pl.BlockSpec((None, HEAD_DIM, WINDOW), lambda b, n: (layer, 0, 0)),
                  pl.BlockSpec((None, 1, V7X_LANES), lambda b, n: (layer, 0, 0)),
                  slab(wa), slab(wb), slab(wo)],
        out_specs=[pl.BlockSpec((rows, ATTN_WIDTH), lambda b, n: (b * nb + n, 0)),
                   slab_out(wa), slab_out(wb), slab_out(wo)],
        out_shape=[jax.ShapeDtypeStruct((m, ATTN_WIDTH), BF16)]
        + [jax.ShapeDtypeStruct(w.shape[1:], BF16) for w in (wa, wb, wo)],
        scratch_shapes=[pltpu.VMEM((KV_WIDTH // V7X_LANES, WINDOW, V7X_LANES), BF16),
                        pltpu.VMEM((KV_WIDTH, WINDOW), BF16)],
        compiler_params=_params(2),
        name="swa_attention",
    )(sinks, proj, proj, cos, sin_signed, cos_t, sin_t, gq_t, gk, wa, wb, wo)


def _sgu_kernel(u0_ref, u1_ref, v0_ref, v1_ref, w_ref, bt_ref, g_ref, b_ref, o_ref, *, layer):
    r = lax.broadcasted_iota(jnp.int32, (CHUNK, CHUNK), 0)
    c = lax.broadcasted_iota(jnp.int32, (CHUNK, CHUNK), 1)
    causal = r >= c
    lane_mean = jnp.full((SGU_GROUP_DIM, SGU_GROUP_DIM), 1.0 / SGU_GROUP_DIM, BF16)
    groups_per_tile = TN // SGU_GROUP_DIM
    n_chunks = o_ref.shape[0] // CHUNK
    groups = range(SGU_GROUPS)

    def lanes(grp):
        return slice(grp * SGU_GROUP_DIM, (grp + 1) * SGU_GROUP_DIM)

    def tile_lanes(grp):
        return lanes(grp % groups_per_tile)

    vs = [(v0_ref, v1_ref)[grp // groups_per_tile][:, tile_lanes(grp)].astype(F32) for grp in groups]
    mus = [jnp.dot(v.astype(BF16), lane_mean, preferred_element_type=F32) for v in vs]
    dvs = [v - mu for v, mu in zip(vs, mus)]
    vars_ = [jnp.dot((dv * dv).astype(BF16), lane_mean, preferred_element_type=F32) for dv in dvs]
    ss = []
    for grp in groups:
        vn = (dvs[grp] * lax.rsqrt(vars_[grp] + EPS) * g_ref[layer:layer + 1, lanes(grp)]
              + b_ref[layer:layer + 1, lanes(grp)]).astype(BF16)
        w = jnp.where(causal, w_ref[grp], 0.0).astype(BF16)
        vn_chunks = [vn[ch * CHUNK:(ch + 1) * CHUNK, :] for ch in range(n_chunks)]
        ss.append(jnp.dot(w, jnp.concatenate(vn_chunks, axis=-1), preferred_element_type=F32))
    for grp in groups:
        s = ss[grp] + bt_ref[:, grp:grp + 1]
        u_ref = (u0_ref, u1_ref)[grp // groups_per_tile]
        for ch in range(n_chunks):
            u = u_ref[ch * CHUNK:(ch + 1) * CHUNK, tile_lanes(grp)].astype(F32)
            o_ref[ch * CHUNK:(ch + 1) * CHUNK, lanes(grp)] = (
                u * s[:, ch * CHUNK:(ch + 1) * CHUNK]).astype(o_ref.dtype)


def _sgu(proj, w_s, b_t, ln_g, ln_b, layer, rows=8 * CHUNK):
    m = proj.shape[0]
    u0 = PROJ_UV_TILE

    def tile(t):
        return pl.BlockSpec((rows, TN), lambda i: (i, t))

    return pl.pallas_call(
        functools.partial(_sgu_kernel, layer=layer),
        grid=(m // rows,),
        in_specs=[tile(u0), tile(u0 + 1), tile(u0 + 2), tile(u0 + 3),
                  pl.BlockSpec((None, SGU_GROUPS, CHUNK, CHUNK), lambda i: (layer, 0, 0, 0)),
                  pl.BlockSpec((None, CHUNK, SGU_GROUPS), lambda i: (layer, 0, 0)),
                  _whole(ln_g), _whole(ln_b)],
        out_specs=pl.BlockSpec((rows, SGU_WIDTH), lambda i: (i, 0)),
        out_shape=jax.ShapeDtypeStruct((m, SGU_WIDTH), BF16),
        compiler_params=_params(1),
        name="sgu",
    )(proj, proj, proj, proj, w_s, b_t, ln_g, ln_b)


def _rope_tables(seq):
    pos = jnp.arange(seq, dtype=F32)
    inv_freq = jnp.power(ROPE_THETA, -jnp.arange(0, HEAD_DIM, 2, dtype=F32) / HEAD_DIM)
    ang = pos[:, None] * inv_freq[None, :]
    cos, sin = jnp.cos(ang), jnp.sin(ang)
    reps = V7X_LANES // HEAD_DIM
    return (jnp.tile(cos, (1, 2 * reps)), jnp.tile(jnp.concatenate([-sin, sin], axis=-1), (1, reps)),
            cos.T, sin.T)


def kernel(x, mix_norm, w_in, q_norm, k_norm, sinks, sgu_ln_g, sgu_ln_b, w_spatial, b_spatial,
           w_attn_branch, w_sgu_branch, w_out, ffn_norm, w_gate, w_up, w_down):
    batch, seq, d = x.shape
    depth = w_in.shape[0]
    m = batch * seq
    assert d == D_MODEL and w_in.shape[-1] == (QKV_TILES + UV_TILES + GATE_TILES) * TN
    tables = _rope_tables(seq)
    gq_t = jnp.broadcast_to((q_norm * (HEAD_DIM ** -0.5 * LOG2_E))[:, :, None],
                            (depth, HEAD_DIM, WINDOW))
    gk = jnp.tile(k_norm, (1, V7X_LANES // HEAD_DIM)).reshape(depth, 1, V7X_LANES)
    b_t = jnp.swapaxes(b_spatial, 1, 2)

    xf = x.reshape(m, d)
    h = _rmsnorm(xf, mix_norm, 0)
    for l in range(depth):
        proj = _in_proj(h, w_in, l)
        a, wa_bf, wb_bf, wo_bf = _attention(proj, sinks[l], tables, gq_t, gk,
                                            w_attn_branch, w_sgu_branch, w_out, l, batch, seq)
        b = _sgu(proj, w_spatial, b_t, sgu_ln_g, sgu_ln_b, l)
        xf, h2 = _mix_out(a, b, proj, xf, wa_bf, wb_bf, wo_bf, ffn_norm, l)
        act, wd_bf = _swiglu(h2, w_gate, w_up, w_down, l)
        if l + 1 < depth:
            xf, h = _down_out(act, xf, wd_bf, mix_norm, l + 1)
        else:
            (xf,) = _down_out(act, xf, wd_bf, None, l)
    return xf.reshape(batch, seq, d)
```

```python
import functools

import jax
import jax.numpy as jnp
from jax import lax
from jax.experimental import pallas as pl
from jax.experimental.pallas import tpu as pltpu

F32 = jnp.float32
BF16 = jnp.bfloat16

HEAD_DIM = 64
N_Q_HEADS = 16
N_KV_HEADS = 4
Q_PER_KV = N_Q_HEADS // N_KV_HEADS
ATTN_WIDTH = N_Q_HEADS * HEAD_DIM
KV_WIDTH = N_KV_HEADS * HEAD_DIM
WINDOW = 128
ROPE_THETA = 10000.0
SGU_GROUPS = 8
SGU_GROUP_DIM = 128
SGU_WIDTH = SGU_GROUPS * SGU_GROUP_DIM
CHUNK = 128
EPS = 1e-6
MASK_VALUE = -1e30
LOG2_E = 1.4426950408889634

V7X_LANES = 128
V7X_SUBLANES_BF16 = 16
V7X_VMEM_BYTES = 64 * 1024 * 1024
VMEM_LIMIT = V7X_VMEM_BYTES - 8 * 1024 * 1024

TN = 512
D_MODEL = 2048
QKV_TILES = (ATTN_WIDTH + 2 * KV_WIDTH) // TN
UV_TILES = 2 * SGU_WIDTH // TN
GATE_TILES = 2 * D_MODEL // TN
PROJ_QKV_COL = GATE_TILES * TN
PROJ_UV_TILE = GATE_TILES + QKV_TILES


def _params(n_axes):
    return pltpu.CompilerParams(dimension_semantics=("arbitrary",) * n_axes,
                                vmem_limit_bytes=VMEM_LIMIT)


def _layer_row(ref, layer):
    return ref[layer:layer + 1, :]


def _whole(p):
    return pl.BlockSpec(p.shape, lambda *_: (0,) * p.ndim)


def _rmsnorm_kernel(x_ref, g_ref, o_ref, *, layer):
    x = x_ref[...]
    y = x * lax.rsqrt(jnp.mean(x * x, axis=-1, keepdims=True) + EPS)
    o_ref[...] = (y * _layer_row(g_ref, layer)).astype(o_ref.dtype)


def _rmsnorm(x, gains, layer, tm=512):
    m, d = x.shape
    return pl.pallas_call(
        functools.partial(_rmsnorm_kernel, layer=layer),
        grid=(m // tm,),
        in_specs=[pl.BlockSpec((tm, d), lambda i: (i, 0)), _whole(gains)],
        out_specs=pl.BlockSpec((tm, d), lambda i: (i, 0)),
        out_shape=jax.ShapeDtypeStruct((m, d), BF16),
        compiler_params=_params(1),
        name="rmsnorm",
    )(x, gains)


SUB_ROWS = 512


def _row_blocks(o_ref):
    tm = o_ref.shape[0]
    sub = min(SUB_ROWS, tm // 2)
    return [pl.ds(r, sub) for r in range(0, tm, sub)]


def _cast_weights(w_refs, wbf_refs):
    @pl.when(pl.program_id(1) == 0)
    def _():
        for w_ref, wbf_ref in zip(w_refs, wbf_refs):
            wbf_ref[...] = w_ref[...].astype(BF16)


def _sigmoid(y):
    return 0.5 * jnp.tanh(0.5 * y) + 0.5


def _gelu_tanh(y):
    c = 0.7978845608028654
    half = 0.5 * y
    return half + half * jnp.tanh(y * (c + (c * 0.044715) * (y * y)))


def _in_proj_kernel(h_ref, w_ref, o_ref, wbf_ref):
    _cast_weights([w_ref], [wbf_ref])
    j = pl.program_id(0)

    def emit(act):
        for rows in _row_blocks(o_ref):
            y = jnp.dot(h_ref[rows, :], wbf_ref[...], preferred_element_type=F32)
            o_ref[rows, :] = (y if act is None else act(y)).astype(o_ref.dtype)

    pl.when(j < QKV_TILES)(lambda: emit(None))
    pl.when((j >= QKV_TILES) & (j < QKV_TILES + UV_TILES))(
        lambda: emit(_gelu_tanh))
    pl.when(j >= QKV_TILES + UV_TILES)(lambda: emit(_sigmoid))


def _in_proj(h, w, layer, tm=4096):
    m, k = h.shape
    n = w.shape[-1]
    return pl.pallas_call(
        _in_proj_kernel,
        grid=(n // TN, m // tm),
        in_specs=[pl.BlockSpec((tm, k), lambda j, i: (i, 0)),
                  pl.BlockSpec((None, k, TN), lambda j, i: (layer, 0, j))],
        out_specs=pl.BlockSpec(
            (tm, TN), lambda j, i: (i, jnp.where(j < QKV_TILES + UV_TILES, j + GATE_TILES,
                                                 j - (QKV_TILES + UV_TILES)))),
        out_shape=jax.ShapeDtypeStruct((m, n), BF16),
        scratch_shapes=[pltpu.VMEM((k, TN), BF16)],
        compiler_params=_params(2),
        name="in_proj",
    )(h, w)


def _swiglu_kernel(h_ref, wg_ref, wu_ref, wd_ref, o_ref, wd_bf_ref, wgbf_ref, wubf_ref):
    _cast_weights([wg_ref, wu_ref], [wgbf_ref, wubf_ref])
    wd_bf_ref[...] = wd_ref[...].astype(wd_bf_ref.dtype)
    for rows in _row_blocks(o_ref):
        h = h_ref[rows, :]
        g = jnp.dot(h, wgbf_ref[...], preferred_element_type=F32)
        u = jnp.dot(h, wubf_ref[...], preferred_element_type=F32)
        o_ref[rows, :] = (g * _sigmoid(g) * u).astype(o_ref.dtype)


def _swiglu(h, wg, wu, wd, layer, tm=2048):
    m, k = h.shape
    f = wg.shape[-1]
    mt = m // tm
    slab = wd.shape[1] // (f // TN * mt)
    w_spec = pl.BlockSpec((None, k, TN), lambda j, i: (layer, 0, j))
    return pl.pallas_call(
        _swiglu_kernel,
        grid=(f // TN, mt),
        in_specs=[pl.BlockSpec((tm, k), lambda j, i: (i, 0)), w_spec, w_spec,
                  pl.BlockSpec((None, slab, wd.shape[2]), lambda j, i: (layer, j * mt + i, 0))],
        out_specs=[pl.BlockSpec((tm, TN), lambda j, i: (i, j)),
                   pl.BlockSpec((slab, wd.shape[2]), lambda j, i: (j * mt + i, 0))],
        out_shape=[jax.ShapeDtypeStruct((m, f), BF16), jax.ShapeDtypeStruct(wd.shape[1:], BF16)],
        scratch_shapes=[pltpu.VMEM((k, TN), BF16), pltpu.VMEM((k, TN), BF16)],
        compiler_params=_params(2),
        name="swiglu",
    )(h, wg, wu, wd)


def _mix_out_kernel(a_ref, b_ref, ga_ref, gb_ref, x_ref, wa_ref, wb_ref, wo_ref, gain_ref,
                    o_ref, h_ref, merged_ref, *, layer):
    tm, d = o_ref.shape
    tiles = [slice(t * TN, (t + 1) * TN) for t in range(d // TN)]
    halves = [pl.ds(r, tm // 2) for r in (0, tm // 2)]
    for rows in halves:
        for cols in tiles:
            ya = jnp.dot(a_ref[rows, :], wa_ref[:, cols], preferred_element_type=F32)
            yb = jnp.dot(b_ref[rows, :], wb_ref[:, cols], preferred_element_type=F32)
            merged_ref[rows, cols] = (ga_ref[rows, cols] * ya + gb_ref[rows, cols] * yb
                                      ).astype(merged_ref.dtype)
    for rows in halves:
        ssq = jnp.zeros((tm // 2, 1), F32)
        for cols in tiles:
            y = x_ref[rows, cols] + jnp.dot(merged_ref[rows, :], wo_ref[:, cols],
                                            preferred_element_type=F32)
            o_ref[rows, cols] = y
            ssq = ssq + jnp.sum(y * y, axis=-1, keepdims=True)
        scale = lax.rsqrt(ssq * (1.0 / d) + EPS)
        h_ref[rows, :] = (o_ref[rows, :] * scale * _layer_row(gain_ref, layer)).astype(h_ref.dtype)


def _mix_out(a, b, proj, x, wa_bf, wb_bf, wo_bf, gains, layer, tm=512):
    m, k = a.shape
    d = x.shape[1]
    row_tile = pl.BlockSpec((tm, d), lambda i: (i, 0))
    return pl.pallas_call(
        functools.partial(_mix_out_kernel, layer=layer),
        grid=(m // tm,),
        in_specs=[pl.BlockSpec((tm, k), lambda i: (i, 0)),
                  pl.BlockSpec((tm, k), lambda i: (i, 0)),
                  pl.BlockSpec((tm, d), lambda i: (i, 0)),
                  pl.BlockSpec((tm, d), lambda i: (i, 1)),
                  row_tile, _whole(wa_bf), _whole(wb_bf), _whole(wo_bf), _whole(gains)],
        out_specs=[row_tile, row_tile],
        out_shape=[jax.ShapeDtypeStruct((m, d), F32), jax.ShapeDtypeStruct((m, d), BF16)],
        scratch_shapes=[pltpu.VMEM((tm, d), BF16)],
        compiler_params=_params(1),
        name="mix_out",
    )(a, b, proj, proj, x, wa_bf, wb_bf, wo_bf, gains)


def _down_out_kernel(act_ref, x_ref, w_ref, *rest, layer):
    gain_ref, o_ref, h_ref = rest if len(rest) == 3 else (None, rest[0], None)
    tm, d = o_ref.shape
    tiles = [slice(t * TN, (t + 1) * TN) for t in range(d // TN)]
    for rows in (pl.ds(r, tm // 2) for r in (0, tm // 2)):
        ssq = jnp.zeros((tm // 2, 1), F32)
        for cols in tiles:
            y = x_ref[rows, cols] + jnp.dot(act_ref[rows, :], w_ref[:, cols],
                                            preferred_element_type=F32)
            o_ref[rows, cols] = y
            if h_ref is not None:
                ssq = ssq + jnp.sum(y * y, axis=-1, keepdims=True)
        if h_ref is not None:
            scale = lax.rsqrt(ssq * (1.0 / d) + EPS)
            h_ref[rows, :] = (o_ref[rows, :] * scale * _layer_row(gain_ref, layer)).astype(h_ref.dtype)


def _down_out(act, x, w_bf, gains, layer, tm=512):
    m, k = act.shape
    d = x.shape[1]
    row_tile = pl.BlockSpec((tm, d), lambda i: (i, 0))
    in_specs = [pl.BlockSpec((tm, k), lambda i: (i, 0)), row_tile, _whole(w_bf)]
    operands = [act, x, w_bf]
    out_specs, out_shape = [row_tile], [jax.ShapeDtypeStruct((m, d), F32)]
    if gains is not None:
        in_specs.append(_whole(gains))
        operands.append(gains)
        out_specs.append(row_tile)
        out_shape.append(jax.ShapeDtypeStruct((m, d), BF16))
    return pl.pallas_call(
        functools.partial(_down_out_kernel, layer=layer),
        grid=(m // tm,),
        in_specs=in_specs,
        out_specs=out_specs,
        out_shape=out_shape,
        compiler_params=_params(1),
        name="down_out" if gains is not None else "down_out_last",
    )(*operands)


ATTN_BLOCKS_PER_STEP = 8


def _attn_kernel(sinks_ref, q_ref, kv_ref, cos_ref, sin_ref, cos_t_ref, sin_t_ref, gq_t_ref, gk_ref,
                 wa_ref, wb_ref, wo_ref, o_ref, wa_bf_ref, wb_bf_ref, wo_bf_ref,
                 k_prev_ref, vt_prev_ref):
    n = pl.program_id(1)
    half_dim = HEAD_DIM // 2

    for src, dst in ((wa_ref, wa_bf_ref), (wb_ref, wb_bf_ref), (wo_ref, wo_bf_ref)):
        dst[...] = src[...].astype(dst.dtype)

    @pl.when(n == 0)
    def _():
        k_prev_ref[...] = jnp.zeros_like(k_prev_ref)
        vt_prev_ref[...] = jnp.zeros_like(vt_prev_ref)

    lane = lax.broadcasted_iota(jnp.int32, (1, V7X_LANES), 1)
    even_quarter = ((lane // half_dim) % 2) == 0
    r = lax.broadcasted_iota(jnp.int32, (V7X_LANES, V7X_LANES), 0)
    c = lax.broadcasted_iota(jnp.int32, (V7X_LANES, V7X_LANES), 1)
    head_mean = jnp.where((r // HEAD_DIM) == (c // HEAD_DIM), 1.0 / HEAD_DIM, 0.0).astype(BF16)
    key = lax.broadcasted_iota(jnp.int32, (2 * WINDOW, WINDOW), 0)
    qry = lax.broadcasted_iota(jnp.int32, (2 * WINDOW, WINDOW), 1)
    diff = qry + WINDOW - key
    band = (diff >= 0) & (diff < WINDOW)
    gq_t = gq_t_ref[...]
    zeros = jnp.zeros((HEAD_DIM, WINDOW), BF16)
    n_cols = KV_WIDTH // V7X_LANES
    heads_per_col = N_Q_HEADS // n_cols
    group_w = Q_PER_KV * WINDOW
    sink_row = [jnp.full((1, WINDOW), sinks_ref[h] * LOG2_E, F32) for h in range(N_Q_HEADS)]

    def one_block(rows, k_prev, vt_prev, allowed):
        cos, sin = cos_ref[rows, :], sin_ref[rows, :]
        k_cols = []
        for col in range(n_cols):
            x = kv_ref[rows, col * V7X_LANES:(col + 1) * V7X_LANES].astype(F32)
            ms = jnp.dot((x * x).astype(BF16), head_mean, preferred_element_type=F32)
            y = x * lax.rsqrt(ms + EPS) * gk_ref[...]
            partner = jnp.where(even_quarter,
                                pltpu.roll(y, V7X_LANES - half_dim, 1),
                                pltpu.roll(y, half_dim, 1))
            k_cols.append((y * cos + partner * sin).astype(BF16))
        k_win = [jnp.concatenate([k_prev[col], k_cols[col]], axis=0) for col in range(n_cols)]
        vt_cur = kv_ref[rows, KV_WIDTH:].astype(F32).T.astype(BF16)
        vt_win = jnp.concatenate([vt_prev, vt_cur], axis=1)

        cos_t, sin_t = cos_t_ref[:, rows], sin_t_ref[:, rows]
        qt_pads = []
        for pair in range(N_Q_HEADS // 2):
            sl = slice(pair * V7X_LANES, (pair + 1) * V7X_LANES)
            qt_pair = q_ref[rows, sl].astype(F32).T
            for half in range(2):
                kv_head = (2 * pair + half) // Q_PER_KV
                x = qt_pair[half * HEAD_DIM:(half + 1) * HEAD_DIM, :]
                y = x * lax.rsqrt(jnp.mean(x * x, axis=0, keepdims=True) + EPS) * gq_t
                y1, y2 = y[:half_dim], y[half_dim:]
                qt = jnp.concatenate([y1 * cos_t - y2 * sin_t, y2 * cos_t + y1 * sin_t],
                                     axis=0).astype(BF16)
                qt_pads.append(jnp.concatenate([zeros, qt] if kv_head % 2 else [qt, zeros], axis=0))

        probs, sink_terms = [], []
        for kv_head in range(N_KV_HEADS):
            heads = range(kv_head * Q_PER_KV, (kv_head + 1) * Q_PER_KV)
            s = jnp.dot(k_win[kv_head // (N_KV_HEADS // n_cols)],
                        jnp.concatenate([qt_pads[h] for h in heads], axis=1),
                        preferred_element_type=F32)
            s = jnp.where(jnp.concatenate([allowed] * Q_PER_KV, axis=1), s, MASK_VALUE)
            sink = jnp.concatenate([sink_row[h] for h in heads], axis=1)
            m = jnp.maximum(jnp.max(s, axis=0, keepdims=True), sink)
            probs.append(jnp.exp2(s - m).astype(BF16))
            sink_terms.append(jnp.exp2(sink - m))

        ones_rows = jnp.ones((V7X_SUBLANES_BF16, 2 * WINDOW), BF16)
        for kv_head in range(N_KV_HEADS):
            vt = jnp.concatenate([vt_win[kv_head * HEAD_DIM:(kv_head + 1) * HEAD_DIM, :], ones_rows],
                                 axis=0)
            o = jnp.dot(vt, probs[kv_head], preferred_element_type=F32)
            denom = o[HEAD_DIM:HEAD_DIM + 1, :] + sink_terms[kv_head]
            o = o[:HEAD_DIM, :] * (1.0 / denom)
            for pair in range(Q_PER_KV // 2):
                o_pair = jnp.concatenate([o[:, (2 * pair) * WINDOW:(2 * pair + 1) * WINDOW],
                                          o[:, (2 * pair + 1) * WINDOW:(2 * pair + 2) * WINDOW]], axis=0)
                out_lane = (kv_head * Q_PER_KV + 2 * pair) * HEAD_DIM
                o_ref[rows, out_lane:out_lane + V7X_LANES] = o_pair.T.astype(o_ref.dtype)
        return k_cols, vt_cur

    k_prev = [k_prev_ref[col] for col in range(n_cols)]
    vt_prev = vt_prev_ref[...]
    for blk in range(o_ref.shape[0] // WINDOW):
        allowed = band & ((key >= WINDOW) | (n > 0)) if blk == 0 else band
        k_prev, vt_prev = one_block(pl.ds(blk * WINDOW, WINDOW), k_prev, vt_prev, allowed)

    for col in range(n_cols):
        k_prev_ref[col] = k_prev[col]
    vt_prev_ref[...] = vt_prev


def _attention(proj, sinks, tables, gq_t, gk, wa, wb, wo, layer, batch, seq):
    m = proj.shape[0]
    rows = ATTN_BLOCKS_PER_STEP * WINDOW
    nb = seq // rows
    steps = batch * nb

    def slab(w):
        return pl.BlockSpec((None, w.shape[1] // steps, w.shape[2]),
                            lambda b, n: (layer, b * nb + n, 0))

    def slab_out(w):
        return pl.BlockSpec((w.shape[1] // steps, w.shape[2]), lambda b, n: (b * nb + n, 0))

    q_blk = PROJ_QKV_COL // ATTN_WIDTH
    kv_blk = (PROJ_QKV_COL + ATTN_WIDTH) // (2 * KV_WIDTH)
    cos, sin_signed, cos_t, sin_t = tables
    half_dim = HEAD_DIM // 2
    row_table = pl.BlockSpec((rows, V7X_LANES), lambda b, n: (n, 0))
    col_table = pl.BlockSpec((half_dim, rows), lambda b, n: (0, n))
    return pl.pallas_call(
        _attn_kernel,
        grid=(batch, nb),
        in_specs=[pl.BlockSpec(memory_space=pltpu.SMEM),
                  pl.BlockSpec((rows, ATTN_WIDTH), lambda b, n: (b * nb + n, q_blk)),
                  pl.BlockSpec((rows, 2 * KV_WIDTH), lambda b, n: (b * nb + n, kv_blk)),
                  row_table, row_table, col_table, col_table,
                  pl.BlockSpec((None, HEAD_DIM, WINDOW), lambda b, n: (layer, 0, 0)),
                  pl.BlockSpec((None, 1, V7X_LANES), lambda b, n: (layer, 0, 0)),
                  slab(wa), slab(wb), slab(wo)],
        out_specs=[pl.BlockSpec((rows, ATTN_WIDTH), lambda b, n: (b * nb + n, 0)),
                   slab_out(wa), slab_out(wb), slab_out(wo)],
        out_shape=[jax.ShapeDtypeStruct((m, ATTN_WIDTH), BF16)]
        + [jax.ShapeDtypeStruct(w.shape[1:], BF16) for w in (wa, wb, wo)],
        scratch_shapes=[pltpu.VMEM((KV_WIDTH // V7X_LANES, WINDOW, V7X_LANES), BF16),
                        pltpu.VMEM((KV_WIDTH, WINDOW), BF16)],
        compiler_params=_params(2),
        name="swa_attention",
    )(sinks, proj, proj, cos, sin_signed, cos_t, sin_t, gq_t, gk, wa, wb, wo)


def _sgu_kernel(u0_ref, u1_ref, v0_ref, v1_ref, w_ref, bt_ref, g_ref, b_ref, o_ref, *, layer):
    r = lax.broadcasted_iota(jnp.int32, (CHUNK, CHUNK), 0)
    c = lax.broadcasted_iota(jnp.int32, (CHUNK, CHUNK), 1)
    causal = r >= c
    lane_mean = jnp.full((SGU_GROUP_DIM, SGU_GROUP_DIM), 1.0 / SGU_GROUP_DIM, BF16)
    groups_per_tile = TN // SGU_GROUP_DIM
    n_chunks = o_ref.shape[0] // CHUNK
    groups = range(SGU_GROUPS)

    def lanes(grp):
        return slice(grp * SGU_GROUP_DIM, (grp + 1) * SGU_GROUP_DIM)

    def tile_lanes(grp):
        return lanes(grp % groups_per_tile)

    vs = [(v0_ref, v1_ref)[grp // groups_per_tile][:, tile_lanes(grp)].astype(F32) for grp in groups]
    mus = [jnp.dot(v.astype(BF16), lane_mean, preferred_element_type=F32) for v in vs]
    dvs = [v - mu for v, mu in zip(vs, mus)]
    vars_ = [jnp.dot((dv * dv).astype(BF16), lane_mean, preferred_element_type=F32) for dv in dvs]
    ss = []
    for grp in groups:
        vn = (dvs[grp] * lax.rsqrt(vars_[grp] + EPS) * g_ref[layer:layer + 1, lanes(grp)]
              + b_ref[layer:layer + 1, lanes(grp)]).astype(BF16)
        w = jnp.where(causal, w_ref[grp], 0.0).astype(BF16)
        vn_chunks = [vn[ch * CHUNK:(ch + 1) * CHUNK, :] for ch in range(n_chunks)]
        ss.append(jnp.dot(w, jnp.concatenate(vn_chunks, axis=-1), preferred_element_type=F32))
    for grp in groups:
        s = ss[grp] + bt_ref[:, grp:grp + 1]
        u_ref = (u0_ref, u1_ref)[grp // groups_per_tile]
        for ch in range(n_chunks):
            u = u_ref[ch * CHUNK:(ch + 1) * CHUNK, tile_lanes(grp)].astype(F32)
            o_ref[ch * CHUNK:(ch + 1) * CHUNK, lanes(grp)] = (
                u * s[:, ch * CHUNK:(ch + 1) * CHUNK]).astype(o_ref.dtype)


def _sgu(proj, w_s, b_t, ln_g, ln_b, layer, rows=8 * CHUNK):
    m = proj.shape[0]
    u0 = PROJ_UV_TILE

    def tile(t):
        return pl.BlockSpec((rows, TN), lambda i: (i, t))

    return pl.pallas_call(
        functools.partial(_sgu_kernel, layer=layer),
        grid=(m // rows,),
        in_specs=[tile(u0), tile(u0 + 1), tile(u0 + 2), tile(u0 + 3),
                  pl.BlockSpec((None, SGU_GROUPS, CHUNK, CHUNK), lambda i: (layer, 0, 0, 0)),
                  pl.BlockSpec((None, CHUNK, SGU_GROUPS), lambda i: (layer, 0, 0)),
                  _whole(ln_g), _whole(ln_b)],
        out_specs=pl.BlockSpec((rows, SGU_WIDTH), lambda i: (i, 0)),
        out_shape=jax.ShapeDtypeStruct((m, SGU_WIDTH), BF16),
        compiler_params=_params(1),
        name="sgu",
    )(proj, proj, proj, proj, w_s, b_t, ln_g, ln_b)


def _rope_tables(seq):
    pos = jnp.arange(seq, dtype=F32)
    inv_freq = jnp.power(ROPE_THETA, -jnp.arange(0, HEAD_DIM, 2, dtype=F32) / HEAD_DIM)
    ang = pos[:, None] * inv_freq[None, :]
    cos, sin = jnp.cos(ang), jnp.sin(ang)
    reps = V7X_LANES // HEAD_DIM
    return (jnp.tile(cos, (1, 2 * reps)), jnp.tile(jnp.concatenate([-sin, sin], axis=-1), (1, reps)),
            cos.T, sin.T)


def kernel(x, mix_norm, w_in, q_norm, k_norm, sinks, sgu_ln_g, sgu_ln_b, w_spatial, b_spatial,
           w_attn_branch, w_sgu_branch, w_out, ffn_norm, w_gate, w_up, w_down):
    batch, seq, d = x.shape
    depth = w_in.shape[0]
    m = batch * seq
    assert d == D_MODEL and w_in.shape[-1] == (QKV_TILES + UV_TILES + GATE_TILES) * TN
    tables = _rope_tables(seq)
    gq_t = jnp.broadcast_to((q_norm * (HEAD_DIM ** -0.5 * LOG2_E))[:, :, None],
                            (depth, HEAD_DIM, WINDOW))
    gk = jnp.tile(k_norm, (1, V7X_LANES // HEAD_DIM)).reshape(depth, 1, V7X_LANES)
    b_t = jnp.swapaxes(b_spatial, 1, 2)

    xf = x.reshape(m, d)
    h = _rmsnorm(xf, mix_norm, 0)
    for l in range(depth):
        proj = _in_proj(h, w_in, l)
        a, wa_bf, wb_bf, wo_bf = _attention(proj, sinks[l], tables, gq_t, gk,
                                            w_attn_branch, w_sgu_branch, w_out, l, batch, seq)
        b = _sgu(proj, w_spatial, b_t, sgu_ln_g, sgu_ln_b, l)
        xf, h2 = _mix_out(a, b, proj, xf, wa_bf, wb_bf, wo_bf, ffn_norm, l)
        act, wd_bf = _swiglu(h2, w_gate, w_up, w_down, l)
        if l + 1 < depth:
            xf, h = _down_out(act, xf, wd_bf, mix_norm, l + 1)
        else:
            (xf,) = _down_out(act, xf, wd_bf, None, l)
    return xf.reshape(batch, seq, d)
```
